```python
import jax, jax.numpy as jnp
from jax import lax
import numpy as np

D_MODEL = 1024
BATCH = 4
SEQ = 4096
DEPTH = 2

N_MIXERS = 2
HEAD_DIM = 64
MEM_LEN = 256
MEM_HEADS = 4
MEM_WIDTH = MEM_HEADS * HEAD_DIM
MIX_WIDTH = D_MODEL - MEM_WIDTH
CONV_WIDTH = 31
MOBA_HEADS = MIX_WIDTH // HEAD_DIM
MOBA_BLOCK = 256
MOBA_TOPK = 3
Q_CHUNK = 64
N_GROUPS = 4
EXPERTS_PER_GROUP = 8
EXPERT_TOPK = 2
D_EXPERT = 512
LN_EPS = 1e-5
DEEPNORM_ALPHA = (2 * DEPTH) ** 0.25
DEEPNORM_BETA = (8 * DEPTH) ** -0.25
N_CONV_LAYERS = (DEPTH + 1) // 2
N_MOBA_LAYERS = DEPTH // 2

kernel_name = "hybrid_conformer_moba_memxattn_hmoe_deepnorm"


def layer_norm(x, g, b):
    xf = x.astype(jnp.float32)
    mu = jnp.mean(xf, axis=-1, keepdims=True)
    var = jnp.mean(jnp.square(xf - mu), axis=-1, keepdims=True)
    return ((xf - mu) * lax.rsqrt(var + LN_EPS) * g + b).astype(x.dtype)


def conformer_conv(u, dw_w, dw_b, ln_g, ln_b):
    a, gate = jnp.split(u, 2, axis=-1)
    h = a * jax.nn.sigmoid(gate)
    h = lax.conv_general_dilated(
        h, dw_w[:, None, :], window_strides=(1,),
        padding=[(CONV_WIDTH - 1, 0)],
        dimension_numbers=('NWC', 'WIO', 'NWC'),
        feature_group_count=MIX_WIDTH) + dw_b
    h = layer_norm(h, ln_g, ln_b)
    return jax.nn.silu(h)


def memory_attention(q, mem_k, mem_v):
    s = jnp.einsum('bshd,bmhd->bhsm', q, mem_k).astype(jnp.float32) * (HEAD_DIM ** -0.5)
    p = jax.nn.softmax(s, axis=-1).astype(q.dtype)
    return jnp.einsum('bhsm,bmhd->bshd', p, mem_v)


def moba_attention(q, k, v):
    b_, h_, s_, hd = q.shape
    n_blocks = -(-s_ // MOBA_BLOCK)
    s_pad = n_blocks * MOBA_BLOCK
    pad = ((0, 0), (0, 0), (0, s_pad - s_), (0, 0))
    q, k, v = jnp.pad(q, pad), jnp.pad(k, pad), jnp.pad(v, pad)
    k_blk = k.reshape(b_, h_, n_blocks, MOBA_BLOCK, hd)
    v_blk = v.reshape(b_, h_, n_blocks, MOBA_BLOCK, hd)
    k_mean = jnp.mean(k_blk.astype(jnp.float32), axis=3)
    n_sel = min(MOBA_TOPK, n_blocks - 1)
    scale = hd ** -0.5
    b_idx = jnp.arange(b_)[:, None, None, None]
    h_idx = jnp.arange(h_)[None, :, None, None]

    def chunk(ci):
        start = ci * Q_CHUNK
        q_c = lax.dynamic_slice_in_dim(q, start, Q_CHUNK, axis=2)
        own = start // MOBA_BLOCK
        q_pos = start + jnp.arange(Q_CHUNK)
        k_own = lax.dynamic_index_in_dim(k_blk, own, axis=2, keepdims=False)
        v_own = lax.dynamic_index_in_dim(v_blk, own, axis=2, keepdims=False)
        k_pos = own * MOBA_BLOCK + jnp.arange(MOBA_BLOCK)
        s_own = jnp.einsum('bhqd,bhkd->bhqk', q_c, k_own).astype(jnp.float32) * scale
        s_own = jnp.where(k_pos[None, :] <= q_pos[:, None], s_own, -jnp.inf)
        if n_sel == 0:
            p = jax.nn.softmax(s_own, axis=-1).astype(v.dtype)
            return jnp.einsum('bhqk,bhkd->bhqd', p, v_own)
        gate = jnp.einsum('bhqd,bhnd->bhqn', q_c.astype(jnp.float32), k_mean)
        gate = jnp.where(jnp.arange(n_blocks) < own, gate, -jnp.inf)
        _, idx = lax.top_k(gate, n_sel)
        valid = idx < own
        k_sel = k_blk[b_idx, h_idx, idx]
        v_sel = v_blk[b_idx, h_idx, idx]
        s_sel = jnp.einsum('bhqd,bhqnkd->bhqnk', q_c, k_sel).astype(jnp.float32) * scale
        s_sel = jnp.where(valid[..., None], s_sel, -jnp.inf)
        s_sel = s_sel.reshape(b_, h_, Q_CHUNK, n_sel * MOBA_BLOCK)
        p = jax.nn.softmax(jnp.concatenate([s_own, s_sel], axis=-1), axis=-1).astype(v.dtype)
        p_own = p[..., :MOBA_BLOCK]
        p_sel = p[..., MOBA_BLOCK:].reshape(b_, h_, Q_CHUNK, n_sel, MOBA_BLOCK)
        return (jnp.einsum('bhqk,bhkd->bhqd', p_own, v_own)
                + jnp.einsum('bhqnk,bhqnkd->bhqd', p_sel, v_sel))

    outs = lax.map(chunk, jnp.arange(s_pad // Q_CHUNK))
    out = jnp.moveaxis(outs, 0, 2).reshape(b_, h_, s_pad, hd)
    return out[:, :, :s_]


def hierarchical_moe(x, w_rg, b_rg, w_re, b_re, w_gate, w_up, w_down):
    b_, s_, d_ = x.shape
    t = x.reshape(-1, d_)
    n_tok = t.shape[0]
    rows = jnp.arange(n_tok)
    g_logits = jnp.dot(t, w_rg).astype(jnp.float32) + b_rg
    g_prob = jax.nn.softmax(g_logits, axis=-1)
    g_idx = jnp.argmax(g_logits, axis=-1)
    g_w = g_prob[rows, g_idx]
    e_logits = (jnp.dot(t, w_re).astype(jnp.float32) + b_re).reshape(n_tok, N_GROUPS, EXPERTS_PER_GROUP)
    e_sel = e_logits[rows, g_idx]
    top_v, top_i = lax.top_k(e_sel, EXPERT_TOPK)
    top_w = jax.nn.softmax(top_v, axis=-1) * g_w[:, None]
    within = jnp.sum(jax.nn.one_hot(top_i, EXPERTS_PER_GROUP, dtype=jnp.float32) * top_w[..., None], axis=1)
    combine = (jax.nn.one_hot(g_idx, N_GROUPS, dtype=jnp.float32)[:, :, None]
               * within[:, None, :]).astype(t.dtype)
    out = jnp.zeros_like(t)
    for g in range(N_GROUPS):
        h = (jax.nn.silu(jnp.einsum('td,edf->tef', t, w_gate[g]))
             * jnp.einsum('td,edf->tef', t, w_up[g]))
        out = out + jnp.einsum('tef,efd->td', h * combine[:, g, :, None], w_down[g])
    return out.reshape(b_, s_, d_)


def setup_inputs(seed: int = 0) -> dict:
    key = jax.random.key(seed)
    ks = jax.random.split(key, 24)
    f32 = jnp.float32
    nrm = lambda k, shape, scale: jax.random.normal(k, shape, f32) * scale
    na, nb, L = N_CONV_LAYERS, N_MOBA_LAYERS, DEPTH
    G, E, F, D = N_GROUPS, EXPERTS_PER_GROUP, D_EXPERT, D_MODEL
    return {
        "x": nrm(ks[0], (BATCH, SEQ, D), 1.0),
        "mem": nrm(ks[1], (BATCH, MEM_LEN, D), 1.0),
        "w_mem_kv": nrm(ks[2], (D, 2 * MEM_WIDTH), D ** -0.5),
        "conv_w_in": nrm(ks[3], (na, D, 2 * MIX_WIDTH + MEM_WIDTH), D ** -0.5),
        "conv_dw_w": nrm(ks[4], (na, CONV_WIDTH, MIX_WIDTH), CONV_WIDTH ** -0.5),
        "conv_dw_b": nrm(ks[5], (na, MIX_WIDTH), 0.02),
        "conv_ln_g": 1.0 + nrm(ks[6], (na, MIX_WIDTH), 0.02),
        "conv_ln_b": nrm(ks[7], (na, MIX_WIDTH), 0.02),
        "moba_w_in": nrm(ks[8], (nb, D, 3 * MIX_WIDTH + MEM_WIDTH), D ** -0.5),
        "w_o": nrm(ks[9], (L, D, D), D ** -0.5 * DEEPNORM_BETA),
        "ln1_g": 1.0 + nrm(ks[10], (L, D), 0.02),
        "ln1_b": nrm(ks[11], (L, D), 0.02),
        "w_rg": nrm(ks[12], (L, D, G), D ** -0.5),
        "b_rg": nrm(ks[13], (L, G), 0.01),
        "w_re": nrm(ks[14], (L, D, G * E), D ** -0.5),
        "b_re": nrm(ks[15], (L, G * E), 0.01),
        "w_gate": nrm(ks[16], (L, G, E, D, F), D ** -0.5),
        "w_up": nrm(ks[17], (L, G, E, D, F), D ** -0.5),
        "w_down": nrm(ks[18], (L, G, E, F, D), F ** -0.5 * DEEPNORM_BETA),
        "ln2_g": 1.0 + nrm(ks[19], (L, D), 0.02),
        "ln2_b": nrm(ks[20], (L, D), 0.02),
    }


def reference(x, mem, w_mem_kv, conv_w_in, conv_dw_w, conv_dw_b, conv_ln_g, conv_ln_b,
              moba_w_in, w_o, ln1_g, ln1_b, w_rg, b_rg, w_re, b_re, w_gate, w_up, w_down,
              ln2_g, ln2_b):
    b_, s_, _ = x.shape
    mem_kv = jnp.dot(mem, w_mem_kv).reshape(b_, MEM_LEN, 2, MEM_HEADS, HEAD_DIM)
    mem_k, mem_v = mem_kv[:, :, 0], mem_kv[:, :, 1]
    for i in range(DEPTH):
        j = i // N_MIXERS
        if i % N_MIXERS == 0:
            u = jnp.dot(x, conv_w_in[j])
            y_mix = conformer_conv(u[..., :2 * MIX_WIDTH], conv_dw_w[j], conv_dw_b[j],
                                   conv_ln_g[j], conv_ln_b[j])
            q_mem = u[..., 2 * MIX_WIDTH:]
        else:
            u = jnp.dot(x, moba_w_in[j])
            qkv = u[..., :3 * MIX_WIDTH].reshape(b_, s_, 3, MOBA_HEADS, HEAD_DIM)
            qkv = jnp.transpose(qkv, (2, 0, 3, 1, 4))
            y_mix = moba_attention(qkv[0], qkv[1], qkv[2])
            y_mix = jnp.transpose(y_mix, (0, 2, 1, 3)).reshape(b_, s_, MIX_WIDTH)
            q_mem = u[..., 3 * MIX_WIDTH:]
        y_mem = memory_attention(q_mem.reshape(b_, s_, MEM_HEADS, HEAD_DIM), mem_k, mem_v)
        y_mem = y_mem.reshape(b_, s_, MEM_WIDTH)
        y = jnp.dot(jnp.concatenate([y_mix, y_mem], axis=-1), w_o[i])
        x = layer_norm(DEEPNORM_ALPHA * x + y, ln1_g[i], ln1_b[i])
        f = hierarchical_moe(x, w_rg[i], b_rg[i], w_re[i], b_re[i], w_gate[i], w_up[i], w_down[i])
        x = layer_norm(DEEPNORM_ALPHA * x + f, ln2_g[i], ln2_b[i])
    return x
```

```python
import functools

import jax
import jax.numpy as jnp
from jax import lax
from jax.experimental import pallas as pl
from jax.experimental.pallas import tpu as pltpu

F32 = jnp.float32
BF16 = jnp.bfloat16

HEAD_DIM = 64
MEM_LEN = 256
MEM_HEADS = 4
MEM_WIDTH = MEM_HEADS * HEAD_DIM
CONV_WIDTH = 31
MOBA_BLOCK = 256
MOBA_TOPK = 3
N_GROUPS = 4
EXPERTS_PER_GROUP = 8
N_EXPERTS = N_GROUPS * EXPERTS_PER_GROUP
LN_EPS = 1e-5

LANES = 128
SUBLANES = 8
TOKEN_TILE = 256
CONV_HALO = 32
CONV_CHUNK = 32
ROUTER_LANES = 128
ROUTER_OFF = N_GROUPS
VMEM_LIMIT = 56 * 1024 * 1024

NEG_INF = float("-inf")


def _ln(z, g, b):
    mu = jnp.mean(z, axis=-1, keepdims=True)
    zc = z - mu
    var = jnp.mean(zc * zc, axis=-1, keepdims=True)
    return zc * lax.rsqrt(var + LN_EPS) * g + b


def _dot(a, b):
    return jnp.dot(a, b, preferred_element_type=F32)


def _dot_nt(a, b):
    return lax.dot_general(a, b, (((1,), (1,)), ((), ())), preferred_element_type=F32)


def _split_bf16(x):
    hi = x.astype(BF16)
    lo = (x - hi.astype(F32)).astype(BF16)
    return hi, lo


def _memkv_kernel(mem_ref, w_ref, o_ref):
    o_ref[...] = _dot(mem_ref[...].astype(BF16), w_ref[...].astype(BF16))


def _memkv(mem2d, w):
    return pl.pallas_call(
        _memkv_kernel,
        out_shape=jax.ShapeDtypeStruct((mem2d.shape[0], w.shape[1]), F32),
        name="memkv",
    )(mem2d, w)


def _mem_attention(qm, kbd, vbd):
    s = _dot(qm.astype(BF16), kbd) * (HEAD_DIM ** -0.5)
    parts = []
    for h in range(MEM_HEADS):
        seg = s[:, h * MEM_LEN:(h + 1) * MEM_LEN]
        m = jnp.max(seg, axis=-1, keepdims=True)
        e = jnp.exp(seg - m)
        parts.append(e / jnp.sum(e, axis=-1, keepdims=True))
    p = jnp.concatenate(parts, axis=-1)
    return _dot(p.astype(BF16), vbd)


def _out_proj_ln(x, y_mix, y_mem, wo_ref, g_ref, b_ref, alpha, mix_w):
    y = _dot(y_mix.astype(BF16), wo_ref[0:mix_w, :]) + _dot(y_mem.astype(BF16), wo_ref[mix_w:, :])
    return _ln(alpha * x + y, g_ref[...], b_ref[...])


def _conv_mixer_kernel(x_ref, win_ref, dww_ref, dwb_ref, cg_ref, cb_ref, kbd_ref, vbd_ref,
                       wo_ref, g1_ref, b1_ref, o_ref, hbuf, cbuf, *, alpha, mix_w):
    tm = x_ref.shape[0]
    j = pl.program_id(1)
    x = x_ref[...]
    u = _dot(x.astype(BF16), win_ref[...])
    a = u[:, 0:mix_w]
    gate = u[:, mix_w:2 * mix_w]
    qm = u[:, 2 * mix_w:]
    h = a * jax.nn.sigmoid(gate)

    @pl.when(j == 0)
    def _():
        hbuf[0:CONV_HALO, :] = jnp.zeros((CONV_HALO, mix_w), F32)

    hbuf[CONV_HALO:CONV_HALO + tm, :] = h

    first = CONV_HALO - (CONV_WIDTH - 1)
    for c in range(tm // CONV_CHUNK):
        r0 = c * CONV_CHUNK
        acc = jnp.broadcast_to(dwb_ref[...], (CONV_CHUNK, mix_w))
        for al in range(SUBLANES):
            taps = [k for k in range(CONV_WIDTH) if (first + k) % SUBLANES == al]
            if not taps:
                continue
            span = (first + taps[-1]) // SUBLANES * SUBLANES + CONV_CHUNK
            z = hbuf[r0 + al:r0 + al + span, :]
            for k in taps:
                m8 = (first + k) // SUBLANES * SUBLANES
                acc = acc + dww_ref[k:k + 1, :] * z[m8:m8 + CONV_CHUNK, :]
        cbuf[r0:r0 + CONV_CHUNK, :] = acc

    hbuf[0:CONV_HALO, :] = hbuf[tm:tm + CONV_HALO, :]

    cn = _ln(cbuf[...], cg_ref[...], cb_ref[...])
    y_mix = cn * jax.nn.sigmoid(cn)
    y_mem = _mem_attention(qm, kbd_ref[...], vbd_ref[...])
    o_ref[...] = _out_proj_ln(x, y_mix, y_mem, wo_ref, g1_ref, b1_ref, alpha, mix_w)


def _conv_mixer(x2d, batch, w_in, dw_w, dw_b, cg, cb, kbd, vbd, wo, g1, b1, alpha):
    t, d = x2d.shape
    tm = TOKEN_TILE
    nj = t // batch // tm
    mix_w = d - MEM_WIDTH
    full = lambda shape: pl.BlockSpec(shape, lambda b, j: (0,) * len(shape))
    return pl.pallas_call(
        functools.partial(_conv_mixer_kernel, alpha=alpha, mix_w=mix_w),
        out_shape=jax.ShapeDtypeStruct((t, d), F32),
        grid=(batch, nj),
        in_specs=[
            pl.BlockSpec((tm, d), lambda b, j: (b * nj + j, 0)),
            full(w_in.shape), full(dw_w.shape), full(dw_b.shape), full(cg.shape), full(cb.shape),
            pl.BlockSpec((None,) + kbd.shape[1:], lambda b, j: (b, 0, 0)),
            pl.BlockSpec((None,) + vbd.shape[1:], lambda b, j: (b, 0, 0)),
            full(wo.shape), full(g1.shape), full(b1.shape),
        ],
        out_specs=pl.BlockSpec((tm, d), lambda b, j: (b * nj + j, 0)),
        scratch_shapes=[pltpu.VMEM((CONV_HALO + tm, mix_w), F32), pltpu.VMEM((tm, mix_w), F32)],
        compiler_params=pltpu.CompilerParams(
            dimension_semantics=("arbitrary", "arbitrary"), vmem_limit_bytes=VMEM_LIMIT),
        name="conv_mixer",
    )(x2d, w_in, dw_w, dw_b, cg, cb, kbd, vbd, wo, g1, b1)


def _moba_select_bias(gt, i, nb):
    blk = lax.broadcasted_iota(jnp.int32, gt.shape, 0)
    past = blk < i
    gm = jnp.where(past, gt, NEG_INF)
    cnt = jnp.zeros(gt.shape, F32)
    for n in range(nb):
        row = gm[n:n + 1, :]
        beats = (row > gm) | ((row == gm) & (blk > n))
        cnt = cnt + jnp.where(beats & (i > n), 1.0, 0.0)
    return jnp.where(past & (cnt < MOBA_TOPK), 0.0, NEG_INF)


def _moba_mixer_kernel(x_ref, win_ref, kbd_ref, vbd_ref, wo_ref, g1_ref, b1_ref, o_ref,
                       k_sc, vt_sc, kmt_sc, bias_sc, yt_sc, *, alpha, mix_w, nb):
    tm = x_ref.shape[0]
    heads = mix_w // HEAD_DIM
    i = pl.program_id(1)
    x = x_ref[...]
    u = _dot(x.astype(BF16), win_ref[...])
    q = u[:, 0:mix_w] * (HEAD_DIM ** -0.5)
    k = u[:, mix_w:2 * mix_w]
    v = u[:, 2 * mix_w:3 * mix_w]
    qm = u[:, 3 * mix_w:]

    @pl.when(i == 0)
    def _():
        kmt_sc[...] = jnp.zeros(kmt_sc.shape, F32)

    k_sc[i] = k.astype(BF16)
    vt_sc[i] = v.T.astype(BF16)
    kmean = jnp.mean(k, axis=0, keepdims=True)
    lane = lax.broadcasted_iota(jnp.int32, (1, mix_w), 1)
    for h in range(heads):
        in_head = (lane >= h * HEAD_DIM) & (lane < (h + 1) * HEAD_DIM)
        kmt_sc[pl.ds(h * nb + i, 1), :] = jnp.where(in_head, kmean, 0.0)

    q_hi, q_lo = _split_bf16(q)
    km_hi, km_lo = _split_bf16(kmt_sc[...])
    gate_t = _dot_nt(km_hi, q_hi) + _dot_nt(km_hi, q_lo) + _dot_nt(km_lo, q_hi)
    for h in range(heads):
        bias_sc[h] = _moba_select_bias(gate_t[h * nb:(h + 1) * nb, :], i, nb)

    kidx = lax.broadcasted_iota(jnp.int32, (tm, tm), 0)
    qidx = lax.broadcasted_iota(jnp.int32, (tm, tm), 1)
    causal = jnp.where(kidx <= qidx, 0.0, NEG_INF)
    lane_p = lax.broadcasted_iota(jnp.int32, (tm, 2 * HEAD_DIM), 1)

    for p in range(heads // 2):
        cols = slice(2 * p * HEAD_DIM, 2 * (p + 1) * HEAD_DIM)
        qp = q_hi[:, cols]
        q_heads = (jnp.where(lane_p < HEAD_DIM, qp, 0).astype(BF16),
                   jnp.where(lane_p >= HEAD_DIM, qp, 0).astype(BF16))

        def scores(j, hh):
            return _dot_nt(k_sc[j, :, cols], q_heads[hh])

        def values_t(j, hh):
            r0 = (2 * p + hh) * HEAD_DIM
            return vt_sc[j, r0:r0 + HEAD_DIM, :]

        carry = []
        for hh in range(2):
            s = scores(i, hh) + causal
            m = jnp.max(s, axis=0, keepdims=True)
            e = jnp.exp(s - m)
            l = jnp.sum(e, axis=0, keepdims=True)
            acc = _dot(values_t(i, hh), e.astype(BF16))
            carry += [m, l, acc]

        def body(j, c):
            out = []
            for hh in range(2):
                m, l, acc = c[3 * hh:3 * hh + 3]
                s = scores(j, hh) + bias_sc[2 * p + hh, pl.ds(j, 1), :]
                m_new = jnp.maximum(m, jnp.max(s, axis=0, keepdims=True))
                corr = jnp.exp(m - m_new)
                e = jnp.exp(s - m_new)
                l = corr * l + jnp.sum(e, axis=0, keepdims=True)
                acc = corr * acc + _dot(values_t(j, hh), e.astype(BF16))
                out += [m_new, l, acc]
            return tuple(out)

        carry = lax.fori_loop(0, i, body, tuple(carry))
        for hh in range(2):
            r0 = (2 * p + hh) * HEAD_DIM
            yt_sc[r0:r0 + HEAD_DIM, :] = carry[3 * hh + 2] / carry[3 * hh + 1]

    y_mix = yt_sc[...].T
    y_mem = _mem_attention(qm, kbd_ref[...], vbd_ref[...])
    o_ref[...] = _out_proj_ln(x, y_mix, y_mem, wo_ref, g1_ref, b1_ref, alpha, mix_w)


def _moba_mixer(x2d, batch, w_in, kbd, vbd, wo, g1, b1, alpha):
    t, d = x2d.shape
    tm = MOBA_BLOCK
    nb = t // batch // tm
    mix_w = d - MEM_WIDTH
    heads = mix_w // HEAD_DIM
    full = lambda shape: pl.BlockSpec(shape, lambda b, j: (0,) * len(shape))
    return pl.pallas_call(
        functools.partial(_moba_mixer_kernel, alpha=alpha, mix_w=mix_w, nb=nb),
        out_shape=jax.ShapeDtypeStruct((t, d), F32),
        grid=(batch, nb),
        in_specs=[
            pl.BlockSpec((tm, d), lambda b, j: (b * nb + j, 0)),
            full(w_in.shape),
            pl.BlockSpec((None,) + kbd.shape[1:], lambda b, j: (b, 0, 0)),
            pl.BlockSpec((None,) + vbd.shape[1:], lambda b, j: (b, 0, 0)),
            full(wo.shape), full(g1.shape), full(b1.shape),
        ],
        out_specs=pl.BlockSpec((tm, d), lambda b, j: (b * nb + j, 0)),
        scratch_shapes=[
            pltpu.VMEM((nb, tm, mix_w), BF16),
            pltpu.VMEM((nb, mix_w, tm), BF16),
            pltpu.VMEM((heads * nb, mix_w), F32),
            pltpu.VMEM((heads, nb, tm), F32),
            pltpu.VMEM((mix_w, tm), F32),
        ],
        compiler_params=pltpu.CompilerParams(
            dimension_semantics=("arbitrary", "arbitrary"), vmem_limit_bytes=VMEM_LIMIT),
        name="moba_mixer",
    )(x2d, w_in, kbd, vbd, wo, g1, b1)


def _router_kernel(x_ref, whi_ref, wlo_ref, br_ref, tri_ref, info_ref, cnt_ref, run_sc):
    step = pl.program_id(0)

    @pl.when(step == 0)
    def _():
        run_sc[...] = jnp.zeros(run_sc.shape, F32)

    x_hi, x_lo = _split_bf16(x_ref[...])
    logits = _dot(x_hi, whi_ref[...]) + _dot(x_hi, wlo_ref[...]) + _dot(x_lo, whi_ref[...]) + br_ref[...]
    lane = lax.broadcasted_iota(jnp.int32, logits.shape, 1).astype(F32)
    rmax = lambda a: jnp.max(a, axis=-1, keepdims=True)
    rmin = lambda a: jnp.min(a, axis=-1, keepdims=True)
    rsum = lambda a: jnp.sum(a, axis=-1, keepdims=True)
    big = float(2 * ROUTER_LANES)

    is_g = lane < N_GROUPS
    gl = jnp.where(is_g, logits, NEG_INF)
    gmax = rmax(gl)
    gidx = rmin(jnp.where(gl == gmax, lane, big))
    g_w = 1.0 / rsum(jnp.where(is_g, jnp.exp(gl - gmax), 0.0))

    lo = ROUTER_OFF + EXPERTS_PER_GROUP * gidx
    el = jnp.where((lane >= lo) & (lane < lo + EXPERTS_PER_GROUP), logits, NEG_INF)
    v0 = rmax(el)
    i0 = rmin(jnp.where(el == v0, lane, big))
    el1 = jnp.where(lane == i0, NEG_INF, el)
    v1 = rmax(el1)
    i1 = rmin(jnp.where(el1 == v1, lane, big))
    t = jnp.exp(v1 - v0)
    w0 = g_w / (1.0 + t)
    w1 = g_w * t / (1.0 + t)

    pick0 = lane == i0
    pick1 = lane == i1
    onehot = jnp.where(pick0 | pick1, 1.0, 0.0)
    before = _dot(tri_ref[...], onehot.astype(BF16)) + run_sc[0:1, :]
    pos0 = rsum(jnp.where(pick0, before, 0.0))
    pos1 = rsum(jnp.where(pick1, before, 0.0))
    run_sc[...] = run_sc[...] + jnp.sum(onehot, axis=0, keepdims=True)

    vals = (i0 - ROUTER_OFF, i1 - ROUTER_OFF, pos0, pos1, w0, w1)
    info = jnp.zeros(logits.shape, F32)
    for c, val in enumerate(vals):
        info = jnp.where(lane == c, val, info)
    info_ref[...] = info
    cnt_ref[...] = run_sc[...]


def _router(x2d, whi, wlo, br, tri):
    t, d = x2d.shape
    tm = TOKEN_TILE
    full = lambda shape: pl.BlockSpec(shape, lambda s: (0,) * len(shape))
    return pl.pallas_call(
        _router_kernel,
        out_shape=(jax.ShapeDtypeStruct((t, ROUTER_LANES), F32),
                   jax.ShapeDtypeStruct((SUBLANES, ROUTER_LANES), F32)),
        grid=(t // tm,),
        in_specs=[pl.BlockSpec((tm, d), lambda s: (s, 0)),
                  full(whi.shape), full(wlo.shape), full(br.shape), full(tri.shape)],
        out_specs=(pl.BlockSpec((tm, ROUTER_LANES), lambda s: (s, 0)),
                   full((SUBLANES, ROUTER_LANES))),
        scratch_shapes=[pltpu.VMEM((SUBLANES, ROUTER_LANES), F32)],
        compiler_params=pltpu.CompilerParams(dimension_semantics=("arbitrary",)),
        name="router",
    )(x2d, whi, wlo, br, tri)


def _row_copy(src, src_row, dst, dst_row, sem):
    return pltpu.make_async_copy(src.at[pl.ds(src_row, 1), :], dst.at[pl.ds(dst_row, 1), :], sem)


def _dispatch_kernel(s0_ref, s1_ref, x_ref, xs_in_ref, xs_ref, sem):
    del xs_in_ref
    tm = x_ref.shape[0]

    def start(r, c):
        _row_copy(x_ref, r, xs_ref, s0_ref[0, r], sem).start()
        _row_copy(x_ref, r, xs_ref, s1_ref[0, r], sem).start()
        return c

    def wait(r, c):
        _row_copy(x_ref, r, xs_ref, s0_ref[0, r], sem).wait()
        _row_copy(x_ref, r, xs_ref, s1_ref[0, r], sem).wait()
        return c

    lax.fori_loop(0, tm, start, 0)
    lax.fori_loop(0, tm, wait, 0)


def _dispatch(x2d, slot0, slot1, n_slots):
    t, d = x2d.shape
    tm = TOKEN_TILE
    smem_rows = pl.BlockSpec((None, 1, tm), lambda s: (s, 0, 0), memory_space=pltpu.SMEM)
    zeros = jnp.zeros((n_slots, d), F32)
    return pl.pallas_call(
        _dispatch_kernel,
        out_shape=jax.ShapeDtypeStruct((n_slots, d), F32),
        grid=(t // tm,),
        in_specs=[smem_rows, smem_rows,
                  pl.BlockSpec((tm, d), lambda s: (s, 0)),
                  pl.BlockSpec(memory_space=pl.ANY)],
        out_specs=pl.BlockSpec(memory_space=pl.ANY),
        scratch_shapes=[pltpu.SemaphoreType.DMA],
        input_output_aliases={3: 0},
        compiler_params=pltpu.CompilerParams(dimension_semantics=("arbitrary",)),
        name="dispatch",
    )(slot0.reshape(t // tm, 1, tm), slot1.reshape(t // tm, 1, tm), x2d, zeros)


def _expert_kernel(te_ref, nu_ref, xs_ref, wg_ref, wu_ref, wd_ref, ys_ref, wg_sc, wu_sc, wd_sc):
    t = pl.program_id(0)
    used = t < nu_ref[0]
    prev = te_ref[jnp.maximum(t - 1, 0)]

    @pl.when(used & ((t == 0) | (te_ref[t] != prev)))
    def _():
        wg_sc[...] = wg_ref[...].astype(BF16)
        wu_sc[...] = wu_ref[...].astype(BF16)
        wd_sc[...] = wd_ref[...].astype(BF16)

    @pl.when(used)
    def _():
        xb = xs_ref[...].astype(BF16)
        hg = _dot(xb, wg_sc[...])
        hu = _dot(xb, wu_sc[...])
        h = hg * jax.nn.sigmoid(hg) * hu
        ys_ref[...] = _dot(h.astype(BF16), wd_sc[...])

    @pl.when(jnp.logical_not(used))
    def _():
        ys_ref[...] = jnp.zeros(ys_ref.shape, F32)


def _expert_mlp(xs, tile_expert, n_used, w_gate, w_up, w_down, layer):
    ns, d = xs.shape
    f = w_gate.shape[-1]
    tm = TOKEN_TILE
    base = layer * N_EXPERTS
    return pl.pallas_call(
        _expert_kernel,
        out_shape=jax.ShapeDtypeStruct((ns, d), F32),
        grid_spec=pltpu.PrefetchScalarGridSpec(
            num_scalar_prefetch=2,
            grid=(ns // tm,),
            in_specs=[
                pl.BlockSpec((tm, d), lambda t, te, nu: (t, 0)),
                pl.BlockSpec((None, d, f), lambda t, te, nu: (base + te[t], 0, 0)),
                pl.BlockSpec((None, d, f), lambda t, te, nu: (base + te[t], 0, 0)),
                pl.BlockSpec((None, f, d), lambda t, te, nu: (base + te[t], 0, 0)),
            ],
            out_specs=pl.BlockSpec((tm, d), lambda t, te, nu: (t, 0)),
            scratch_shapes=[pltpu.VMEM((d, f), BF16), pltpu.VMEM((d, f), BF16), pltpu.VMEM((f, d), BF16)],
        ),
        compiler_params=pltpu.CompilerParams(
            dimension_semantics=("arbitrary",), vmem_limit_bytes=VMEM_LIMIT),
        name="expert_mlp",
    )(tile_expert, n_used, xs, w_gate, w_up, w_down)


def _combine_kernel(s0_ref, s1_ref, info_ref, x_ref, ys_ref, g_ref, b_ref, o_ref, y0_sc, y1_sc, sem,
                    *, alpha):
    tm = x_ref.shape[0]

    def start(r, c):
        _row_copy(ys_ref, s0_ref[0, r], y0_sc, r, sem).start()
        _row_copy(ys_ref, s1_ref[0, r], y1_sc, r, sem).start()
        return c

    def wait(r, c):
        _row_copy(ys_ref, s0_ref[0, r], y0_sc, r, sem).wait()
        _row_copy(ys_ref, s1_ref[0, r], y1_sc, r, sem).wait()
        return c

    lax.fori_loop(0, tm, start, 0)
    lax.fori_loop(0, tm, wait, 0)
    info = info_ref[...]
    f = info[:, 4:5] * y0_sc[...] + info[:, 5:6] * y1_sc[...]
    o_ref[...] = _ln(alpha * x_ref[...] + f, g_ref[...], b_ref[...])


def _combine(x2d, info, ys, slot0, slot1, g2, b2, alpha):
    t, d = x2d.shape
    tm = TOKEN_TILE
    smem_rows = pl.BlockSpec((None, 1, tm), lambda s: (s, 0, 0), memory_space=pltpu.SMEM)
    full = lambda shape: pl.BlockSpec(shape, lambda s: (0,) * len(shape))
    return pl.pallas_call(
        functools.partial(_combine_kernel, alpha=alpha),
        out_shape=jax.ShapeDtypeStruct((t, d), F32),
        grid=(t // tm,),
        in_specs=[smem_rows, smem_rows,
                  pl.BlockSpec((tm, ROUTER_LANES), lambda s: (s, 0)),
                  pl.BlockSpec((tm, d), lambda s: (s, 0)),
                  pl.BlockSpec(memory_space=pl.ANY),
                  full(g2.shape), full(b2.shape)],
        out_specs=pl.BlockSpec((tm, d), lambda s: (s, 0)),
        scratch_shapes=[pltpu.VMEM((tm, d), F32), pltpu.VMEM((tm, d), F32), pltpu.SemaphoreType.DMA],
        compiler_params=pltpu.CompilerParams(dimension_semantics=("arbitrary",)),
        name="combine",
    )(slot0.reshape(t // tm, 1, tm), slot1.reshape(t // tm, 1, tm), info, x2d, ys, g2, b2)


def _moe(x2d, layer, w_rg, b_rg, w_re, b_re, w_gate, w_up, w_down, g2, b2, alpha):
    t, d = x2d.shape
    tm = TOKEN_TILE
    wr = jnp.zeros((d, ROUTER_LANES), F32).at[:, 0:N_GROUPS].set(w_rg)
    wr = wr.at[:, ROUTER_OFF:ROUTER_OFF + N_EXPERTS].set(w_re)
    br = jnp.zeros((1, ROUTER_LANES), F32).at[0, 0:N_GROUPS].set(b_rg)
    br = br.at[0, ROUTER_OFF:ROUTER_OFF + N_EXPERTS].set(b_re)
    whi, wlo = _split_bf16(wr)
    ridx = lax.broadcasted_iota(jnp.int32, (tm, tm), 0)
    cidx = lax.broadcasted_iota(jnp.int32, (tm, tm), 1)
    tri = (cidx < ridx).astype(BF16)
    info, cnt = _router(x2d, whi, wlo, br, tri)

    counts = cnt[0, ROUTER_OFF:ROUTER_OFF + N_EXPERTS].astype(jnp.int32)
    padded = (counts + tm - 1) // tm * tm
    ends = jnp.cumsum(padded)
    base = ends - padded
    e0 = info[:, 0].astype(jnp.int32)
    e1 = info[:, 1].astype(jnp.int32)
    slot0 = base[e0] + info[:, 2].astype(jnp.int32)
    slot1 = base[e1] + info[:, 3].astype(jnp.int32)

    n_slots = 2 * t + N_EXPERTS * tm
    n_tiles = n_slots // tm
    n_used = ends[-1] // tm
    tile_ids = jnp.minimum(jnp.arange(n_tiles, dtype=jnp.int32), n_used - 1)
    tile_expert = jnp.sum((tile_ids[:, None] * tm >= ends[None, :]).astype(jnp.int32), axis=1)

    xs = _dispatch(x2d, slot0, slot1, n_slots)
    ys = _expert_mlp(xs, tile_expert.astype(jnp.int32), n_used.reshape(1).astype(jnp.int32),
                     w_gate, w_up, w_down, layer)
    return _combine(x2d, info, ys, slot0, slot1, g2, b2, alpha)


def kernel(x, mem, w_mem_kv, conv_w_in, conv_dw_w, conv_dw_b, conv_ln_g, conv_ln_b, moba_w_in, w_o,
           ln1_g, ln1_b, w_rg, b_rg, w_re, b_re, w_gate, w_up, w_down, ln2_g, ln2_b):
    batch, seq, d = x.shape
    depth = w_o.shape[0]
    alpha = (2 * depth) ** 0.25
    t = batch * seq
    row = lambda a: a.reshape(1, -1)

    mem_kv = _memkv(mem.reshape(batch * MEM_LEN, d), w_mem_kv)
    mem_kv = mem_kv.reshape(batch, MEM_LEN, 2, MEM_HEADS, HEAD_DIM)
    eye = jnp.eye(MEM_HEADS, dtype=F32)
    kt = jnp.transpose(mem_kv[:, :, 0], (0, 2, 3, 1))
    kbd = jnp.einsum("bhdm,hg->bhdgm", kt, eye).reshape(batch, MEM_WIDTH, MEM_HEADS * MEM_LEN)
    vt = jnp.transpose(mem_kv[:, :, 1], (0, 2, 1, 3))
    vbd = jnp.einsum("bhmd,hg->bhmgd", vt, eye).reshape(batch, MEM_HEADS * MEM_LEN, MEM_WIDTH)
    kbd, vbd = kbd.astype(BF16), vbd.astype(BF16)

    e_shape = w_gate.shape
    w_gate = w_gate.reshape((-1,) + e_shape[-2:])
    w_up = w_up.reshape((-1,) + e_shape[-2:])
    w_down = w_down.reshape((-1,) + w_down.shape[-2:])

    x2d = x.reshape(t, d)
    for i in range(depth):
        j = i // 2
        wo = w_o[i].astype(BF16)
        if i % 2 == 0:
            dw_w = jnp.zeros((CONV_HALO, conv_dw_w.shape[-1]), F32).at[0:CONV_WIDTH].set(conv_dw_w[j])
            x2d = _conv_mixer(x2d, batch, conv_w_in[j].astype(BF16), dw_w, row(conv_dw_b[j]),
                              row(conv_ln_g[j]), row(conv_ln_b[j]), kbd, vbd, wo,
                              row(ln1_g[i]), row(ln1_b[i]), alpha)
        else:
            x2d = _moba_mixer(x2d, batch, moba_w_in[j].astype(BF16), kbd, vbd, wo,
                              row(ln1_g[i]), row(ln1_b[i]), alpha)
        x2d = _moe(x2d, i, w_rg[i], b_rg[i], w_re[i], b_re[i], w_gate, w_up, w_down,
                   row(ln2_g[i]), row(ln2_b[i]), alpha)
    return x2d.reshape(batch, seq, d)
```

```python
import functools

import jax
import jax.numpy as jnp
from jax import lax
from jax.experimental import pallas as pl
from jax.experimental.pallas import tpu as pltpu

F32 = jnp.float32
BF16 = jnp.bfloat16

HEAD_DIM = 64
MEM_LEN = 256
MEM_HEADS = 4
MEM_WIDTH = MEM_HEADS * HEAD_DIM
CONV_WIDTH = 31
MOBA_BLOCK = 256
MOBA_TOPK = 3
MOBA_LOOKAHEAD = 4
N_GROUPS = 4
EXPERTS_PER_GROUP = 8
N_EXPERTS = N_GROUPS * EXPERTS_PER_GROUP
LN_EPS = 1e-5

LANES = 128
SUBLANES = 8
TOKEN_TILE = 256
CONV_HALO = 32
CONV_CHUNK = 32
ROUTER_LANES = 128
ROUTER_OFF = N_GROUPS
DMA_UNROLL = 8
VMEM_LIMIT = 56 * 1024 * 1024

NEG_INF = float("-inf")
LOG2_E = 1.4426950408889634


def _ln(z, g, b):
    mu = jnp.mean(z, axis=-1, keepdims=True)
    zc = z - mu
    var = jnp.mean(zc * zc, axis=-1, keepdims=True)
    return zc * lax.rsqrt(var + LN_EPS) * g + b


def _dot(a, b):
    return jnp.dot(a, b, preferred_element_type=F32)


def _dot_nt(a, b):
    return lax.dot_general(a, b, (((1,), (1,)), ((), ())), preferred_element_type=F32)


def _split_bf16(x):
    hi = x.astype(BF16)
    lo = (x - hi.astype(F32)).astype(BF16)
    return hi, lo


def _memkv_kernel(mem_ref, w_ref, o_ref):
    o_ref[...] = _dot(mem_ref[...].astype(BF16), w_ref[...].astype(BF16))


def _memkv(mem2d, w):
    return pl.pallas_call(
        _memkv_kernel,
        out_shape=jax.ShapeDtypeStruct((mem2d.shape[0], w.shape[1]), F32),
        name="memkv",
    )(mem2d, w)


def _mem_attention(qm, kbd, vbd):
    s = _dot(qm.astype(BF16), kbd) * (HEAD_DIM ** -0.5)
    parts = []
    for h in range(MEM_HEADS):
        seg = s[:, h * MEM_LEN:(h + 1) * MEM_LEN]
        m = jnp.max(seg, axis=-1, keepdims=True)
        e = jnp.exp(seg - m)
        parts.append(e / jnp.sum(e, axis=-1, keepdims=True))
    p = jnp.concatenate(parts, axis=-1)
    return _dot(p.astype(BF16), vbd)


def _out_proj_ln(x, y_mix, y_mem, wo_ref, g_ref, b_ref, alpha, mix_w):
    y = _dot(y_mix.astype(BF16), wo_ref[0:mix_w, :]) + _dot(y_mem.astype(BF16), wo_ref[mix_w:, :])
    return _ln(alpha * x + y, g_ref[...], b_ref[...])


def _conv_mixer_kernel(x_ref, win_ref, dww_ref, dwb_ref, cg_ref, cb_ref, kbd_ref, vbd_ref,
                       wo_ref, g1_ref, b1_ref, o_ref, hbuf, zbuf, cbuf, *, alpha, mix_w):
    tm = x_ref.shape[0]
    j = pl.program_id(1)
    x = x_ref[...]
    u = _dot(x.astype(BF16), win_ref[...])
    a = u[:, 0:mix_w]
    gate = u[:, mix_w:2 * mix_w]
    qm = u[:, 2 * mix_w:]
    h = a * jax.nn.sigmoid(gate)

    @pl.when(j == 0)
    def _():
        hbuf[0:CONV_HALO, :] = jnp.zeros((CONV_HALO, mix_w), F32)

    hbuf[CONV_HALO:CONV_HALO + tm, :] = h

    shifted_rows = CONV_HALO + tm - SUBLANES
    for a in range(1, SUBLANES):
        zbuf[a - 1, 0:shifted_rows, :] = hbuf[a:a + shifted_rows, :]

    first = CONV_HALO - (CONV_WIDTH - 1)
    for c in range(tm // CONV_CHUNK):
        r0 = c * CONV_CHUNK
        acc = jnp.broadcast_to(dwb_ref[...], (CONV_CHUNK, mix_w))
        for k in range(CONV_WIDTH):
            a = (first + k) % SUBLANES
            r = r0 + (first + k) - a
            rows = hbuf[r:r + CONV_CHUNK, :] if a == 0 else zbuf[a - 1, r:r + CONV_CHUNK, :]
            acc = acc + dww_ref[k:k + 1, :] * rows
        cbuf[r0:r0 + CONV_CHUNK, :] = acc

    hbuf[0:CONV_HALO, :] = hbuf[tm:tm + CONV_HALO, :]

    cn = _ln(cbuf[...], cg_ref[...], cb_ref[...])
    y_mix = cn * jax.nn.sigmoid(cn)
    y_mem = _mem_attention(qm, kbd_ref[...], vbd_ref[...])
    o_ref[...] = _out_proj_ln(x, y_mix, y_mem, wo_ref, g1_ref, b1_ref, alpha, mix_w)


def _conv_mixer(x2d, batch, w_in, dw_w, dw_b, cg, cb, kbd, vbd, wo, g1, b1, alpha):
    t, d = x2d.shape
    tm = TOKEN_TILE
    nj = t // batch // tm
    mix_w = d - MEM_WIDTH
    full = lambda shape: pl.BlockSpec(shape, lambda b, j: (0,) * len(shape))
    return pl.pallas_call(
        functools.partial(_conv_mixer_kernel, alpha=alpha, mix_w=mix_w),
        out_shape=jax.ShapeDtypeStruct((t, d), F32),
        grid=(batch, nj),
        in_specs=[
            pl.BlockSpec((tm, d), lambda b, j: (b * nj + j, 0)),
            full(w_in.shape), full(dw_w.shape), full(dw_b.shape), full(cg.shape), full(cb.shape),
            pl.BlockSpec((None,) + kbd.shape[1:], lambda b, j: (b, 0, 0)),
            pl.BlockSpec((None,) + vbd.shape[1:], lambda b, j: (b, 0, 0)),
            full(wo.shape), full(g1.shape), full(b1.shape),
        ],
        out_specs=pl.BlockSpec((tm, d), lambda b, j: (b * nj + j, 0)),
        scratch_shapes=[pltpu.VMEM((CONV_HALO + tm, mix_w), F32),
                        pltpu.VMEM((SUBLANES - 1, CONV_HALO + tm, mix_w), F32),
                        pltpu.VMEM((tm, mix_w), F32)],
        compiler_params=pltpu.CompilerParams(
            dimension_semantics=("arbitrary", "arbitrary"), vmem_limit_bytes=VMEM_LIMIT),
        name="conv_mixer",
    )(x2d, w_in, dw_w, dw_b, cg, cb, kbd, vbd, wo, g1, b1)


def _moba_select_bias(gate, i):
    nb = gate.shape[1]
    blk = lax.broadcasted_iota(jnp.int32, gate.shape, 1)
    past = blk < i
    gm = jnp.where(past, gate, NEG_INF)
    cnt = jnp.zeros(gate.shape, F32)
    for n in range(nb):
        row = gm[:, n:n + 1, :]
        ge = jnp.where(row >= gm, 1.0, 0.0)
        gt = jnp.where(row > gm, 1.0, 0.0)
        cnt = cnt + jnp.where(blk > n, ge, gt)
    return jnp.where(past & (cnt < MOBA_TOPK), 0.0, NEG_INF)


def _moba_mixer_kernel(x_ref, win_ref, kbd_ref, vbd_ref, wo_ref, g1_ref, b1_ref, o_ref,
                       k_sc, vt_sc, kmt_sc, bias_sc, qh_sc, m_sc, l_sc, acc_sc, *, alpha, mix_w, nb):
    tm = x_ref.shape[0]
    heads = mix_w // HEAD_DIM
    i = pl.program_id(1)
    x = x_ref[...]
    u = _dot(x.astype(BF16), win_ref[...])
    q = u[:, 0:mix_w] * (HEAD_DIM ** -0.5 * LOG2_E)
    k = u[:, mix_w:2 * mix_w]
    v = u[:, 2 * mix_w:3 * mix_w]
    qm = u[:, 3 * mix_w:]

    @pl.when(i == 0)
    def _():
        kmt_sc[...] = jnp.zeros(kmt_sc.shape, F32)

    k_sc[i] = k.astype(BF16)
    vt_sc[i] = v.T.astype(BF16)
    kmean = jnp.mean(k, axis=0, keepdims=True)
    lane = lax.broadcasted_iota(jnp.int32, (1, mix_w), 1)
    for h in range(heads):
        in_head = (lane >= h * HEAD_DIM) & (lane < (h + 1) * HEAD_DIM)
        kmt_sc[pl.ds(h * nb + i, 1), :] = jnp.where(in_head, kmean, 0.0)

    q_hi, q_lo = _split_bf16(q)
    km_hi, km_lo = _split_bf16(kmt_sc[...])
    gate_t = _dot_nt(km_hi, q_hi) + _dot_nt(km_hi, q_lo) + _dot_nt(km_lo, q_hi)
    bias_sc[...] = _moba_select_bias(gate_t.reshape(heads, nb, tm), i)

    lane_p = lax.broadcasted_iota(jnp.int32, (tm, 2 * HEAD_DIM), 1)
    for h in range(heads):
        qp = q_hi[:, (h // 2) * 2 * HEAD_DIM:(h // 2 + 1) * 2 * HEAD_DIM]
        keep = (lane_p < HEAD_DIM) if h % 2 == 0 else (lane_p >= HEAD_DIM)
        qh_sc[h] = jnp.where(keep, qp, jnp.zeros_like(qp))

    def scores(j, h):
        cols = slice((h // 2) * 2 * HEAD_DIM, (h // 2 + 1) * 2 * HEAD_DIM)
        return _dot_nt(k_sc[j, :, cols], qh_sc[h])

    rows = lambda h: slice(h * HEAD_DIM, (h + 1) * HEAD_DIM)

    kidx = lax.broadcasted_iota(jnp.int32, (tm, tm), 0)
    qidx = lax.broadcasted_iota(jnp.int32, (tm, tm), 1)
    causal = kidx <= qidx
    def heads_pipelined(j):
        pending = [scores(j, h) for h in range(MOBA_LOOKAHEAD)]
        for h in range(heads):
            if h + MOBA_LOOKAHEAD < heads:
                pending.append(scores(j, h + MOBA_LOOKAHEAD))
            yield h, pending.pop(0)

    def own_block(j, c):
        for h, s in heads_pipelined(j):
            s = jnp.where(causal, s, NEG_INF)
            m = jnp.max(s, axis=0, keepdims=True)
            e = jnp.exp2(s - m)
            m_sc[h] = m
            l_sc[h] = jnp.sum(e, axis=0, keepdims=True)
            acc_sc[rows(h), :] = _dot(vt_sc[j, rows(h), :], e.astype(BF16))
        return c

    lax.fori_loop(i, i + 1, own_block, 0)

    def body(j, c):
        for h, s in heads_pipelined(j):
            b = bias_sc[h, pl.ds(j, 1), :]
            m_old = m_sc[h]
            m_new = jnp.maximum(m_old, jnp.max(s, axis=0, keepdims=True) + b)
            e = jnp.exp2(s - (m_new - b))
            corr = jnp.exp2(m_old - m_new)
            m_sc[h] = m_new
            l_sc[h] = corr * l_sc[h] + jnp.sum(e, axis=0, keepdims=True)
            acc_sc[rows(h), :] = corr * acc_sc[rows(h), :] + _dot(vt_sc[j, rows(h), :], e.astype(BF16))
        return c

    lax.fori_loop(0, i, body, 0)
    for h in range(heads):
        acc_sc[rows(h), :] = acc_sc[rows(h), :] / l_sc[h]

    y_mix = acc_sc[...].T
    y_mem = _mem_attention(qm, kbd_ref[...], vbd_ref[...])
    o_ref[...] = _out_proj_ln(x, y_mix, y_mem, wo_ref, g1_ref, b1_ref, alpha, mix_w)


def _moba_mixer(x2d, batch, w_in, kbd, vbd, wo, g1, b1, alpha):
    t, d = x2d.shape
    tm = MOBA_BLOCK
    nb = t // batch // tm
    mix_w = d - MEM_WIDTH
    heads = mix_w // HEAD_DIM
    full = lambda shape: pl.BlockSpec(shape, lambda b, j: (0,) * len(shape))
    return pl.pallas_call(
        functools.partial(_moba_mixer_kernel, alpha=alpha, mix_w=mix_w, nb=nb),
        out_shape=jax.ShapeDtypeStruct((t, d), F32),
        grid=(batch, nb),
        in_specs=[
            pl.BlockSpec((tm, d), lambda b, j: (b * nb + j, 0)),
            full(w_in.shape),
            pl.BlockSpec((None,) + kbd.shape[1:], lambda b, j: (b, 0, 0)),
            pl.BlockSpec((None,) + vbd.shape[1:], lambda b, j: (b, 0, 0)),
            full(wo.shape), full(g1.shape), full(b1.shape),
        ],
        out_specs=pl.BlockSpec((tm, d), lambda b, j: (b * nb + j, 0)),
        scratch_shapes=[
            pltpu.VMEM((nb, tm, mix_w), BF16),
            pltpu.VMEM((nb, mix_w, tm), BF16),
            pltpu.VMEM((heads * nb, mix_w), F32),
            pltpu.VMEM((heads, nb, tm), F32),
            pltpu.VMEM((heads, tm, 2 * HEAD_DIM), BF16),
            pltpu.VMEM((heads, 1, tm), F32),
            pltpu.VMEM((heads, 1, tm), F32),
            pltpu.VMEM((mix_w, tm), F32),
        ],
        compiler_params=pltpu.CompilerParams(
            dimension_semantics=("arbitrary", "arbitrary"), vmem_limit_bytes=VMEM_LIMIT),
        name="moba_mixer",
    )(x2d, w_in, kbd, vbd, wo, g1, b1)


def _router_kernel(x_ref, whi_ref, wlo_ref, br_ref, tri_ref, info_ref, cnt_ref, run_sc):
    step = pl.program_id(0)

    @pl.when(step == 0)
    def _():
        run_sc[...] = jnp.zeros(run_sc.shape, F32)

    x_hi, x_lo = _split_bf16(x_ref[...])
    logits = _dot(x_hi, whi_ref[...]) + _dot(x_hi, wlo_ref[...]) + _dot(x_lo, whi_ref[...]) + br_ref[...]
    lane = lax.broadcasted_iota(jnp.int32, logits.shape, 1).astype(F32)
    rmax = lambda a: jnp.max(a, axis=-1, keepdims=True)
    rmin = lambda a: jnp.min(a, axis=-1, keepdims=True)
    rsum = lambda a: jnp.sum(a, axis=-1, keepdims=True)
    big = float(2 * ROUTER_LANES)

    is_g = lane < N_GROUPS
    gl = jnp.where(is_g, logits, NEG_INF)
    gmax = rmax(gl)
    gidx = rmin(jnp.where(gl == gmax, lane, big))
    g_w = 1.0 / rsum(jnp.where(is_g, jnp.exp(gl - gmax), 0.0))

    lo = ROUTER_OFF + EXPERTS_PER_GROUP * gidx
    el = jnp.where((lane >= lo) & (lane < lo + EXPERTS_PER_GROUP), logits, NEG_INF)
    v0 = rmax(el)
    i0 = rmin(jnp.where(el == v0, lane, big))
    el1 = jnp.where(lane == i0, NEG_INF, el)
    v1 = rmax(el1)
    i1 = rmin(jnp.where(el1 == v1, lane, big))
    t = jnp.exp(v1 - v0)
    w0 = g_w / (1.0 + t)
    w1 = g_w * t / (1.0 + t)

    pick0 = lane == i0
    pick1 = lane == i1
    onehot = jnp.where(pick0 | pick1, 1.0, 0.0)
    before = _dot(tri_ref[...], onehot.astype(BF16)) + run_sc[0:1, :]
    pos0 = rsum(jnp.where(pick0, before, 0.0))
    pos1 = rsum(jnp.where(pick1, before, 0.0))
    run_sc[...] = run_sc[...] + jnp.sum(onehot, axis=0, keepdims=True)

    vals = (i0 - ROUTER_OFF, i1 - ROUTER_OFF, pos0, pos1, w0, w1)
    info = jnp.zeros(logits.shape, F32)
    for c, val in enumerate(vals):
        info = jnp.where(lane == c, val, info)
    info_ref[...] = info
    cnt_ref[...] = run_sc[...]


def _router(x2d, whi, wlo, br, tri):
    t, d = x2d.shape
    tm = TOKEN_TILE
    full = lambda shape: pl.BlockSpec(shape, lambda s: (0,) * len(shape))
    return pl.pallas_call(
        _router_kernel,
        out_shape=(jax.ShapeDtypeStruct((t, ROUTER_LANES), F32),
                   jax.ShapeDtypeStruct((SUBLANES, ROUTER_LANES), F32)),
        grid=(t // tm,),
        in_specs=[pl.BlockSpec((tm, d), lambda s: (s, 0)),
                  full(whi.shape), full(wlo.shape), full(br.shape), full(tri.shape)],
        out_specs=(pl.BlockSpec((tm, ROUTER_LANES), lambda s: (s, 0)),
                   full((SUBLANES, ROUTER_LANES))),
        scratch_shapes=[pltpu.VMEM((SUBLANES, ROUTER_LANES), F32)],
        compiler_params=pltpu.CompilerParams(dimension_semantics=("arbitrary",)),
        name="router",
    )(x2d, whi, wlo, br, tri)


def _row_copy(src, src_row, dst, dst_row, sem):
    return pltpu.make_async_copy(src.at[pl.ds(src_row, 1), :], dst.at[pl.ds(dst_row, 1), :], sem)


def _slots_kernel(info_ref, base_ref, o_ref):
    info = info_ref[...]
    base = base_ref[...]
    lane = lax.broadcasted_iota(jnp.int32, info.shape, 1).astype(F32)
    rsum = lambda a: jnp.sum(a, axis=-1, keepdims=True)
    s0 = rsum(jnp.where(lane == info[:, 0:1] + ROUTER_OFF, base, 0.0)) + info[:, 2:3]
    s1 = rsum(jnp.where(lane == info[:, 1:2] + ROUTER_OFF, base, 0.0)) + info[:, 3:4]
    both = jnp.where(lane == 0, s0, jnp.where(lane == 1, s1, 0.0))
    o_ref[...] = both.T[0:SUBLANES, :].astype(jnp.int32)


def _slots(info, base_row):
    t = info.shape[0]
    tm = TOKEN_TILE
    return pl.pallas_call(
        _slots_kernel,
        out_shape=jax.ShapeDtypeStruct((t // tm, SUBLANES, tm), jnp.int32),
        grid=(t // tm,),
        in_specs=[pl.BlockSpec((tm, ROUTER_LANES), lambda s: (s, 0)),
                  pl.BlockSpec((1, ROUTER_LANES), lambda s: (0, 0))],
        out_specs=pl.BlockSpec((None, SUBLANES, tm), lambda s: (s, 0, 0)),
        compiler_params=pltpu.CompilerParams(dimension_semantics=("arbitrary",)),
        name="slots",
    )(info, base_row)


def _dispatch_kernel(s0_ref, s1_ref, x_ref, xs_in_ref, xs_ref, sem):
    del xs_in_ref
    tm = x_ref.shape[0]

    def start(r, c):
        _row_copy(x_ref, r, xs_ref, s0_ref[0, r], sem).start(priority=0)
        _row_copy(x_ref, r, xs_ref, s1_ref[0, r], sem).start(priority=1)
        return c

    def wait(r, c):
        _row_copy(x_ref, r, xs_ref, s0_ref[0, r], sem).wait()
        _row_copy(x_ref, r, xs_ref, s1_ref[0, r], sem).wait()
        return c

    lax.fori_loop(0, tm, start, 0, unroll=DMA_UNROLL)
    lax.fori_loop(0, tm, wait, 0, unroll=DMA_UNROLL)


def _dispatch(x2d, slot0, slot1, n_slots):
    t, d = x2d.shape
    tm = TOKEN_TILE
    smem_rows = pl.BlockSpec((None, 1, tm), lambda s: (s, 0, 0), memory_space=pltpu.SMEM)
    return pl.pallas_call(
        _dispatch_kernel,
        out_shape=jax.ShapeDtypeStruct((n_slots, d), F32),
        grid=(t // tm,),
        in_specs=[smem_rows, smem_rows, pl.BlockSpec((tm, d), lambda s: (s, 0)),
                  pl.BlockSpec(memory_space=pl.ANY)],
        out_specs=pl.BlockSpec(memory_space=pl.ANY),
        scratch_shapes=[pltpu.SemaphoreType.DMA],
        input_output_aliases={3: 0},
        compiler_params=pltpu.CompilerParams(dimension_semantics=("arbitrary",)),
        name="dispatch",
    )(slot0, slot1, x2d, jnp.zeros((n_slots, d), F32))


def _expert_kernel(te_ref, nv_ref, xs_ref, wg_ref, wu_ref, wd_ref, ys_ref, wg_sc, wu_sc, wd_sc):
    t = pl.program_id(0)
    n_valid = nv_ref[t]
    used = n_valid > 0
    prev = te_ref[jnp.maximum(t - 1, 0)]

    @pl.when(used & ((t == 0) | (te_ref[t] != prev)))
    def _():
        wg_sc[...] = wg_ref[...].astype(BF16)
        wu_sc[...] = wu_ref[...].astype(BF16)
        wd_sc[...] = wd_ref[...].astype(BF16)

    @pl.when(used)
    def _():
        xb = xs_ref[...].astype(BF16)
        hg = _dot(xb, wg_sc[...])
        hu = _dot(xb, wu_sc[...])
        h = hg * jax.nn.sigmoid(hg) * hu
        ys_ref[...] = _dot(h.astype(BF16), wd_sc[...])

    @pl.when(jnp.logical_not(used))
    def _():
        ys_ref[...] = jnp.zeros(ys_ref.shape, F32)


def _expert_mlp(xs, tile_expert, tile_valid, w_gate, w_up, w_down, layer):
    ns, d = xs.shape
    f = w_gate.shape[-1]
    tm = TOKEN_TILE
    base = layer * N_EXPERTS
    return pl.pallas_call(
        _expert_kernel,
        out_shape=jax.ShapeDtypeStruct((ns, d), F32),
        grid_spec=pltpu.PrefetchScalarGridSpec(
            num_scalar_prefetch=2,
            grid=(ns // tm,),
            in_specs=[
                pl.BlockSpec((tm, d), lambda t, te, nu: (t, 0)),
                pl.BlockSpec((None, d, f), lambda t, te, nu: (base + te[t], 0, 0)),
                pl.BlockSpec((None, d, f), lambda t, te, nu: (base + te[t], 0, 0)),
                pl.BlockSpec((None, f, d), lambda t, te, nu: (base + te[t], 0, 0)),
            ],
            out_specs=pl.BlockSpec((tm, d), lambda t, te, nu: (t, 0)),
            scratch_shapes=[pltpu.VMEM((d, f), BF16), pltpu.VMEM((d, f), BF16), pltpu.VMEM((f, d), BF16)],
        ),
        compiler_params=pltpu.CompilerParams(
            dimension_semantics=("arbitrary",), vmem_limit_bytes=VMEM_LIMIT),
        name="expert_mlp",
    )(tile_expert, tile_valid, xs, w_gate, w_up, w_down)


def _combine_kernel(s0_ref, s1_ref, info_ref, x_ref, ys_ref, g_ref, b_ref, o_ref, y0_sc, y1_sc, sem,
                    *, alpha):
    tm = x_ref.shape[0]

    def start(r, c):
        _row_copy(ys_ref, s0_ref[0, r], y0_sc, r, sem).start(priority=0)
        _row_copy(ys_ref, s1_ref[0, r], y1_sc, r, sem).start(priority=1)
        return c

    def wait(r, c):
        _row_copy(ys_ref, s0_ref[0, r], y0_sc, r, sem).wait()
        _row_copy(ys_ref, s1_ref[0, r], y1_sc, r, sem).wait()
        return c

    lax.fori_loop(0, tm, start, 0, unroll=DMA_UNROLL)
    lax.fori_loop(0, tm, wait, 0, unroll=DMA_UNROLL)
    info = info_ref[...]
    f = info[:, 4:5] * y0_sc[...] + info[:, 5:6] * y1_sc[...]
    o_ref[...] = _ln(alpha * x_ref[...] + f, g_ref[...], b_ref[...])


def _combine(x2d, info, ys, slot0, slot1, g2, b2, alpha):
    t, d = x2d.shape
    tm = TOKEN_TILE
    smem_rows = pl.BlockSpec((None, 1, tm), lambda s: (s, 0, 0), memory_space=pltpu.SMEM)
    full = lambda shape: pl.BlockSpec(shape, lambda s: (0,) * len(shape))
    return pl.pallas_call(
        functools.partial(_combine_kernel, alpha=alpha),
        out_shape=jax.ShapeDtypeStruct((t, d), F32),
        grid=(t // tm,),
        in_specs=[smem_rows, smem_rows,
                  pl.BlockSpec((tm, ROUTER_LANES), lambda s: (s, 0)),
                  pl.BlockSpec((tm, d), lambda s: (s, 0)),
                  pl.BlockSpec(memory_space=pl.ANY),
                  full(g2.shape), full(b2.shape)],
        out_specs=pl.BlockSpec((tm, d), lambda s: (s, 0)),
        scratch_shapes=[pltpu.VMEM((tm, d), F32), pltpu.VMEM((tm, d), F32), pltpu.SemaphoreType.DMA],
        compiler_params=pltpu.CompilerParams(dimension_semantics=("arbitrary",)),
        name="combine",
    )(slot0, slot1, info, x2d, ys, g2, b2)


def _moe(x2d, layer, w_rg, b_rg, w_re, b_re, w_gate, w_up, w_down, g2, b2, alpha):
    t, d = x2d.shape
    tm = TOKEN_TILE
    wr = jnp.zeros((d, ROUTER_LANES), F32).at[:, 0:N_GROUPS].set(w_rg)
    wr = wr.at[:, ROUTER_OFF:ROUTER_OFF + N_EXPERTS].set(w_re)
    br = jnp.zeros((1, ROUTER_LANES), F32).at[0, 0:N_GROUPS].set(b_rg)
    br = br.at[0, ROUTER_OFF:ROUTER_OFF + N_EXPERTS].set(b_re)
    whi, wlo = _split_bf16(wr)
    ridx = lax.broadcasted_iota(jnp.int32, (tm, tm), 0)
    cidx = lax.broadcasted_iota(jnp.int32, (tm, tm), 1)
    tri = (cidx < ridx).astype(BF16)
    info, cnt = _router(x2d, whi, wlo, br, tri)

    counts = cnt[0, ROUTER_OFF:ROUTER_OFF + N_EXPERTS].astype(jnp.int32)
    padded = (counts + tm - 1) // tm * tm
    ends = jnp.cumsum(padded)
    base = ends - padded
    base_row = jnp.zeros((1, ROUTER_LANES), F32).at[0, ROUTER_OFF:ROUTER_OFF + N_EXPERTS].set(base.astype(F32))
    slots = _slots(info, base_row)
    slot0, slot1 = slots[:, 0:1, :], slots[:, 1:2, :]

    n_slots = 2 * t + N_EXPERTS * tm
    n_tiles = n_slots // tm
    n_used = ends[-1] // tm
    tile_ids = jnp.arange(n_tiles, dtype=jnp.int32)
    tile_expert = jnp.sum((jnp.minimum(tile_ids, n_used - 1)[:, None] * tm >= ends[None, :]).astype(jnp.int32),
                          axis=1)
    tile_valid = jnp.clip(counts[tile_expert] - (tile_ids * tm - base[tile_expert]), 0, tm)
    tile_valid = jnp.where(tile_ids < n_used, tile_valid, 0).astype(jnp.int32)

    xs = _dispatch(x2d, slot0, slot1, n_slots)
    ys = _expert_mlp(xs, tile_expert, tile_valid, w_gate, w_up, w_down, layer)
    return _combine(x2d, info, ys, slot0, slot1, g2, b2, alpha)


def kernel(x, mem, w_mem_kv, conv_w_in, conv_dw_w, conv_dw_b, conv_ln_g, conv_ln_b, moba_w_in, w_o,
           ln1_g, ln1_b, w_rg, b_rg, w_re, b_re, w_gate, w_up, w_down, ln2_g, ln2_b):
    batch, seq, d = x.shape
    depth = w_o.shape[0]
    alpha = (2 * depth) ** 0.25
    t = batch * seq
    row = lambda a: a.reshape(1, -1)

    mem_kv = _memkv(mem.reshape(batch * MEM_LEN, d), w_mem_kv)
    mem_kv = mem_kv.reshape(batch, MEM_LEN, 2, MEM_HEADS, HEAD_DIM)
    eye = jnp.eye(MEM_HEADS, dtype=F32)
    kt = jnp.transpose(mem_kv[:, :, 0], (0, 2, 3, 1))
    kbd = jnp.einsum("bhdm,hg->bhdgm", kt, eye).reshape(batch, MEM_WIDTH, MEM_HEADS * MEM_LEN)
    vt = jnp.transpose(mem_kv[:, :, 1], (0, 2, 1, 3))
    vbd = jnp.einsum("bhmd,hg->bhmgd", vt, eye).reshape(batch, MEM_HEADS * MEM_LEN, MEM_WIDTH)
    kbd, vbd = kbd.astype(BF16), vbd.astype(BF16)

    e_shape = w_gate.shape
    w_gate = w_gate.reshape((-1,) + e_shape[-2:])
    w_up = w_up.reshape((-1,) + e_shape[-2:])
    w_down = w_down.reshape((-1,) + w_down.shape[-2:])

    x2d = x.reshape(t, d)
    for i in range(depth):
        j = i // 2
        wo = w_o[i].astype(BF16)
        if i % 2 == 0:
            dw_w = jnp.zeros((CONV_HALO, conv_dw_w.shape[-1]), F32).at[0:CONV_WIDTH].set(conv_dw_w[j])
            x2d = _conv_mixer(x2d, batch, conv_w_in[j].astype(BF16), dw_w, row(conv_dw_b[j]),
                              row(conv_ln_g[j]), row(conv_ln_b[j]), kbd, vbd, wo,
                              row(ln1_g[i]), row(ln1_b[i]), alpha)
        else:
            x2d = _moba_mixer(x2d, batch, moba_w_in[j].astype(BF16), kbd, vbd, wo,
                              row(ln1_g[i]), row(ln1_b[i]), alpha)
        x2d = _moe(x2d, i, w_rg[i], b_rg[i], w_re[i], b_re[i], w_gate, w_up, w_down,
                   row(ln2_g[i]), row(ln2_b[i]), alpha)
    return x2d.reshape(batch, seq, d)
```

```python
import functools

import jax
import jax.numpy as jnp
from jax import lax
from jax.experimental import pallas as pl
from jax.experimental.pallas import tpu as pltpu

F32 = jnp.float32
BF16 = jnp.bfloat16

HEAD_DIM = 64
MEM_LEN = 256
MEM_HEADS = 4
MEM_WIDTH = MEM_HEADS * HEAD_DIM
CONV_WIDTH = 31
MOBA_BLOCK = 256
MOBA_TOPK = 3
MOBA_LOOKAHEAD = 6
N_GROUPS = 4
EXPERTS_PER_GROUP = 8
N_EXPERTS = N_GROUPS * EXPERTS_PER_GROUP
LN_EPS = 1e-5

LANES = 128
SUBLANES = 8
TOKEN_TILE = 256
CONV_HALO = 32
CONV_CHUNK = 32
ROUTER_LANES = 128
ROUTER_ROWS = 40
ROUTER_OFF = N_GROUPS
DMA_UNROLL = 8
VMEM_LIMIT = 56 * 1024 * 1024

NEG_INF = float("-inf")
LOG2_E = 1.4426950408889634


def _ln(z, g, b):
    mu = jnp.mean(z, axis=-1, keepdims=True)
    zc = z - mu
    var = jnp.mean(zc * zc, axis=-1, keepdims=True)
    return zc * lax.rsqrt(var + LN_EPS) * g + b


def _dot(a, b):
    return jnp.dot(a, b, preferred_element_type=F32)


def _dot_nt(a, b):
    return lax.dot_general(a, b, (((1,), (1,)), ((), ())), preferred_element_type=F32)


def _split_bf16(x):
    hi = x.astype(BF16)
    lo = (x - hi.astype(F32)).astype(BF16)
    return hi, lo


def _memkv_kernel(mem_ref, w_ref, kbd_ref, vbd_ref):
    kv = _dot(mem_ref[...].astype(BF16), w_ref[...].astype(BF16))
    k_t = kv[:, 0:MEM_WIDTH].T
    v = kv[:, MEM_WIDTH:]
    r = lax.broadcasted_iota(jnp.int32, kbd_ref.shape, 0)
    c = lax.broadcasted_iota(jnp.int32, kbd_ref.shape, 1)
    kbd_ref[...] = jnp.where(r // HEAD_DIM == c // MEM_LEN,
                             jnp.concatenate([k_t] * MEM_HEADS, axis=1), 0.0).astype(BF16)
    r = lax.broadcasted_iota(jnp.int32, vbd_ref.shape, 0)
    c = lax.broadcasted_iota(jnp.int32, vbd_ref.shape, 1)
    vbd_ref[...] = jnp.where(r // MEM_LEN == c // HEAD_DIM,
                             jnp.concatenate([v] * MEM_HEADS, axis=0), 0.0).astype(BF16)


def _memkv(mem, w):
    batch, m, d = mem.shape
    return pl.pallas_call(
        _memkv_kernel,
        out_shape=(jax.ShapeDtypeStruct((batch, MEM_WIDTH, MEM_HEADS * m), BF16),
                   jax.ShapeDtypeStruct((batch, MEM_HEADS * m, MEM_WIDTH), BF16)),
        grid=(batch,),
        in_specs=[pl.BlockSpec((None, m, d), lambda b: (b, 0, 0)),
                  pl.BlockSpec(w.shape, lambda b: (0, 0))],
        out_specs=(pl.BlockSpec((None, MEM_WIDTH, MEM_HEADS * m), lambda b: (b, 0, 0)),
                   pl.BlockSpec((None, MEM_HEADS * m, MEM_WIDTH), lambda b: (b, 0, 0))),
        compiler_params=pltpu.CompilerParams(dimension_semantics=("arbitrary",)),
        name="memkv",
    )(mem, w)


def _mem_attention(qm, kbd, vbd):
    s = _dot(qm.astype(BF16), kbd) * (HEAD_DIM ** -0.5)
    parts = []
    for h in range(MEM_HEADS):
        seg = s[:, h * MEM_LEN:(h + 1) * MEM_LEN]
        m = jnp.max(seg, axis=-1, keepdims=True)
        e = jnp.exp(seg - m)
        parts.append(e / jnp.sum(e, axis=-1, keepdims=True))
    p = jnp.concatenate(parts, axis=-1)
    return _dot(p.astype(BF16), vbd)


def _out_proj_ln(x, y_mix, y_mem, wo_ref, g_ref, b_ref, alpha, mix_w):
    y = _dot(y_mix.astype(BF16), wo_ref[0:mix_w, :]) + _dot(y_mem.astype(BF16), wo_ref[mix_w:, :])
    return _ln(alpha * x + y, g_ref[...], b_ref[...])


def _conv_mixer_kernel(x_ref, win_ref, dww_ref, dwb_ref, cg_ref, cb_ref, kbd_ref, vbd_ref,
                       wo_ref, g1_ref, b1_ref, o_ref, hbuf, zbuf, cbuf, *, alpha, mix_w):
    tm = x_ref.shape[0]
    j = pl.program_id(1)
    x = x_ref[...]
    u = _dot(x.astype(BF16), win_ref[...])
    a = u[:, 0:mix_w]
    gate = u[:, mix_w:2 * mix_w]
    qm = u[:, 2 * mix_w:]
    h = a * jax.nn.sigmoid(gate)

    @pl.when(j == 0)
    def _():
        hbuf[0:CONV_HALO, :] = jnp.zeros((CONV_HALO, mix_w), F32)

    hbuf[CONV_HALO:CONV_HALO + tm, :] = h

    shifted_rows = CONV_HALO + tm - SUBLANES
    for a in range(1, SUBLANES):
        zbuf[a - 1, 0:shifted_rows, :] = hbuf[a:a + shifted_rows, :]

    first = CONV_HALO - (CONV_WIDTH - 1)
    for c in range(tm // CONV_CHUNK):
        r0 = c * CONV_CHUNK
        acc = jnp.broadcast_to(dwb_ref[...], (CONV_CHUNK, mix_w))
        for k in range(CONV_WIDTH):
            a = (first + k) % SUBLANES
            r = r0 + (first + k) - a
            rows = hbuf[r:r + CONV_CHUNK, :] if a == 0 else zbuf[a - 1, r:r + CONV_CHUNK, :]
            acc = acc + dww_ref[k:k + 1, :] * rows
        cbuf[r0:r0 + CONV_CHUNK, :] = acc

    hbuf[0:CONV_HALO, :] = hbuf[tm:tm + CONV_HALO, :]

    cn = _ln(cbuf[...], cg_ref[...], cb_ref[...])
    y_mix = cn * jax.nn.sigmoid(cn)
    y_mem = _mem_attention(qm, kbd_ref[...], vbd_ref[...])
    o_ref[...] = _out_proj_ln(x, y_mix, y_mem, wo_ref, g1_ref, b1_ref, alpha, mix_w)


def _conv_mixer(x2d, batch, w_in, dw_w, dw_b, cg, cb, kbd, vbd, wo, g1, b1, alpha):
    t, d = x2d.shape
    tm = TOKEN_TILE
    nj = t // batch // tm
    mix_w = d - MEM_WIDTH
    full = lambda shape: pl.BlockSpec(shape, lambda b, j: (0,) * len(shape))
    return pl.pallas_call(
        functools.partial(_conv_mixer_kernel, alpha=alpha, mix_w=mix_w),
        out_shape=jax.ShapeDtypeStruct((t, d), F32),
        grid=(batch, nj),
        in_specs=[
            pl.BlockSpec((tm, d), lambda b, j: (b * nj + j, 0)),
            full(w_in.shape), full(dw_w.shape), full(dw_b.shape), full(cg.shape), full(cb.shape),
            pl.BlockSpec((None,) + kbd.shape[1:], lambda b, j: (b, 0, 0)),
            pl.BlockSpec((None,) + vbd.shape[1:], lambda b, j: (b, 0, 0)),
            full(wo.shape), full(g1.shape), full(b1.shape),
        ],
        out_specs=pl.BlockSpec((tm, d), lambda b, j: (b * nj + j, 0)),
        scratch_shapes=[pltpu.VMEM((CONV_HALO + tm, mix_w), F32),
                        pltpu.VMEM((SUBLANES - 1, CONV_HALO + tm, mix_w), F32),
                        pltpu.VMEM((tm, mix_w), F32)],
        compiler_params=pltpu.CompilerParams(
            dimension_semantics=("arbitrary", "arbitrary"), vmem_limit_bytes=VMEM_LIMIT),
        name="conv_mixer",
    )(x2d, w_in, dw_w, dw_b, cg, cb, kbd, vbd, wo, g1, b1)


def _moba_select_bias(gate, i):
    nb = gate.shape[1]
    blk = lax.broadcasted_iota(jnp.int32, gate.shape, 1)
    past = blk < i
    gm = jnp.where(past, gate, NEG_INF)
    cnt = jnp.zeros(gate.shape, F32)
    for n in range(nb):
        row = gm[:, n:n + 1, :]
        ge = jnp.where(row >= gm, 1.0, 0.0)
        gt = jnp.where(row > gm, 1.0, 0.0)
        cnt = cnt + jnp.where(blk > n, ge, gt)
    return jnp.where(past & (cnt < MOBA_TOPK), 0.0, NEG_INF)


def _moba_mixer_kernel(x_ref, win_ref, kbd_ref, vbd_ref, wo_ref, g1_ref, b1_ref, o_ref,
                       k_sc, vt_sc, kmt_sc, bias_sc, qh_sc, m_sc, l_sc, acc_sc, *, alpha, mix_w, nb):
    tm = x_ref.shape[0]
    heads = mix_w // HEAD_DIM
    i = pl.program_id(1)
    x = x_ref[...]
    u = _dot(x.astype(BF16), win_ref[...])
    q = u[:, 0:mix_w] * (HEAD_DIM ** -0.5 * LOG2_E)
    k = u[:, mix_w:2 * mix_w]
    v = u[:, 2 * mix_w:3 * mix_w]
    qm = u[:, 3 * mix_w:]

    @pl.when(i == 0)
    def _():
        kmt_sc[...] = jnp.zeros(kmt_sc.shape, F32)

    k_sc[i] = k.astype(BF16)
    vt_sc[i] = v.T.astype(BF16)
    kmean = jnp.mean(k, axis=0, keepdims=True)
    lane = lax.broadcasted_iota(jnp.int32, (1, mix_w), 1)
    for h in range(heads):
        in_head = (lane >= h * HEAD_DIM) & (lane < (h + 1) * HEAD_DIM)
        kmt_sc[pl.ds(h * nb + i, 1), :] = jnp.where(in_head, kmean, 0.0)

    q_hi, q_lo = _split_bf16(q)
    km_hi, km_lo = _split_bf16(kmt_sc[...])
    gate_t = _dot_nt(km_hi, q_hi) + _dot_nt(km_hi, q_lo) + _dot_nt(km_lo, q_hi)
    bias_sc[...] = _moba_select_bias(gate_t.reshape(heads, nb, tm), i)

    lane_p = lax.broadcasted_iota(jnp.int32, (tm, 2 * HEAD_DIM), 1)
    for h in range(heads):
        qp = q_hi[:, (h // 2) * 2 * HEAD_DIM:(h // 2 + 1) * 2 * HEAD_DIM]
        keep = (lane_p < HEAD_DIM) if h % 2 == 0 else (lane_p >= HEAD_DIM)
        qh_sc[h] = jnp.where(keep, qp, jnp.zeros_like(qp))

    def scores(j, h):
        cols = slice((h // 2) * 2 * HEAD_DIM, (h // 2 + 1) * 2 * HEAD_DIM)
        return _dot_nt(k_sc[j, :, cols], qh_sc[h])

    rows = lambda h: slice(h * HEAD_DIM, (h + 1) * HEAD_DIM)

    kidx = lax.broadcasted_iota(jnp.int32, (tm, tm), 0)
    qidx = lax.broadcasted_iota(jnp.int32, (tm, tm), 1)
    causal = kidx <= qidx
    def heads_pipelined(j):
        pending = [scores(j, h) for h in range(MOBA_LOOKAHEAD)]
        for h in range(heads):
            if h + MOBA_LOOKAHEAD < heads:
                pending.append(scores(j, h + MOBA_LOOKAHEAD))
            yield h, pending.pop(0)

    def own_block(j, c):
        for h, s in heads_pipelined(j):
            s = jnp.where(causal, s, NEG_INF)
            m = jnp.max(s, axis=0, keepdims=True)
            e = jnp.exp2(s - m)
            m_sc[h] = m
            l_sc[h] = jnp.sum(e, axis=0, keepdims=True)
            acc_sc[rows(h), :] = _dot(vt_sc[j, rows(h), :], e.astype(BF16))
        return c

    lax.fori_loop(i, i + 1, own_block, 0)

    def body(j, c):
        for h, s in heads_pipelined(j):
            b = bias_sc[h, pl.ds(j, 1), :]
            m_old = m_sc[h]
            m_new = jnp.maximum(m_old, jnp.max(s, axis=0, keepdims=True) + b)
            e = jnp.exp2(s - (m_new - b))
            corr = jnp.exp2(m_old - m_new)
            m_sc[h] = m_new
            l_sc[h] = corr * l_sc[h] + jnp.sum(e, axis=0, keepdims=True)
            acc_sc[rows(h), :] = corr * acc_sc[rows(h), :] + _dot(vt_sc[j, rows(h), :], e.astype(BF16))
        return c

    lax.fori_loop(0, i, body, 0)
    for h in range(heads):
        acc_sc[rows(h), :] = acc_sc[rows(h), :] / l_sc[h]

    y_mix = acc_sc[...].T
    y_mem = _mem_attention(qm, kbd_ref[...], vbd_ref[...])
    o_ref[...] = _out_proj_ln(x, y_mix, y_mem, wo_ref, g1_ref, b1_ref, alpha, mix_w)


def _moba_mixer(x2d, batch, w_in, kbd, vbd, wo, g1, b1, alpha):
    t, d = x2d.shape
    tm = MOBA_BLOCK
    nb = t // batch // tm
    mix_w = d - MEM_WIDTH
    heads = mix_w // HEAD_DIM
    full = lambda shape: pl.BlockSpec(shape, lambda b, j: (0,) * len(shape))
    return pl.pallas_call(
        functools.partial(_moba_mixer_kernel, alpha=alpha, mix_w=mix_w, nb=nb),
        out_shape=jax.ShapeDtypeStruct((t, d), F32),
        grid=(batch, nb),
        in_specs=[
            pl.BlockSpec((tm, d), lambda b, j: (b * nb + j, 0)),
            full(w_in.shape),
            pl.BlockSpec((None,) + kbd.shape[1:], lambda b, j: (b, 0, 0)),
            pl.BlockSpec((None,) + vbd.shape[1:], lambda b, j: (b, 0, 0)),
            full(wo.shape), full(g1.shape), full(b1.shape),
        ],
        out_specs=pl.BlockSpec((tm, d), lambda b, j: (b * nb + j, 0)),
        scratch_shapes=[
            pltpu.VMEM((nb, tm, mix_w), BF16),
            pltpu.VMEM((nb, mix_w, tm), BF16),
            pltpu.VMEM((heads * nb, mix_w), F32),
            pltpu.VMEM((heads, nb, tm), F32),
            pltpu.VMEM((heads, tm, 2 * HEAD_DIM), BF16),
            pltpu.VMEM((heads, 1, tm), F32),
            pltpu.VMEM((heads, 1, tm), F32),
            pltpu.VMEM((mix_w, tm), F32),
        ],
        compiler_params=pltpu.CompilerParams(
            dimension_semantics=("arbitrary", "arbitrary"), vmem_limit_bytes=VMEM_LIMIT),
        name="moba_mixer",
    )(x2d, w_in, kbd, vbd, wo, g1, b1)


def _router_kernel(x_ref, whi_ref, wlo_ref, br_ref, tri_ref, info_ref, sel_ref, cnt_ref, run_sc):
    step = pl.program_id(0)

    @pl.when(step == 0)
    def _():
        run_sc[...] = jnp.zeros(run_sc.shape, F32)

    x_hi, x_lo = _split_bf16(x_ref[...])
    logits = (_dot_nt(whi_ref[...], x_hi) + _dot_nt(wlo_ref[...], x_hi) + _dot_nt(whi_ref[...], x_lo)
              + br_ref[...])[0:ROUTER_ROWS, :]
    row = lax.broadcasted_iota(jnp.int32, logits.shape, 0).astype(F32)
    cmax = lambda a: jnp.max(a, axis=0, keepdims=True)
    cmin = lambda a: jnp.min(a, axis=0, keepdims=True)
    csum = lambda a: jnp.sum(a, axis=0, keepdims=True)
    big = float(2 * ROUTER_ROWS)

    is_g = row < N_GROUPS
    gl = jnp.where(is_g, logits, NEG_INF)
    gmax = cmax(gl)
    gidx = cmin(jnp.where(gl == gmax, row, big))
    g_w = 1.0 / csum(jnp.where(is_g, jnp.exp(gl - gmax), 0.0))

    lo = ROUTER_OFF + EXPERTS_PER_GROUP * gidx
    el = jnp.where((row >= lo) & (row < lo + EXPERTS_PER_GROUP), logits, NEG_INF)
    v0 = cmax(el)
    i0 = cmin(jnp.where(el == v0, row, big))
    el1 = jnp.where(row == i0, NEG_INF, el)
    v1 = cmax(el1)
    i1 = cmin(jnp.where(el1 == v1, row, big))
    t = jnp.exp(v1 - v0)
    w0 = g_w / (1.0 + t)
    w1 = g_w * t / (1.0 + t)

    pick0 = row == i0
    pick1 = row == i1
    onehot = jnp.where(pick0 | pick1, 1.0, 0.0)
    before = _dot(onehot.astype(BF16), tri_ref[...]) + run_sc[...]
    pos0 = csum(jnp.where(pick0, before, 0.0))
    pos1 = csum(jnp.where(pick1, before, 0.0))
    run_sc[...] = run_sc[...] + jnp.sum(onehot, axis=1, keepdims=True)
    cnt_ref[...] = run_sc[...]

    vals = (i0 - ROUTER_OFF, i1 - ROUTER_OFF, pos0, pos1, w0, w1)
    row8 = lax.broadcasted_iota(jnp.int32, sel_ref.shape, 0)
    sel = jnp.zeros(sel_ref.shape, F32)
    for c, val in enumerate(vals):
        sel = jnp.where(row8 == c, val, sel)
    sel_ref[...] = sel
    rowl = lax.broadcasted_iota(jnp.int32, (ROUTER_LANES, x_ref.shape[0]), 0)
    info_ref[...] = jnp.where(rowl == 4, w0, jnp.where(rowl == 5, w1, 0.0)).T


def _router(x2d, whi, wlo, br, tri):
    t, d = x2d.shape
    tm = TOKEN_TILE
    full = lambda shape: pl.BlockSpec(shape, lambda s: (0,) * len(shape))
    return pl.pallas_call(
        _router_kernel,
        out_shape=(jax.ShapeDtypeStruct((t, ROUTER_LANES), F32),
                   jax.ShapeDtypeStruct((t // tm, SUBLANES, tm), F32),
                   jax.ShapeDtypeStruct((ROUTER_ROWS, tm), F32)),
        grid=(t // tm,),
        in_specs=[pl.BlockSpec((tm, d), lambda s: (s, 0)),
                  full(whi.shape), full(wlo.shape), full(br.shape), full(tri.shape)],
        out_specs=(pl.BlockSpec((tm, ROUTER_LANES), lambda s: (s, 0)),
                   pl.BlockSpec((None, SUBLANES, tm), lambda s: (s, 0, 0)),
                   full((ROUTER_ROWS, tm))),
        scratch_shapes=[pltpu.VMEM((ROUTER_ROWS, tm), F32)],
        compiler_params=pltpu.CompilerParams(dimension_semantics=("arbitrary",)),
        name="router",
    )(x2d, whi, wlo, br, tri)


def _row_copy(src, src_row, dst, dst_row, sem):
    return pltpu.make_async_copy(src.at[pl.ds(src_row, 1), :], dst.at[pl.ds(dst_row, 1), :], sem)


def _slots_kernel(sel_ref, base_ref, o_ref):
    sel = sel_ref[...]
    row = lax.broadcasted_iota(jnp.int32, (ROUTER_ROWS, sel.shape[1]), 0).astype(F32)
    csum = lambda a: jnp.sum(a, axis=0, keepdims=True)
    s0 = csum(jnp.where(row == sel[0:1, :] + ROUTER_OFF, base_ref[...], 0.0)) + sel[2:3, :]
    s1 = csum(jnp.where(row == sel[1:2, :] + ROUTER_OFF, base_ref[...], 0.0)) + sel[3:4, :]
    row8 = lax.broadcasted_iota(jnp.int32, o_ref.shape, 0)
    o_ref[...] = jnp.where(row8 == 0, s0, jnp.where(row8 == 1, s1, 0.0)).astype(jnp.int32)


def _slots(sel, base_col):
    nt, _, tm = sel.shape
    return pl.pallas_call(
        _slots_kernel,
        out_shape=jax.ShapeDtypeStruct((nt, SUBLANES, tm), jnp.int32),
        grid=(nt,),
        in_specs=[pl.BlockSpec((None, SUBLANES, tm), lambda s: (s, 0, 0)),
                  pl.BlockSpec(base_col.shape, lambda s: (0, 0))],
        out_specs=pl.BlockSpec((None, SUBLANES, tm), lambda s: (s, 0, 0)),
        compiler_params=pltpu.CompilerParams(dimension_semantics=("arbitrary",)),
        name="slots",
    )(sel, base_col)


def _dispatch_kernel(s0_ref, s1_ref, x_ref, xs_in_ref, xs_ref, sem):
    del xs_in_ref
    tm = x_ref.shape[0]

    def start(r, c):
        _row_copy(x_ref, r, xs_ref, s0_ref[0, r], sem).start(priority=0)
        _row_copy(x_ref, r, xs_ref, s1_ref[0, r], sem).start(priority=1)
        return c

    lax.fori_loop(0, tm, start, 0, unroll=DMA_UNROLL)
    for _ in range(2):
        pltpu.make_async_copy(x_ref, xs_ref.at[pl.ds(0, tm), :], sem).wait()


def _dispatch(x2d, slot0, slot1, n_slots):
    t, d = x2d.shape
    tm = TOKEN_TILE
    smem_rows = pl.BlockSpec((None, 1, tm), lambda s: (s, 0, 0), memory_space=pltpu.SMEM)
    return pl.pallas_call(
        _dispatch_kernel,
        out_shape=jax.ShapeDtypeStruct((n_slots, d), F32),
        grid=(t // tm,),
        in_specs=[smem_rows, smem_rows, pl.BlockSpec((tm, d), lambda s: (s, 0)),
                  pl.BlockSpec(memory_space=pl.ANY)],
        out_specs=pl.BlockSpec(memory_space=pl.ANY),
        scratch_shapes=[pltpu.SemaphoreType.DMA],
        input_output_aliases={3: 0},
        compiler_params=pltpu.CompilerParams(dimension_semantics=("arbitrary",)),
        name="dispatch",
    )(slot0, slot1, x2d, jnp.zeros((n_slots, d), F32))


def _expert_kernel(te_ref, nv_ref, xs_ref, wg_ref, wu_ref, wd_ref, ys_ref, wg_sc, wu_sc, wd_sc):
    t = pl.program_id(0)
    n_valid = nv_ref[t]
    used = n_valid > 0
    prev = te_ref[jnp.maximum(t - 1, 0)]

    @pl.when(used & ((t == 0) | (te_ref[t] != prev)))
    def _():
        wg_sc[...] = wg_ref[...].astype(BF16)
        wu_sc[...] = wu_ref[...].astype(BF16)
        wd_sc[...] = wd_ref[...].astype(BF16)

    @pl.when(used)
    def _():
        xb = xs_ref[...].astype(BF16)
        hg = _dot(xb, wg_sc[...])
        hu = _dot(xb, wu_sc[...])
        h = hg * jax.nn.sigmoid(hg) * hu
        ys_ref[...] = _dot(h.astype(BF16), wd_sc[...])

    @pl.when(jnp.logical_not(used))
    def _():
        ys_ref[...] = jnp.zeros(ys_ref.shape, F32)


def _expert_mlp(xs, tile_expert, tile_valid, w_gate, w_up, w_down, layer):
    ns, d = xs.shape
    f = w_gate.shape[-1]
    tm = TOKEN_TILE
    base = layer * N_EXPERTS
    return pl.pallas_call(
        _expert_kernel,
        out_shape=jax.ShapeDtypeStruct((ns, d), F32),
        grid_spec=pltpu.PrefetchScalarGridSpec(
            num_scalar_prefetch=2,
            grid=(ns // tm,),
            in_specs=[
                pl.BlockSpec((tm, d), lambda t, te, nu: (t, 0)),
                pl.BlockSpec((None, d, f), lambda t, te, nu: (base + te[t], 0, 0)),
                pl.BlockSpec((None, d, f), lambda t, te, nu: (base + te[t], 0, 0)),
                pl.BlockSpec((None, f, d), lambda t, te, nu: (base + te[t], 0, 0)),
            ],
            out_specs=pl.BlockSpec((tm, d), lambda t, te, nu: (t, 0)),
            scratch_shapes=[pltpu.VMEM((d, f), BF16), pltpu.VMEM((d, f), BF16), pltpu.VMEM((f, d), BF16)],
        ),
        compiler_params=pltpu.CompilerParams(
            dimension_semantics=("arbitrary",), vmem_limit_bytes=VMEM_LIMIT),
        name="expert_mlp",
    )(tile_expert, tile_valid, xs, w_gate, w_up, w_down)


def _combine_kernel(s0_ref, s1_ref, info_ref, x_ref, ys_ref, g_ref, b_ref, o_ref, y0_sc, y1_sc, sem,
                    *, alpha):
    tm = x_ref.shape[0]

    def start(r, c):
        _row_copy(ys_ref, s0_ref[0, r], y0_sc, r, sem).start(priority=0)
        _row_copy(ys_ref, s1_ref[0, r], y1_sc, r, sem).start(priority=1)
        return c

    lax.fori_loop(0, tm, start, 0, unroll=DMA_UNROLL)
    for dst in (y0_sc, y1_sc):
        pltpu.make_async_copy(ys_ref.at[pl.ds(0, tm), :], dst, sem).wait()
    info = info_ref[...]
    f = info[:, 4:5] * y0_sc[...] + info[:, 5:6] * y1_sc[...]
    o_ref[...] = _ln(alpha * x_ref[...] + f, g_ref[...], b_ref[...])


def _combine(x2d, info, ys, slot0, slot1, g2, b2, alpha):
    t, d = x2d.shape
    tm = TOKEN_TILE
    smem_rows = pl.BlockSpec((None, 1, tm), lambda s: (s, 0, 0), memory_space=pltpu.SMEM)
    full = lambda shape: pl.BlockSpec(shape, lambda s: (0,) * len(shape))
    return pl.pallas_call(
        functools.partial(_combine_kernel, alpha=alpha),
        out_shape=jax.ShapeDtypeStruct((t, d), F32),
        grid=(t // tm,),
        in_specs=[smem_rows, smem_rows,
                  pl.BlockSpec((tm, ROUTER_LANES), lambda s: (s, 0)),
                  pl.BlockSpec((tm, d), lambda s: (s, 0)),
                  pl.BlockSpec(memory_space=pl.ANY),
                  full(g2.shape), full(b2.shape)],
        out_specs=pl.BlockSpec((tm, d), lambda s: (s, 0)),
        scratch_shapes=[pltpu.VMEM((tm, d), F32), pltpu.VMEM((tm, d), F32), pltpu.SemaphoreType.DMA],
        compiler_params=pltpu.CompilerParams(dimension_semantics=("arbitrary",)),
        name="combine",
    )(slot0, slot1, info, x2d, ys, g2, b2)


def _moe(x2d, layer, w_rg, b_rg, w_re, b_re, w_gate, w_up, w_down, g2, b2, alpha):
    t, d = x2d.shape
    tm = TOKEN_TILE
    wr = jnp.zeros((ROUTER_LANES, d), F32).at[0:N_GROUPS].set(w_rg.T)
    wr = wr.at[ROUTER_OFF:ROUTER_OFF + N_EXPERTS].set(w_re.T)
    br = jnp.zeros((ROUTER_LANES, 1), F32).at[0:N_GROUPS, 0].set(b_rg)
    br = br.at[ROUTER_OFF:ROUTER_OFF + N_EXPERTS, 0].set(b_re)
    whi, wlo = _split_bf16(wr)
    ridx = lax.broadcasted_iota(jnp.int32, (tm, tm), 0)
    cidx = lax.broadcasted_iota(jnp.int32, (tm, tm), 1)
    tri = (ridx < cidx).astype(BF16)
    info, sel, cnt = _router(x2d, whi, wlo, br, tri)

    counts = cnt[ROUTER_OFF:ROUTER_OFF + N_EXPERTS, 0].astype(jnp.int32)
    padded = (counts + tm - 1) // tm * tm
    ends = jnp.cumsum(padded)
    base = ends - padded
    base_col = jnp.zeros((ROUTER_ROWS, 1), F32).at[ROUTER_OFF:ROUTER_OFF + N_EXPERTS, 0].set(base.astype(F32))
    slots = _slots(sel, base_col)
    slot0, slot1 = slots[:, 0:1, :], slots[:, 1:2, :]

    n_slots = 2 * t + N_EXPERTS * tm
    n_tiles = n_slots // tm
    n_used = ends[-1] // tm
    tile_ids = jnp.arange(n_tiles, dtype=jnp.int32)
    tile_expert = jnp.sum((jnp.minimum(tile_ids, n_used - 1)[:, None] * tm >= ends[None, :]).astype(jnp.int32),
                          axis=1)
    tile_valid = jnp.clip(counts[tile_expert] - (tile_ids * tm - base[tile_expert]), 0, tm)
    tile_valid = jnp.where(tile_ids < n_used, tile_valid, 0).astype(jnp.int32)

    xs = _dispatch(x2d, slot0, slot1, n_slots)
    ys = _expert_mlp(xs, tile_expert, tile_valid, w_gate, w_up, w_down, layer)
    return _combine(x2d, info, ys, slot0, slot1, g2, b2, alpha)


def kernel(x, mem, w_mem_kv, conv_w_in, conv_dw_w, conv_dw_b, conv_ln_g, conv_ln_b, moba_w_in, w_o,
           ln1_g, ln1_b, w_rg, b_rg, w_re, b_re, w_gate, w_up, w_down, ln2_g, ln2_b):
    batch, seq, d = x.shape
    depth = w_o.shape[0]
    alpha = (2 * depth) ** 0.25
    t = batch * seq
    row = lambda a: a.reshape(1, -1)

    kbd, vbd = _memkv(mem, w_mem_kv)

    e_shape = w_gate.shape
    w_gate = w_gate.reshape((-1,) + e_shape[-2:])
    w_up = w_up.reshape((-1,) + e_shape[-2:])
    w_down = w_down.reshape((-1,) + w_down.shape[-2:])

    x2d = x.reshape(t, d)
    for i in range(depth):
        j = i // 2
        wo = w_o[i].astype(BF16)
        if i % 2 == 0:
            dw_w = jnp.zeros((CONV_HALO, conv_dw_w.shape[-1]), F32).at[0:CONV_WIDTH].set(conv_dw_w[j])
            x2d = _conv_mixer(x2d, batch, conv_w_in[j].astype(BF16), dw_w, row(conv_dw_b[j]),
                              row(conv_ln_g[j]), row(conv_ln_b[j]), kbd, vbd, wo,
                              row(ln1_g[i]), row(ln1_b[i]), alpha)
        else:
            x2d = _moba_mixer(x2d, batch, moba_w_in[j].astype(BF16), kbd, vbd, wo,
                              row(ln1_g[i]), row(ln1_b[i]), alpha)
        x2d = _moe(x2d, i, w_rg[i], b_rg[i], w_re[i], b_re[i], w_gate, w_up, w_down,
                   row(ln2_g[i]), row(ln2_b[i]), alpha)
    return x2d.reshape(batch, seq, d)
```

```python
import functools

import jax
import jax.numpy as jnp
from jax import lax
from jax.experimental import pallas as pl
from jax.experimental.pallas import tpu as pltpu

F32 = jnp.float32
BF16 = jnp.bfloat16

HEAD_DIM = 64
MEM_LEN = 256
MEM_HEADS = 4
MEM_WIDTH = MEM_HEADS * HEAD_DIM
CONV_WIDTH = 31
MOBA_BLOCK = 256
MOBA_TOPK = 3
MOBA_LOOKAHEAD = 6
N_GROUPS = 4
EXPERTS_PER_GROUP = 8
N_EXPERTS = N_GROUPS * EXPERTS_PER_GROUP
LN_EPS = 1e-5

LANES = 128
SUBLANES = 8
TOKEN_TILE = 256
CONV_HALO = 32
CONV_CHUNK = 32
ROUTER_LANES = 128
ROUTER_ROWS = 40
ROUTER_OFF = N_GROUPS
DMA_UNROLL = 8
SLOT_TILES_PER_STEP = 16
VMEM_LIMIT = 56 * 1024 * 1024

NEG_INF = float("-inf")
LOG2_E = 1.4426950408889634


def _ln(z, g, b):
    mu = jnp.mean(z, axis=-1, keepdims=True)
    zc = z - mu
    var = jnp.mean(zc * zc, axis=-1, keepdims=True)
    return zc * lax.rsqrt(var + LN_EPS) * g + b


def _dot(a, b):
    return jnp.dot(a, b, preferred_element_type=F32)


def _dot_nt(a, b):
    return lax.dot_general(a, b, (((1,), (1,)), ((), ())), preferred_element_type=F32)


def _split_bf16(x):
    hi = x.astype(BF16)
    lo = (x - hi.astype(F32)).astype(BF16)
    return hi, lo


def _memkv_kernel(mem_ref, w_ref, kbd_ref, vbd_ref):
    kv = _dot(mem_ref[...].astype(BF16), w_ref[...].astype(BF16))
    k_t = kv[:, 0:MEM_WIDTH].T
    v = kv[:, MEM_WIDTH:]
    r = lax.broadcasted_iota(jnp.int32, kbd_ref.shape, 0)
    c = lax.broadcasted_iota(jnp.int32, kbd_ref.shape, 1)
    kbd_ref[...] = jnp.where(r // HEAD_DIM == c // MEM_LEN,
                             jnp.concatenate([k_t] * MEM_HEADS, axis=1), 0.0).astype(BF16)
    r = lax.broadcasted_iota(jnp.int32, vbd_ref.shape, 0)
    c = lax.broadcasted_iota(jnp.int32, vbd_ref.shape, 1)
    vbd_ref[...] = jnp.where(r // MEM_LEN == c // HEAD_DIM,
                             jnp.concatenate([v] * MEM_HEADS, axis=0), 0.0).astype(BF16)


def _memkv(mem, w):
    batch, m, d = mem.shape
    return pl.pallas_call(
        _memkv_kernel,
        out_shape=(jax.ShapeDtypeStruct((batch, MEM_WIDTH, MEM_HEADS * m), BF16),
                   jax.ShapeDtypeStruct((batch, MEM_HEADS * m, MEM_WIDTH), BF16)),
        grid=(batch,),
        in_specs=[pl.BlockSpec((None, m, d), lambda b: (b, 0, 0)),
                  pl.BlockSpec(w.shape, lambda b: (0, 0))],
        out_specs=(pl.BlockSpec((None, MEM_WIDTH, MEM_HEADS * m), lambda b: (b, 0, 0)),
                   pl.BlockSpec((None, MEM_HEADS * m, MEM_WIDTH), lambda b: (b, 0, 0))),
        compiler_params=pltpu.CompilerParams(dimension_semantics=("arbitrary",)),
        name="memkv",
    )(mem, w)


def _mem_attention(qm, kbd, vbd):
    s = _dot(qm.astype(BF16), kbd) * (HEAD_DIM ** -0.5)
    parts = []
    for h in range(MEM_HEADS):
        seg = s[:, h * MEM_LEN:(h + 1) * MEM_LEN]
        m = jnp.max(seg, axis=-1, keepdims=True)
        e = jnp.exp(seg - m)
        parts.append(e / jnp.sum(e, axis=-1, keepdims=True))
    p = jnp.concatenate(parts, axis=-1)
    return _dot(p.astype(BF16), vbd)


def _out_proj_ln(x, y_mix, y_mem, wo_ref, g_ref, b_ref, alpha, mix_w):
    y = _dot(y_mix.astype(BF16), wo_ref[0:mix_w, :]) + _dot(y_mem.astype(BF16), wo_ref[mix_w:, :])
    return _ln(alpha * x + y, g_ref[...], b_ref[...])


def _conv_mixer_kernel(x_ref, win_ref, dww_ref, dwb_ref, cg_ref, cb_ref, kbd_ref, vbd_ref,
                       wo_ref, g1_ref, b1_ref, o_ref, hbuf, zbuf, cbuf, *, alpha, mix_w):
    tm = x_ref.shape[0]
    j = pl.program_id(1)
    x = x_ref[...]
    u = _dot(x.astype(BF16), win_ref[...])
    a = u[:, 0:mix_w]
    gate = u[:, mix_w:2 * mix_w]
    qm = u[:, 2 * mix_w:]
    h = a * jax.nn.sigmoid(gate)

    @pl.when(j == 0)
    def _():
        hbuf[0:CONV_HALO, :] = jnp.zeros((CONV_HALO, mix_w), F32)

    hbuf[CONV_HALO:CONV_HALO + tm, :] = h

    shifted_rows = CONV_HALO + tm - SUBLANES
    for a in range(1, SUBLANES):
        zbuf[a - 1, 0:shifted_rows, :] = hbuf[a:a + shifted_rows, :]

    first = CONV_HALO - (CONV_WIDTH - 1)
    for c in range(tm // CONV_CHUNK):
        r0 = c * CONV_CHUNK
        acc = jnp.broadcast_to(dwb_ref[...], (CONV_CHUNK, mix_w))
        for k in range(CONV_WIDTH):
            a = (first + k) % SUBLANES
            r = r0 + (first + k) - a
            rows = hbuf[r:r + CONV_CHUNK, :] if a == 0 else zbuf[a - 1, r:r + CONV_CHUNK, :]
            acc = acc + dww_ref[k:k + 1, :] * rows
        cbuf[r0:r0 + CONV_CHUNK, :] = acc

    hbuf[0:CONV_HALO, :] = hbuf[tm:tm + CONV_HALO, :]

    cn = _ln(cbuf[...], cg_ref[...], cb_ref[...])
    y_mix = cn * jax.nn.sigmoid(cn)
    y_mem = _mem_attention(qm, kbd_ref[...], vbd_ref[...])
    o_ref[...] = _out_proj_ln(x, y_mix, y_mem, wo_ref, g1_ref, b1_ref, alpha, mix_w)


def _conv_mixer(x2d, batch, w_in, dw_w, dw_b, cg, cb, kbd, vbd, wo, g1, b1, alpha):
    t, d = x2d.shape
    tm = TOKEN_TILE
    nj = t // batch // tm
    mix_w = d - MEM_WIDTH
    full = lambda shape: pl.BlockSpec(shape, lambda b, j: (0,) * len(shape))
    return pl.pallas_call(
        functools.partial(_conv_mixer_kernel, alpha=alpha, mix_w=mix_w),
        out_shape=jax.ShapeDtypeStruct((t, d), F32),
        grid=(batch, nj),
        in_specs=[
            pl.BlockSpec((tm, d), lambda b, j: (b * nj + j, 0)),
            full(w_in.shape), full(dw_w.shape), full(dw_b.shape), full(cg.shape), full(cb.shape),
            pl.BlockSpec((None,) + kbd.shape[1:], lambda b, j: (b, 0, 0)),
            pl.BlockSpec((None,) + vbd.shape[1:], lambda b, j: (b, 0, 0)),
            full(wo.shape), full(g1.shape), full(b1.shape),
        ],
        out_specs=pl.BlockSpec((tm, d), lambda b, j: (b * nj + j, 0)),
        scratch_shapes=[pltpu.VMEM((CONV_HALO + tm, mix_w), F32),
                        pltpu.VMEM((SUBLANES - 1, CONV_HALO + tm, mix_w), F32),
                        pltpu.VMEM((tm, mix_w), F32)],
        compiler_params=pltpu.CompilerParams(
            dimension_semantics=("arbitrary", "arbitrary"), vmem_limit_bytes=VMEM_LIMIT),
        name="conv_mixer",
    )(x2d, w_in, dw_w, dw_b, cg, cb, kbd, vbd, wo, g1, b1)


def _moba_select_bias(gate, i):
    nb = gate.shape[1]
    blk = lax.broadcasted_iota(jnp.int32, gate.shape, 1).astype(F32)
    past = blk < i.astype(F32)
    gm = jnp.where(past, gate, NEG_INF)
    bias = jnp.full(gate.shape, NEG_INF, F32)
    for _ in range(MOBA_TOPK):
        top = jnp.max(gm, axis=1, keepdims=True)
        first = jnp.min(jnp.where(gm == top, blk, float(nb)), axis=1, keepdims=True)
        taken = blk == first
        bias = jnp.where(taken & past, 0.0, bias)
        gm = jnp.where(taken, NEG_INF, gm)
    return bias


def _moba_mixer_kernel(x_ref, win_ref, kbd_ref, vbd_ref, wo_ref, g1_ref, b1_ref, o_ref,
                       k_sc, vt_sc, kmt_sc, bias_sc, qh_sc, m_sc, l_sc, acc_sc, *, alpha, mix_w, nb):
    tm = x_ref.shape[0]
    heads = mix_w // HEAD_DIM
    i = pl.program_id(1)
    x = x_ref[...]
    u = _dot(x.astype(BF16), win_ref[...])
    q = u[:, 0:mix_w] * (HEAD_DIM ** -0.5 * LOG2_E)
    k = u[:, mix_w:2 * mix_w]
    v = u[:, 2 * mix_w:3 * mix_w]
    qm = u[:, 3 * mix_w:]

    @pl.when(i == 0)
    def _():
        kmt_sc[...] = jnp.zeros(kmt_sc.shape, F32)

    k_sc[i] = k.astype(BF16)
    vt_sc[i] = v.T.astype(BF16)
    kmean = jnp.mean(k, axis=0, keepdims=True)
    lane = lax.broadcasted_iota(jnp.int32, (1, mix_w), 1)
    for h in range(heads):
        in_head = (lane >= h * HEAD_DIM) & (lane < (h + 1) * HEAD_DIM)
        kmt_sc[pl.ds(h * nb + i, 1), :] = jnp.where(in_head, kmean, 0.0)

    q_hi, q_lo = _split_bf16(q)
    km_hi, km_lo = _split_bf16(kmt_sc[...])
    gate_t = _dot_nt(km_hi, q_hi) + _dot_nt(km_hi, q_lo) + _dot_nt(km_lo, q_hi)
    bias_sc[...] = _moba_select_bias(gate_t.reshape(heads, nb, tm), i)

    lane_p = lax.broadcasted_iota(jnp.int32, (tm, 2 * HEAD_DIM), 1)
    for h in range(heads):
        qp = q_hi[:, (h // 2) * 2 * HEAD_DIM:(h // 2 + 1) * 2 * HEAD_DIM]
        keep = (lane_p < HEAD_DIM) if h % 2 == 0 else (lane_p >= HEAD_DIM)
        qh_sc[h] = jnp.where(keep, qp, jnp.zeros_like(qp))

    def scores(j, h):
        cols = slice((h // 2) * 2 * HEAD_DIM, (h // 2 + 1) * 2 * HEAD_DIM)
        return _dot_nt(k_sc[j, :, cols], qh_sc[h])

    rows = lambda h: slice(h * HEAD_DIM, (h + 1) * HEAD_DIM)

    kidx = lax.broadcasted_iota(jnp.int32, (tm, tm), 0)
    qidx = lax.broadcasted_iota(jnp.int32, (tm, tm), 1)
    causal = kidx <= qidx
    def heads_pipelined(j):
        pending = [scores(j, h) for h in range(MOBA_LOOKAHEAD)]
        for h in range(heads):
            if h + MOBA_LOOKAHEAD < heads:
                pending.append(scores(j, h + MOBA_LOOKAHEAD))
            yield h, pending.pop(0)

    def own_block(j, c):
        for h, s in heads_pipelined(j):
            s = jnp.where(causal, s, NEG_INF)
            m = jnp.max(s, axis=0, keepdims=True)
            e = jnp.exp2(s - m)
            m_sc[h] = m
            l_sc[h] = jnp.sum(e, axis=0, keepdims=True)
            acc_sc[rows(h), :] = _dot(vt_sc[j, rows(h), :], e.astype(BF16))
        return c

    lax.fori_loop(i, i + 1, own_block, 0)

    def body(j, c):
        for h, s in heads_pipelined(j):
            b = bias_sc[h, pl.ds(j, 1), :]
            m_old = m_sc[h]
            m_new = jnp.maximum(m_old, jnp.max(s, axis=0, keepdims=True) + b)
            e = jnp.exp2(s - (m_new - b))
            corr = jnp.exp2(m_old - m_new)
            m_sc[h] = m_new
            l_sc[h] = corr * l_sc[h] + jnp.sum(e, axis=0, keepdims=True)
            acc_sc[rows(h), :] = corr * acc_sc[rows(h), :] + _dot(vt_sc[j, rows(h), :], e.astype(BF16))
        return c

    lax.fori_loop(0, i, body, 0)
    for h in range(heads):
        acc_sc[rows(h), :] = acc_sc[rows(h), :] / l_sc[h]

    y_mix = acc_sc[...].T
    y_mem = _mem_attention(qm, kbd_ref[...], vbd_ref[...])
    o_ref[...] = _out_proj_ln(x, y_mix, y_mem, wo_ref, g1_ref, b1_ref, alpha, mix_w)


def _moba_mixer(x2d, batch, w_in, kbd, vbd, wo, g1, b1, alpha):
    t, d = x2d.shape
    tm = MOBA_BLOCK
    nb = t // batch // tm
    mix_w = d - MEM_WIDTH
    heads = mix_w // HEAD_DIM
    full = lambda shape: pl.BlockSpec(shape, lambda b, j: (0,) * len(shape))
    return pl.pallas_call(
        functools.partial(_moba_mixer_kernel, alpha=alpha, mix_w=mix_w, nb=nb),
        out_shape=jax.ShapeDtypeStruct((t, d), F32),
        grid=(batch, nb),
        in_specs=[
            pl.BlockSpec((tm, d), lambda b, j: (b * nb + j, 0)),
            full(w_in.shape),
            pl.BlockSpec((None,) + kbd.shape[1:], lambda b, j: (b, 0, 0)),
            pl.BlockSpec((None,) + vbd.shape[1:], lambda b, j: (b, 0, 0)),
            full(wo.shape), full(g1.shape), full(b1.shape),
        ],
        out_specs=pl.BlockSpec((tm, d), lambda b, j: (b * nb + j, 0)),
        scratch_shapes=[
            pltpu.VMEM((nb, tm, mix_w), BF16),
            pltpu.VMEM((nb, mix_w, tm), BF16),
            pltpu.VMEM((heads * nb, mix_w), F32),
            pltpu.VMEM((heads, nb, tm), F32),
            pltpu.VMEM((heads, tm, 2 * HEAD_DIM), BF16),
            pltpu.VMEM((heads, 1, tm), F32),
            pltpu.VMEM((heads, 1, tm), F32),
            pltpu.VMEM((mix_w, tm), F32),
        ],
        compiler_params=pltpu.CompilerParams(
            dimension_semantics=("arbitrary", "arbitrary"), vmem_limit_bytes=VMEM_LIMIT),
        name="moba_mixer",
    )(x2d, w_in, kbd, vbd, wo, g1, b1)


def _router_kernel(x_ref, whi_ref, wlo_ref, br_ref, tri_ref, info_ref, sel_ref, cnt_ref, run_sc):
    step = pl.program_id(0)

    @pl.when(step == 0)
    def _():
        run_sc[...] = jnp.zeros(run_sc.shape, F32)

    x_hi, x_lo = _split_bf16(x_ref[...])
    logits = (_dot_nt(whi_ref[...], x_hi) + _dot_nt(wlo_ref[...], x_hi) + _dot_nt(whi_ref[...], x_lo)
              + br_ref[...])[0:ROUTER_ROWS, :]
    row = lax.broadcasted_iota(jnp.int32, logits.shape, 0).astype(F32)
    cmax = lambda a: jnp.max(a, axis=0, keepdims=True)
    cmin = lambda a: jnp.min(a, axis=0, keepdims=True)
    csum = lambda a: jnp.sum(a, axis=0, keepdims=True)
    big = float(2 * ROUTER_ROWS)

    is_g = row < N_GROUPS
    gl = jnp.where(is_g, logits, NEG_INF)
    gmax = cmax(gl)
    gidx = cmin(jnp.where(gl == gmax, row, big))
    g_w = 1.0 / csum(jnp.where(is_g, jnp.exp(gl - gmax), 0.0))

    lo = ROUTER_OFF + EXPERTS_PER_GROUP * gidx
    el = jnp.where((row >= lo) & (row < lo + EXPERTS_PER_GROUP), logits, NEG_INF)
    v0 = cmax(el)
    i0 = cmin(jnp.where(el == v0, row, big))
    el1 = jnp.where(row == i0, NEG_INF, el)
    v1 = cmax(el1)
    i1 = cmin(jnp.where(el1 == v1, row, big))
    t = jnp.exp(v1 - v0)
    w0 = g_w / (1.0 + t)
    w1 = g_w * t / (1.0 + t)

    pick0 = row == i0
    pick1 = row == i1
    onehot = jnp.where(pick0 | pick1, 1.0, 0.0)
    before = _dot(onehot.astype(BF16), tri_ref[...]) + run_sc[...]
    pos0 = csum(jnp.where(pick0, before, 0.0))
    pos1 = csum(jnp.where(pick1, before, 0.0))
    run_sc[...] = run_sc[...] + jnp.sum(onehot, axis=1, keepdims=True)
    cnt_ref[...] = run_sc[...]

    vals = (i0 - ROUTER_OFF, i1 - ROUTER_OFF, pos0, pos1, w0, w1)
    row8 = lax.broadcasted_iota(jnp.int32, sel_ref.shape, 0)
    sel = jnp.zeros(sel_ref.shape, F32)
    for c, val in enumerate(vals):
        sel = jnp.where(row8 == c, val, sel)
    sel_ref[...] = sel
    rowl = lax.broadcasted_iota(jnp.int32, (ROUTER_LANES, x_ref.shape[0]), 0)
    info_ref[...] = jnp.where(rowl == 4, w0, jnp.where(rowl == 5, w1, 0.0)).T


def _router(x2d, whi, wlo, br, tri):
    t, d = x2d.shape
    tm = TOKEN_TILE
    full = lambda shape: pl.BlockSpec(shape, lambda s: (0,) * len(shape))
    return pl.pallas_call(
        _router_kernel,
        out_shape=(jax.ShapeDtypeStruct((t, ROUTER_LANES), F32),
                   jax.ShapeDtypeStruct((t // tm, SUBLANES, tm), F32),
                   jax.ShapeDtypeStruct((ROUTER_ROWS, tm), F32)),
        grid=(t // tm,),
        in_specs=[pl.BlockSpec((tm, d), lambda s: (s, 0)),
                  full(whi.shape), full(wlo.shape), full(br.shape), full(tri.shape)],
        out_specs=(pl.BlockSpec((tm, ROUTER_LANES), lambda s: (s, 0)),
                   pl.BlockSpec((None, SUBLANES, tm), lambda s: (s, 0, 0)),
                   full((ROUTER_ROWS, tm))),
        scratch_shapes=[pltpu.VMEM((ROUTER_ROWS, tm), F32)],
        compiler_params=pltpu.CompilerParams(dimension_semantics=("arbitrary",)),
        name="router",
    )(x2d, whi, wlo, br, tri)


def _row_copy(src, src_row, dst, dst_row, sem):
    return pltpu.make_async_copy(src.at[pl.ds(src_row, 1), :], dst.at[pl.ds(dst_row, 1), :], sem)


def _slots_kernel(sel_ref, base_ref, o_ref):
    sel = sel_ref[...]
    g, _, tm = sel.shape
    row = lax.broadcasted_iota(jnp.int32, (g, ROUTER_ROWS, tm), 1).astype(F32)
    base = base_ref[...][None]
    csum = lambda a: jnp.sum(a, axis=1, keepdims=True)
    s0 = csum(jnp.where(row == sel[:, 0:1, :] + ROUTER_OFF, base, 0.0)) + sel[:, 2:3, :]
    s1 = csum(jnp.where(row == sel[:, 1:2, :] + ROUTER_OFF, base, 0.0)) + sel[:, 3:4, :]
    row8 = lax.broadcasted_iota(jnp.int32, o_ref.shape, 1)
    o_ref[...] = jnp.where(row8 == 0, s0, jnp.where(row8 == 1, s1, 0.0)).astype(jnp.int32)


def _slots(sel, base_col):
    nt, _, tm = sel.shape
    g = SLOT_TILES_PER_STEP if nt % SLOT_TILES_PER_STEP == 0 else nt
    return pl.pallas_call(
        _slots_kernel,
        out_shape=jax.ShapeDtypeStruct((nt, SUBLANES, tm), jnp.int32),
        grid=(nt // g,),
        in_specs=[pl.BlockSpec((g, SUBLANES, tm), lambda s: (s, 0, 0)),
                  pl.BlockSpec(base_col.shape, lambda s: (0, 0))],
        out_specs=pl.BlockSpec((g, SUBLANES, tm), lambda s: (s, 0, 0)),
        compiler_params=pltpu.CompilerParams(dimension_semantics=("arbitrary",)),
        name="slots",
    )(sel, base_col)


def _dispatch_kernel(nv_ref, s0_ref, s1_ref, x_ref, xs_ref, zero_sc, sem, fill_sem):
    tm = x_ref.shape[0]

    @pl.when(pl.program_id(0) == 0)
    def _():
        zero_sc[...] = jnp.zeros(zero_sc.shape, F32)

        def fill(t):
            return pltpu.make_async_copy(zero_sc, xs_ref.at[pl.ds(pl.multiple_of(t * tm, tm), tm), :], fill_sem)

        def start_fill(t, c):
            @pl.when(nv_ref[t] < tm)
            def _():
                fill(t).start()
            return c

        def wait_fill(t, c):
            @pl.when(nv_ref[t] < tm)
            def _():
                fill(t).wait()
            return c

        lax.fori_loop(0, nv_ref.shape[0], start_fill, 0)
        lax.fori_loop(0, nv_ref.shape[0], wait_fill, 0)

    def start(r, c):
        _row_copy(x_ref, r, xs_ref, s0_ref[0, r], sem).start(priority=0)
        _row_copy(x_ref, r, xs_ref, s1_ref[0, r], sem).start(priority=1)
        return c

    lax.fori_loop(0, tm, start, 0, unroll=DMA_UNROLL)
    for _ in range(2):
        pltpu.make_async_copy(x_ref, xs_ref.at[pl.ds(0, tm), :], sem).wait()


def _dispatch(x2d, slot0, slot1, tile_valid):
    t, d = x2d.shape
    tm = TOKEN_TILE
    n_slots = tile_valid.shape[0] * tm
    smem_rows = pl.BlockSpec((None, 1, tm), lambda s, nv: (s, 0, 0), memory_space=pltpu.SMEM)
    return pl.pallas_call(
        _dispatch_kernel,
        out_shape=jax.ShapeDtypeStruct((n_slots, d), F32),
        grid_spec=pltpu.PrefetchScalarGridSpec(
            num_scalar_prefetch=1,
            grid=(t // tm,),
            in_specs=[smem_rows, smem_rows, pl.BlockSpec((tm, d), lambda s, nv: (s, 0))],
            out_specs=pl.BlockSpec(memory_space=pl.ANY),
            scratch_shapes=[pltpu.VMEM((tm, d), F32), pltpu.SemaphoreType.DMA, pltpu.SemaphoreType.DMA],
        ),
        compiler_params=pltpu.CompilerParams(dimension_semantics=("arbitrary",)),
        name="dispatch",
    )(tile_valid, slot0, slot1, x2d)


def _expert_kernel(te_ref, nv_ref, first_ref, buf_ref, next_ref, xs_ref, wg_hbm, wu_hbm, wd_hbm, ys_ref,
                   wg_buf, wu_buf, wd_buf, wg_sc, wu_sc, wd_sc, sems, *, base):
    t = pl.program_id(0)
    used = nv_ref[t] > 0

    def weight_copies(e, b):
        return (pltpu.make_async_copy(wg_hbm.at[base + e], wg_buf.at[b], sems.at[b, 0]),
                pltpu.make_async_copy(wu_hbm.at[base + e], wu_buf.at[b], sems.at[b, 1]),
                pltpu.make_async_copy(wd_hbm.at[base + e], wd_buf.at[b], sems.at[b, 2]))

    @pl.when(t == 0)
    def _():
        for c in weight_copies(te_ref[0], 0):
            c.start()

    @pl.when(first_ref[t] == 1)
    def _():
        b = buf_ref[t]
        for c in weight_copies(te_ref[t], b):
            c.wait()
        wg_sc[...] = wg_buf[b].astype(BF16)
        wu_sc[...] = wu_buf[b].astype(BF16)
        wd_sc[...] = wd_buf[b].astype(BF16)

        @pl.when(next_ref[t] >= 0)
        def _():
            for c in weight_copies(next_ref[t], 1 - b):
                c.start()

    @pl.when(used)
    def _():
        xb = xs_ref[...].astype(BF16)
        hg = _dot(xb, wg_sc[...])
        hu = _dot(xb, wu_sc[...])
        h = hg * jax.nn.sigmoid(hg) * hu
        ys_ref[...] = _dot(h.astype(BF16), wd_sc[...])

    @pl.when(jnp.logical_not(used))
    def _():
        ys_ref[...] = jnp.zeros(ys_ref.shape, F32)


def _expert_mlp(xs, tile_expert, tile_valid, tile_first, tile_buf, tile_next, w_gate, w_up, w_down, layer):
    ns, d = xs.shape
    f = w_gate.shape[-1]
    tm = TOKEN_TILE
    hbm = pl.BlockSpec(memory_space=pl.ANY)
    return pl.pallas_call(
        functools.partial(_expert_kernel, base=layer * N_EXPERTS),
        out_shape=jax.ShapeDtypeStruct((ns, d), F32),
        grid_spec=pltpu.PrefetchScalarGridSpec(
            num_scalar_prefetch=5,
            grid=(ns // tm,),
            in_specs=[pl.BlockSpec((tm, d), lambda t, *_: (t, 0)), hbm, hbm, hbm],
            out_specs=pl.BlockSpec((tm, d), lambda t, *_: (t, 0)),
            scratch_shapes=[pltpu.VMEM((2, d, f), F32), pltpu.VMEM((2, d, f), F32), pltpu.VMEM((2, f, d), F32),
                            pltpu.VMEM((d, f), BF16), pltpu.VMEM((d, f), BF16), pltpu.VMEM((f, d), BF16),
                            pltpu.SemaphoreType.DMA((2, 3))],
        ),
        compiler_params=pltpu.CompilerParams(
            dimension_semantics=("arbitrary",), vmem_limit_bytes=VMEM_LIMIT),
        name="expert_mlp",
    )(tile_expert, tile_valid, tile_first, tile_buf, tile_next, xs, w_gate, w_up, w_down)


def _combine_kernel(s0_ref, s1_ref, info_ref, x_ref, ys_ref, g_ref, b_ref, o_ref, y0_sc, y1_sc, sem,
                    *, alpha):
    tm = x_ref.shape[0]

    def start(r, c):
        _row_copy(ys_ref, s0_ref[0, r], y0_sc, r, sem).start(priority=0)
        _row_copy(ys_ref, s1_ref[0, r], y1_sc, r, sem).start(priority=1)
        return c

    lax.fori_loop(0, tm, start, 0, unroll=DMA_UNROLL)
    for dst in (y0_sc, y1_sc):
        pltpu.make_async_copy(ys_ref.at[pl.ds(0, tm), :], dst, sem).wait()
    info = info_ref[...]
    f = info[:, 4:5] * y0_sc[...] + info[:, 5:6] * y1_sc[...]
    o_ref[...] = _ln(alpha * x_ref[...] + f, g_ref[...], b_ref[...])


def _combine(x2d, info, ys, slot0, slot1, g2, b2, alpha):
    t, d = x2d.shape
    tm = TOKEN_TILE
    smem_rows = pl.BlockSpec((None, 1, tm), lambda s: (s, 0, 0), memory_space=pltpu.SMEM)
    full = lambda shape: pl.BlockSpec(shape, lambda s: (0,) * len(shape))
    return pl.pallas_call(
        functools.partial(_combine_kernel, alpha=alpha),
        out_shape=jax.ShapeDtypeStruct((t, d), F32),
        grid=(t // tm,),
        in_specs=[smem_rows, smem_rows,
                  pl.BlockSpec((tm, ROUTER_LANES), lambda s: (s, 0)),
                  pl.BlockSpec((tm, d), lambda s: (s, 0)),
                  pl.BlockSpec(memory_space=pl.ANY),
                  full(g2.shape), full(b2.shape)],
        out_specs=pl.BlockSpec((tm, d), lambda s: (s, 0)),
        scratch_shapes=[pltpu.VMEM((tm, d), F32), pltpu.VMEM((tm, d), F32), pltpu.SemaphoreType.DMA],
        compiler_params=pltpu.CompilerParams(dimension_semantics=("arbitrary",)),
        name="combine",
    )(slot0, slot1, info, x2d, ys, g2, b2)


def _moe(x2d, layer, w_rg, b_rg, w_re, b_re, w_gate, w_up, w_down, g2, b2, alpha):
    t, d = x2d.shape
    tm = TOKEN_TILE
    wr = jnp.zeros((ROUTER_LANES, d), F32).at[0:N_GROUPS].set(w_rg.T)
    wr = wr.at[ROUTER_OFF:ROUTER_OFF + N_EXPERTS].set(w_re.T)
    br = jnp.zeros((ROUTER_LANES, 1), F32).at[0:N_GROUPS, 0].set(b_rg)
    br = br.at[ROUTER_OFF:ROUTER_OFF + N_EXPERTS, 0].set(b_re)
    whi, wlo = _split_bf16(wr)
    ridx = lax.broadcasted_iota(jnp.int32, (tm, tm), 0)
    cidx = lax.broadcasted_iota(jnp.int32, (tm, tm), 1)
    tri = (ridx < cidx).astype(BF16)
    info, sel, cnt = _router(x2d, whi, wlo, br, tri)

    counts = cnt[ROUTER_OFF:ROUTER_OFF + N_EXPERTS, 0].astype(jnp.int32)
    padded = (counts + tm - 1) // tm * tm
    ends = jnp.cumsum(padded)
    base = ends - padded
    base_col = jnp.zeros((ROUTER_ROWS, 1), F32).at[ROUTER_OFF:ROUTER_OFF + N_EXPERTS, 0].set(base.astype(F32))
    slots = _slots(sel, base_col)
    slot0, slot1 = slots[:, 0:1, :], slots[:, 1:2, :]

    n_slots = 2 * t + N_EXPERTS * tm
    n_tiles = n_slots // tm
    n_used = ends[-1] // tm
    tile_ids = jnp.arange(n_tiles, dtype=jnp.int32)
    tile_expert = jnp.sum((jnp.minimum(tile_ids, n_used - 1)[:, None] * tm >= ends[None, :]).astype(jnp.int32),
                          axis=1)
    lo = jnp.maximum(base[None, :], tile_ids[:, None] * tm)
    hi = jnp.minimum((base + counts)[None, :], (tile_ids[:, None] + 1) * tm)
    tile_valid = jnp.sum(jnp.maximum(hi - lo, 0), axis=1).astype(jnp.int32)

    used = tile_valid > 0
    tile_first = (used & ((tile_ids == 0) | (tile_expert != jnp.roll(tile_expert, 1)))).astype(jnp.int32)
    tile_buf = ((jnp.cumsum(tile_first) - 1) % 2).astype(jnp.int32)
    experts = jnp.arange(N_EXPERTS, dtype=jnp.int32)
    later_nonempty = (experts[None, :] > experts[:, None]) & (counts[None, :] > 0)
    next_expert = jnp.min(jnp.where(later_nonempty, experts[None, :], N_EXPERTS), axis=1)
    next_expert = jnp.where(next_expert < N_EXPERTS, next_expert, -1)
    tile_next = jnp.sum(jnp.where(tile_expert[:, None] == experts[None, :], next_expert[None, :], 0),
                        axis=1).astype(jnp.int32)

    xs = _dispatch(x2d, slot0, slot1, tile_valid)
    ys = _expert_mlp(xs, tile_expert, tile_valid, tile_first, tile_buf, tile_next, w_gate, w_up, w_down, layer)
    return _combine(x2d, info, ys, slot0, slot1, g2, b2, alpha)


def kernel(x, mem, w_mem_kv, conv_w_in, conv_dw_w, conv_dw_b, conv_ln_g, conv_ln_b, moba_w_in, w_o,
           ln1_g, ln1_b, w_rg, b_rg, w_re, b_re, w_gate, w_up, w_down, ln2_g, ln2_b):
    batch, seq, d = x.shape
    depth = w_o.shape[0]
    alpha = (2 * depth) ** 0.25
    t = batch * seq
    row = lambda a: a.reshape(1, -1)

    kbd, vbd = _memkv(mem, w_mem_kv)

    e_shape = w_gate.shape
    w_gate = w_gate.reshape((-1,) + e_shape[-2:])
    w_up = w_up.reshape((-1,) + e_shape[-2:])
    w_down = w_down.reshape((-1,) + w_down.shape[-2:])

    x2d = x.reshape(t, d)
    for i in range(depth):
        j = i // 2
        wo = w_o[i].astype(BF16)
        if i % 2 == 0:
            dw_w = jnp.zeros((CONV_HALO, conv_dw_w.shape[-1]), F32).at[0:CONV_WIDTH].set(conv_dw_w[j])
            x2d = _conv_mixer(x2d, batch, conv_w_in[j].astype(BF16), dw_w, row(conv_dw_b[j]),
                              row(conv_ln_g[j]), row(conv_ln_b[j]), kbd, vbd, wo,
                              row(ln1_g[i]), row(ln1_b[i]), alpha)
        else:
            x2d = _moba_mixer(x2d, batch, moba_w_in[j].astype(BF16), kbd, vbd, wo,
                              row(ln1_g[i]), row(ln1_b[i]), alpha)
        x2d = _moe(x2d, i, w_rg[i], b_rg[i], w_re[i], b_re[i], w_gate, w_up, w_down,
                   row(ln2_g[i]), row(ln2_b[i]), alpha)
    return x2d.reshape(batch, seq, d)
```

```python
import functools

import jax
import jax.numpy as jnp
from jax import lax
from jax.experimental import pallas as pl
from jax.experimental.pallas import tpu as pltpu

F32 = jnp.float32
BF16 = jnp.bfloat16

HEAD_DIM = 64
MEM_LEN = 256
MEM_HEADS = 4
MEM_WIDTH = MEM_HEADS * HEAD_DIM
CONV_WIDTH = 31
MOBA_BLOCK = 256
MOBA_TOPK = 3
HEAD_ROWS = HEAD_DIM + 16
MOBA_LOOKAHEAD = 6
N_GROUPS = 4
EXPERTS_PER_GROUP = 8
N_EXPERTS = N_GROUPS * EXPERTS_PER_GROUP
LN_EPS = 1e-5

LANES = 128
SUBLANES = 8
TOKEN_TILE = 256
CONV_HALO = 32
CONV_CHUNK = 16
CONV_CHUNKS_PER_STEP = 2
IN_PROJ_CHUNK = 256
ROUTER_LANES = 128
ROUTER_ROWS = 40
ROUTER_OFF = N_GROUPS
DMA_UNROLL = 8
SLOT_TILES_PER_STEP = 16
VMEM_LIMIT = 56 * 1024 * 1024

NEG_INF = float("-inf")
LOG2_E = 1.4426950408889634


def _ln(z, g, b):
    mu = jnp.mean(z, axis=-1, keepdims=True)
    zc = z - mu
    var = jnp.mean(zc * zc, axis=-1, keepdims=True)
    return zc * lax.rsqrt(var + LN_EPS) * g + b


def _dot(a, b):
    return jnp.dot(a, b, preferred_element_type=F32)


def _dot_nt(a, b):
    return lax.dot_general(a, b, (((1,), (1,)), ((), ())), preferred_element_type=F32)


def _split_bf16(x):
    hi = x.astype(BF16)
    lo = (x - hi.astype(F32)).astype(BF16)
    return hi, lo


def _memkv_kernel(mem_ref, w_ref, kbd_ref, vbd_ref):
    kv = _dot(mem_ref[...].astype(BF16), w_ref[...].astype(BF16))
    k_t = kv[:, 0:MEM_WIDTH].T
    v = kv[:, MEM_WIDTH:]
    r = lax.broadcasted_iota(jnp.int32, kbd_ref.shape, 0)
    c = lax.broadcasted_iota(jnp.int32, kbd_ref.shape, 1)
    kbd_ref[...] = jnp.where(r // HEAD_DIM == c // MEM_LEN,
                             jnp.concatenate([k_t] * MEM_HEADS, axis=1), 0.0).astype(BF16)
    r = lax.broadcasted_iota(jnp.int32, vbd_ref.shape, 0)
    c = lax.broadcasted_iota(jnp.int32, vbd_ref.shape, 1)
    vbd_ref[...] = jnp.where(r // MEM_LEN == c // HEAD_DIM,
                             jnp.concatenate([v] * MEM_HEADS, axis=0), 0.0).astype(BF16)


def _memkv(mem, w):
    batch, m, d = mem.shape
    return pl.pallas_call(
        _memkv_kernel,
        out_shape=(jax.ShapeDtypeStruct((batch, MEM_WIDTH, MEM_HEADS * m), BF16),
                   jax.ShapeDtypeStruct((batch, MEM_HEADS * m, MEM_WIDTH), BF16)),
        grid=(batch,),
        in_specs=[pl.BlockSpec((None, m, d), lambda b: (b, 0, 0)),
                  pl.BlockSpec(w.shape, lambda b: (0, 0))],
        out_specs=(pl.BlockSpec((None, MEM_WIDTH, MEM_HEADS * m), lambda b: (b, 0, 0)),
                   pl.BlockSpec((None, MEM_HEADS * m, MEM_WIDTH), lambda b: (b, 0, 0))),
        compiler_params=pltpu.CompilerParams(dimension_semantics=("arbitrary",)),
        name="memkv",
    )(mem, w)


def _mem_attention(qm, kbd, vbd):
    s = _dot(qm.astype(BF16), kbd) * (HEAD_DIM ** -0.5)
    parts = []
    for h in range(MEM_HEADS):
        seg = s[:, h * MEM_LEN:(h + 1) * MEM_LEN]
        m = jnp.max(seg, axis=-1, keepdims=True)
        e = jnp.exp(seg - m)
        parts.append(e / jnp.sum(e, axis=-1, keepdims=True))
    p = jnp.concatenate(parts, axis=-1)
    return _dot(p.astype(BF16), vbd)


def _out_proj_ln(x, y_mix, y_mem, wo_ref, g_ref, b_ref, alpha, mix_w):
    y = _dot(y_mix.astype(BF16), wo_ref[0:mix_w, :]) + _dot(y_mem.astype(BF16), wo_ref[mix_w:, :])
    return _ln(alpha * x + y, g_ref[...], b_ref[...])


def _conv_mixer_kernel(x_ref, xn_ref, win_ref, dww_ref, dwb_ref, cg_ref, cb_ref, kbd_ref, vbd_ref,
                       wo_ref, g1_ref, b1_ref, o_ref, ustage, xn_sc, hbuf, zbuf, cbuf, *, alpha, mix_w):
    tm = x_ref.shape[0]
    n_cc = win_ref.shape[0]
    per = mix_w // IN_PROJ_CHUNK
    j = pl.program_id(1)

    @pl.when((pl.program_id(0) == 0) & (j == 0))
    def _():
        xb = x_ref[...].astype(BF16)
        for c in range(n_cc):
            ustage[c] = _dot(xb, win_ref[c])

    x = x_ref[...]
    qm = ustage[2 * per]
    h = jnp.concatenate([ustage[c] * jax.nn.sigmoid(ustage[per + c]) for c in range(per)], axis=1)
    xn_sc[...] = xn_ref[...].astype(BF16)

    @pl.when(j == 0)
    def _():
        hbuf[0:CONV_HALO, :] = jnp.zeros((CONV_HALO, mix_w), F32)

    hbuf[CONV_HALO:CONV_HALO + tm, :] = h

    shifted_rows = CONV_HALO + tm - SUBLANES
    for a in range(1, SUBLANES):
        zbuf[a - 1, 0:shifted_rows, :] = hbuf[a:a + shifted_rows, :]

    first = CONV_HALO - (CONV_WIDTH - 1)
    aligned = lambda v, m: v if isinstance(v, int) else pl.multiple_of(v, m)

    def conv_rows(r0):
        acc = jnp.broadcast_to(dwb_ref[...], (CONV_CHUNK, mix_w))
        for a in range(SUBLANES):
            offs = [first + k - a for k in range(CONV_WIDTH) if (first + k) % SUBLANES == a]
            r = aligned(r0 + offs[0], SUBLANES)
            span = offs[-1] - offs[0] + CONV_CHUNK
            z = hbuf[pl.ds(r, span), :] if a == 0 else zbuf[a - 1, pl.ds(r, span), :]
            for off in offs:
                k = off + a - first
                rows = z[off - offs[0]:off - offs[0] + CONV_CHUNK, :]
                acc = acc + jnp.tile(dww_ref[k], (CONV_CHUNK // SUBLANES, 1)) * rows
        cbuf[pl.ds(r0, CONV_CHUNK), :] = acc

    def project_next(c):
        ustage[c] = _dot(xn_sc[...], win_ref[c])

    step_rows = CONV_CHUNK * CONV_CHUNKS_PER_STEP
    paired = min(n_cc, tm // step_rows)

    def paired_step(c, carry):
        project_next(c)
        for q in range(CONV_CHUNKS_PER_STEP):
            conv_rows(pl.multiple_of(c * step_rows + q * CONV_CHUNK, CONV_CHUNK))
        return carry

    lax.fori_loop(0, paired, paired_step, 0)
    for r0 in range(paired * step_rows, tm, CONV_CHUNK):
        conv_rows(r0)
    for c in range(paired, n_cc):
        project_next(c)

    hbuf[0:CONV_HALO, :] = hbuf[tm:tm + CONV_HALO, :]

    cn = _ln(cbuf[...], cg_ref[...], cb_ref[...])
    y_mix = cn * jax.nn.sigmoid(cn)
    y_mem = _mem_attention(qm, kbd_ref[...], vbd_ref[...])
    o_ref[...] = _out_proj_ln(x, y_mix, y_mem, wo_ref, g1_ref, b1_ref, alpha, mix_w)


def _conv_mixer(x2d, batch, w_in, dw_w, dw_b, cg, cb, kbd, vbd, wo, g1, b1, alpha):
    t, d = x2d.shape
    tm = TOKEN_TILE
    nj = t // batch // tm
    mix_w = d - MEM_WIDTH
    full = lambda shape: pl.BlockSpec(shape, lambda b, j: (0,) * len(shape))
    return pl.pallas_call(
        functools.partial(_conv_mixer_kernel, alpha=alpha, mix_w=mix_w),
        out_shape=jax.ShapeDtypeStruct((t, d), F32),
        grid=(batch, nj),
        in_specs=[
            pl.BlockSpec((tm, d), lambda b, j: (b * nj + j, 0)),
            pl.BlockSpec((tm, d), lambda b, j: (jnp.minimum(b * nj + j + 1, batch * nj - 1), 0)),
            full(w_in.shape), full(dw_w.shape), full(dw_b.shape), full(cg.shape), full(cb.shape),
            pl.BlockSpec((None,) + kbd.shape[1:], lambda b, j: (b, 0, 0)),
            pl.BlockSpec((None,) + vbd.shape[1:], lambda b, j: (b, 0, 0)),
            full(wo.shape), full(g1.shape), full(b1.shape),
        ],
        out_specs=pl.BlockSpec((tm, d), lambda b, j: (b * nj + j, 0)),
        scratch_shapes=[pltpu.VMEM((w_in.shape[0], tm, IN_PROJ_CHUNK), F32),
                        pltpu.VMEM((tm, d), BF16),
                        pltpu.VMEM((CONV_HALO + tm, mix_w), F32),
                        pltpu.VMEM((SUBLANES - 1, CONV_HALO + tm, mix_w), F32),
                        pltpu.VMEM((tm, mix_w), F32)],
        compiler_params=pltpu.CompilerParams(
            dimension_semantics=("arbitrary", "arbitrary"), vmem_limit_bytes=VMEM_LIMIT),
        name="conv_mixer",
    )(x2d, x2d, w_in, dw_w, dw_b, cg, cb, kbd, vbd, wo, g1, b1)


def _moba_select_bias(gate, i):
    nb = gate.shape[1]
    blk = lax.broadcasted_iota(jnp.int32, gate.shape, 1).astype(F32)
    past = blk < i.astype(F32)
    gm = jnp.where(past, gate, NEG_INF)
    bias = jnp.full(gate.shape, NEG_INF, F32)
    for _ in range(MOBA_TOPK):
        top = jnp.max(gm, axis=1, keepdims=True)
        first = jnp.min(jnp.where(gm == top, blk, float(nb)), axis=1, keepdims=True)
        taken = blk == first
        bias = jnp.where(taken & past, 0.0, bias)
        gm = jnp.where(taken, NEG_INF, gm)
    return bias


def _moba_mixer_kernel(x_ref, win_ref, kbd_ref, vbd_ref, wo_ref, g1_ref, b1_ref, o_ref,
                       k_sc, vt_sc, kmt_sc, bias_sc, qh_sc, m_sc, acc_sc, yt_sc, *, alpha, mix_w, nb):
    tm = x_ref.shape[0]
    heads = mix_w // HEAD_DIM
    i = pl.program_id(1)
    x = x_ref[...]
    u = _dot(x.astype(BF16), win_ref[...])
    q = u[:, 0:mix_w] * (HEAD_DIM ** -0.5 * LOG2_E)
    k = u[:, mix_w:2 * mix_w]
    v = u[:, 2 * mix_w:3 * mix_w]
    qm = u[:, 3 * mix_w:]

    @pl.when(i == 0)
    def _():
        kmt_sc[...] = jnp.zeros(kmt_sc.shape, F32)

    k_sc[i] = k.astype(BF16)
    v_t = v.T.astype(BF16)
    for h in range(heads):
        vt_sc[i, h * HEAD_ROWS:h * HEAD_ROWS + HEAD_DIM, :] = v_t[h * HEAD_DIM:(h + 1) * HEAD_DIM, :]
        vt_sc[i, h * HEAD_ROWS + HEAD_DIM:(h + 1) * HEAD_ROWS, :] = jnp.ones((HEAD_ROWS - HEAD_DIM, tm), BF16)
    kmean = jnp.mean(k, axis=0, keepdims=True)
    lane = lax.broadcasted_iota(jnp.int32, (1, mix_w), 1)
    for h in range(heads):
        in_head = (lane >= h * HEAD_DIM) & (lane < (h + 1) * HEAD_DIM)
        kmt_sc[pl.ds(h * nb + i, 1), :] = jnp.where(in_head, kmean, 0.0)

    q_hi, q_lo = _split_bf16(q)
    km_hi, km_lo = _split_bf16(kmt_sc[...])
    gate_t = _dot_nt(km_hi, q_hi) + _dot_nt(km_hi, q_lo) + _dot_nt(km_lo, q_hi)
    bias_sc[...] = _moba_select_bias(gate_t.reshape(heads, nb, tm), i)

    lane_p = lax.broadcasted_iota(jnp.int32, (tm, 2 * HEAD_DIM), 1)
    for h in range(heads):
        qp = q_hi[:, (h // 2) * 2 * HEAD_DIM:(h // 2 + 1) * 2 * HEAD_DIM]
        keep = (lane_p < HEAD_DIM) if h % 2 == 0 else (lane_p >= HEAD_DIM)
        qh_sc[h] = jnp.where(keep, qp, jnp.zeros_like(qp))

    def scores(j, h):
        cols = slice((h // 2) * 2 * HEAD_DIM, (h // 2 + 1) * 2 * HEAD_DIM)
        return _dot_nt(k_sc[j, :, cols], qh_sc[h])

    rows = lambda h: slice(h * HEAD_ROWS, (h + 1) * HEAD_ROWS)

    kidx = lax.broadcasted_iota(jnp.int32, (tm, tm), 0)
    qidx = lax.broadcasted_iota(jnp.int32, (tm, tm), 1)
    causal = kidx <= qidx
    def heads_pipelined(j):
        pending = [scores(j, h) for h in range(MOBA_LOOKAHEAD)]
        for h in range(heads):
            if h + MOBA_LOOKAHEAD < heads:
                pending.append(scores(j, h + MOBA_LOOKAHEAD))
            yield h, pending.pop(0)

    def own_block(j, c):
        for h, s in heads_pipelined(j):
            s = jnp.where(causal, s, NEG_INF)
            m = jnp.max(s, axis=0, keepdims=True)
            e = jnp.exp2(s - m)
            m_sc[h] = m
            acc_sc[rows(h), :] = _dot(vt_sc[j, rows(h), :], e.astype(BF16))
        return c

    lax.fori_loop(i, i + 1, own_block, 0)

    def body(j, c):
        for h, s in heads_pipelined(j):
            b = bias_sc[h, pl.ds(j, 1), :]
            m_old = m_sc[h]
            m_new = jnp.maximum(m_old, jnp.max(s, axis=0, keepdims=True) + b)
            e = jnp.exp2(s - (m_new - b))
            corr = jnp.exp2(m_old - m_new)
            m_sc[h] = m_new
            acc_sc[rows(h), :] = corr * acc_sc[rows(h), :] + _dot(vt_sc[j, rows(h), :], e.astype(BF16))
        return c

    lax.fori_loop(0, i, body, 0)
    for h in range(heads):
        r0 = h * HEAD_ROWS
        yt_sc[h * HEAD_DIM:(h + 1) * HEAD_DIM, :] = (acc_sc[r0:r0 + HEAD_DIM, :]
                                                     / acc_sc[r0 + HEAD_DIM:r0 + HEAD_DIM + 1, :])

    y_mix = yt_sc[...].T
    y_mem = _mem_attention(qm, kbd_ref[...], vbd_ref[...])
    o_ref[...] = _out_proj_ln(x, y_mix, y_mem, wo_ref, g1_ref, b1_ref, alpha, mix_w)


def _moba_mixer(x2d, batch, w_in, kbd, vbd, wo, g1, b1, alpha):
    t, d = x2d.shape
    tm = MOBA_BLOCK
    nb = t // batch // tm
    mix_w = d - MEM_WIDTH
    heads = mix_w // HEAD_DIM
    full = lambda shape: pl.BlockSpec(shape, lambda b, j: (0,) * len(shape))
    return pl.pallas_call(
        functools.partial(_moba_mixer_kernel, alpha=alpha, mix_w=mix_w, nb=nb),
        out_shape=jax.ShapeDtypeStruct((t, d), F32),
        grid=(batch, nb),
        in_specs=[
            pl.BlockSpec((tm, d), lambda b, j: (b * nb + j, 0)),
            full(w_in.shape),
            pl.BlockSpec((None,) + kbd.shape[1:], lambda b, j: (b, 0, 0)),
            pl.BlockSpec((None,) + vbd.shape[1:], lambda b, j: (b, 0, 0)),
            full(wo.shape), full(g1.shape), full(b1.shape),
        ],
        out_specs=pl.BlockSpec((tm, d), lambda b, j: (b * nb + j, 0)),
        scratch_shapes=[
            pltpu.VMEM((nb, tm, mix_w), BF16),
            pltpu.VMEM((nb, heads * HEAD_ROWS, tm), BF16),
            pltpu.VMEM((heads * nb, mix_w), F32),
            pltpu.VMEM((heads, nb, tm), F32),
            pltpu.VMEM((heads, tm, 2 * HEAD_DIM), BF16),
            pltpu.VMEM((heads, 1, tm), F32),
            pltpu.VMEM((heads * HEAD_ROWS, tm), F32),
            pltpu.VMEM((mix_w, tm), F32),
        ],
        compiler_params=pltpu.CompilerParams(
            dimension_semantics=("arbitrary", "arbitrary"), vmem_limit_bytes=VMEM_LIMIT),
        name="moba_mixer",
    )(x2d, w_in, kbd, vbd, wo, g1, b1)


def _router_kernel(x_ref, whi_ref, wlo_ref, br_ref, tri_ref, info_ref, sel_ref, cnt_ref, run_sc):
    step = pl.program_id(0)

    @pl.when(step == 0)
    def _():
        run_sc[...] = jnp.zeros(run_sc.shape, F32)

    x_hi, x_lo = _split_bf16(x_ref[...])
    logits = (_dot_nt(whi_ref[...], x_hi) + _dot_nt(wlo_ref[...], x_hi) + _dot_nt(whi_ref[...], x_lo)
              + br_ref[...])[0:ROUTER_ROWS, :]
    row = lax.broadcasted_iota(jnp.int32, logits.shape, 0).astype(F32)
    cmax = lambda a: jnp.max(a, axis=0, keepdims=True)
    cmin = lambda a: jnp.min(a, axis=0, keepdims=True)
    csum = lambda a: jnp.sum(a, axis=0, keepdims=True)
    big = float(2 * ROUTER_ROWS)

    is_g = row < N_GROUPS
    gl = jnp.where(is_g, logits, NEG_INF)
    gmax = cmax(gl)
    gidx = cmin(jnp.where(gl == gmax, row, big))
    g_w = 1.0 / csum(jnp.where(is_g, jnp.exp(gl - gmax), 0.0))

    lo = ROUTER_OFF + EXPERTS_PER_GROUP * gidx
    el = jnp.where((row >= lo) & (row < lo + EXPERTS_PER_GROUP), logits, NEG_INF)
    v0 = cmax(el)
    i0 = cmin(jnp.where(el == v0, row, big))
    el1 = jnp.where(row == i0, NEG_INF, el)
    v1 = cmax(el1)
    i1 = cmin(jnp.where(el1 == v1, row, big))
    t = jnp.exp(v1 - v0)
    w0 = g_w / (1.0 + t)
    w1 = g_w * t / (1.0 + t)

    pick0 = row == i0
    pick1 = row == i1
    onehot = jnp.where(pick0 | pick1, 1.0, 0.0)
    before = _dot(onehot.astype(BF16), tri_ref[...]) + run_sc[...]
    pos0 = csum(jnp.where(pick0, before, 0.0))
    pos1 = csum(jnp.where(pick1, before, 0.0))
    run_sc[...] = run_sc[...] + jnp.sum(onehot, axis=1, keepdims=True)
    cnt_ref[...] = run_sc[...]

    vals = (i0 - ROUTER_OFF, i1 - ROUTER_OFF, pos0, pos1, w0, w1)
    row8 = lax.broadcasted_iota(jnp.int32, sel_ref.shape, 0)
    sel = jnp.zeros(sel_ref.shape, F32)
    for c, val in enumerate(vals):
        sel = jnp.where(row8 == c, val, sel)
    sel_ref[...] = sel
    rowl = lax.broadcasted_iota(jnp.int32, (ROUTER_LANES, x_ref.shape[0]), 0)
    info_ref[...] = jnp.where(rowl == 4, w0, jnp.where(rowl == 5, w1, 0.0)).T


def _router(x2d, whi, wlo, br, tri):
    t, d = x2d.shape
    tm = TOKEN_TILE
    full = lambda shape: pl.BlockSpec(shape, lambda s: (0,) * len(shape))
    return pl.pallas_call(
        _router_kernel,
        out_shape=(jax.ShapeDtypeStruct((t, ROUTER_LANES), F32),
                   jax.ShapeDtypeStruct((t // tm, SUBLANES, tm), F32),
                   jax.ShapeDtypeStruct((ROUTER_ROWS, tm), F32)),
        grid=(t // tm,),
        in_specs=[pl.BlockSpec((tm, d), lambda s: (s, 0)),
                  full(whi.shape), full(wlo.shape), full(br.shape), full(tri.shape)],
        out_specs=(pl.BlockSpec((tm, ROUTER_LANES), lambda s: (s, 0)),
                   pl.BlockSpec((None, SUBLANES, tm), lambda s: (s, 0, 0)),
                   full((ROUTER_ROWS, tm))),
        scratch_shapes=[pltpu.VMEM((ROUTER_ROWS, tm), F32)],
        compiler_params=pltpu.CompilerParams(dimension_semantics=("arbitrary",)),
        name="router",
    )(x2d, whi, wlo, br, tri)


def _row_copy(src, src_row, dst, dst_row, sem):
    return pltpu.make_async_copy(src.at[pl.ds(src_row, 1), :], dst.at[pl.ds(dst_row, 1), :], sem)


def _slots_kernel(sel_ref, base_ref, o_ref):
    sel = sel_ref[...]
    g, _, tm = sel.shape
    row = lax.broadcasted_iota(jnp.int32, (g, ROUTER_ROWS, tm), 1).astype(F32)
    base = base_ref[...][None]
    csum = lambda a: jnp.sum(a, axis=1, keepdims=True)
    s0 = csum(jnp.where(row == sel[:, 0:1, :] + ROUTER_OFF, base, 0.0)) + sel[:, 2:3, :]
    s1 = csum(jnp.where(row == sel[:, 1:2, :] + ROUTER_OFF, base, 0.0)) + sel[:, 3:4, :]
    row8 = lax.broadcasted_iota(jnp.int32, o_ref.shape, 1)
    o_ref[...] = jnp.where(row8 == 0, s0, jnp.where(row8 == 1, s1, 0.0)).astype(jnp.int32)


def _slots(sel, base_col):
    nt, _, tm = sel.shape
    g = SLOT_TILES_PER_STEP if nt % SLOT_TILES_PER_STEP == 0 else nt
    return pl.pallas_call(
        _slots_kernel,
        out_shape=jax.ShapeDtypeStruct((nt, SUBLANES, tm), jnp.int32),
        grid=(nt // g,),
        in_specs=[pl.BlockSpec((g, SUBLANES, tm), lambda s: (s, 0, 0)),
                  pl.BlockSpec(base_col.shape, lambda s: (0, 0))],
        out_specs=pl.BlockSpec((g, SUBLANES, tm), lambda s: (s, 0, 0)),
        compiler_params=pltpu.CompilerParams(dimension_semantics=("arbitrary",)),
        name="slots",
    )(sel, base_col)


def _dispatch_kernel(nv_ref, s0_ref, s1_ref, x_ref, xs_ref, zero_sc, sem, fill_sem):
    tm = x_ref.shape[0]

    @pl.when(pl.program_id(0) == 0)
    def _():
        zero_sc[...] = jnp.zeros(zero_sc.shape, F32)

        def fill(t):
            return pltpu.make_async_copy(zero_sc, xs_ref.at[pl.ds(pl.multiple_of(t * tm, tm), tm), :], fill_sem)

        def start_fill(t, c):
            @pl.when(nv_ref[t] < tm)
            def _():
                fill(t).start()
            return c

        def wait_fill(t, c):
            @pl.when(nv_ref[t] < tm)
            def _():
                fill(t).wait()
            return c

        lax.fori_loop(0, nv_ref.shape[0], start_fill, 0)
        lax.fori_loop(0, nv_ref.shape[0], wait_fill, 0)

    def start(r, c):
        _row_copy(x_ref, r, xs_ref, s0_ref[0, r], sem).start(priority=0)
        _row_copy(x_ref, r, xs_ref, s1_ref[0, r], sem).start(priority=1)
        return c

    lax.fori_loop(0, tm, start, 0, unroll=DMA_UNROLL)
    for _ in range(2):
        pltpu.make_async_copy(x_ref, xs_ref.at[pl.ds(0, tm), :], sem).wait()


def _dispatch(x2d, slot0, slot1, tile_valid):
    t, d = x2d.shape
    tm = TOKEN_TILE
    n_slots = tile_valid.shape[0] * tm
    smem_rows = pl.BlockSpec((None, 1, tm), lambda s, nv: (s, 0, 0), memory_space=pltpu.SMEM)
    return pl.pallas_call(
        _dispatch_kernel,
        out_shape=jax.ShapeDtypeStruct((n_slots, d), F32),
        grid_spec=pltpu.PrefetchScalarGridSpec(
            num_scalar_prefetch=1,
            grid=(t // tm,),
            in_specs=[smem_rows, smem_rows, pl.BlockSpec((tm, d), lambda s, nv: (s, 0))],
            out_specs=pl.BlockSpec(memory_space=pl.ANY),
            scratch_shapes=[pltpu.VMEM((tm, d), F32), pltpu.SemaphoreType.DMA, pltpu.SemaphoreType.DMA],
        ),
        compiler_params=pltpu.CompilerParams(dimension_semantics=("arbitrary",)),
        name="dispatch",
    )(tile_valid, slot0, slot1, x2d)


def _expert_kernel(te_ref, nv_ref, first_ref, buf_ref, next_ref, xs_ref, wg_hbm, wu_hbm, wd_hbm, ys_ref,
                   wg_buf, wu_buf, wd_buf, wg_sc, wu_sc, wd_sc, sems, *, base):
    t = pl.program_id(0)
    used = nv_ref[t] > 0

    def weight_copies(e, b):
        return (pltpu.make_async_copy(wg_hbm.at[base + e], wg_buf.at[b], sems.at[b, 0]),
                pltpu.make_async_copy(wu_hbm.at[base + e], wu_buf.at[b], sems.at[b, 1]),
                pltpu.make_async_copy(wd_hbm.at[base + e], wd_buf.at[b], sems.at[b, 2]))

    @pl.when(t == 0)
    def _():
        for c in weight_copies(te_ref[0], 0):
            c.start()

    @pl.when(first_ref[t] == 1)
    def _():
        b = buf_ref[t]
        for c in weight_copies(te_ref[t], b):
            c.wait()
        wg_sc[...] = wg_buf[b].astype(BF16)
        wu_sc[...] = wu_buf[b].astype(BF16)
        wd_sc[...] = wd_buf[b].astype(BF16)

        @pl.when(next_ref[t] >= 0)
        def _():
            for c in weight_copies(next_ref[t], 1 - b):
                c.start()

    @pl.when(used)
    def _():
        xb = xs_ref[...].astype(BF16)
        hg = _dot(xb, wg_sc[...])
        hu = _dot(xb, wu_sc[...])
        h = hg * jax.nn.sigmoid(hg) * hu
        ys_ref[...] = _dot(h.astype(BF16), wd_sc[...])

    @pl.when(jnp.logical_not(used))
    def _():
        ys_ref[...] = jnp.zeros(ys_ref.shape, F32)


def _expert_mlp(xs, tile_expert, tile_valid, tile_first, tile_buf, tile_next, w_gate, w_up, w_down, layer):
    ns, d = xs.shape
    f = w_gate.shape[-1]
    tm = TOKEN_TILE
    hbm = pl.BlockSpec(memory_space=pl.ANY)
    return pl.pallas_call(
        functools.partial(_expert_kernel, base=layer * N_EXPERTS),
        out_shape=jax.ShapeDtypeStruct((ns, d), F32),
        grid_spec=pltpu.PrefetchScalarGridSpec(
            num_scalar_prefetch=5,
            grid=(ns // tm,),
            in_specs=[pl.BlockSpec((tm, d), lambda t, *_: (t, 0)), hbm, hbm, hbm],
            out_specs=pl.BlockSpec((tm, d), lambda t, *_: (t, 0)),
            scratch_shapes=[pltpu.VMEM((2, d, f), F32), pltpu.VMEM((2, d, f), F32), pltpu.VMEM((2, f, d), F32),
                            pltpu.VMEM((d, f), BF16), pltpu.VMEM((d, f), BF16), pltpu.VMEM((f, d), BF16),
                            pltpu.SemaphoreType.DMA((2, 3))],
        ),
        compiler_params=pltpu.CompilerParams(
            dimension_semantics=("arbitrary",), vmem_limit_bytes=VMEM_LIMIT),
        name="expert_mlp",
    )(tile_expert, tile_valid, tile_first, tile_buf, tile_next, xs, w_gate, w_up, w_down)


def _combine_kernel(s0_ref, s1_ref, info_ref, x_ref, ys_ref, g_ref, b_ref, o_ref, y0_sc, y1_sc, sem,
                    *, alpha):
    tm = x_ref.shape[0]

    def start(r, c):
        _row_copy(ys_ref, s0_ref[0, r], y0_sc, r, sem).start(priority=0)
        _row_copy(ys_ref, s1_ref[0, r], y1_sc, r, sem).start(priority=1)
        return c

    lax.fori_loop(0, tm, start, 0, unroll=DMA_UNROLL)
    for dst in (y0_sc, y1_sc):
        pltpu.make_async_copy(ys_ref.at[pl.ds(0, tm), :], dst, sem).wait()
    info = info_ref[...]
    f = info[:, 4:5] * y0_sc[...] + info[:, 5:6] * y1_sc[...]
    o_ref[...] = _ln(alpha * x_ref[...] + f, g_ref[...], b_ref[...])


def _combine(x2d, info, ys, slot0, slot1, g2, b2, alpha):
    t, d = x2d.shape
    tm = TOKEN_TILE
    smem_rows = pl.BlockSpec((None, 1, tm), lambda s: (s, 0, 0), memory_space=pltpu.SMEM)
    full = lambda shape: pl.BlockSpec(shape, lambda s: (0,) * len(shape))
    return pl.pallas_call(
        functools.partial(_combine_kernel, alpha=alpha),
        out_shape=jax.ShapeDtypeStruct((t, d), F32),
        grid=(t // tm,),
        in_specs=[smem_rows, smem_rows,
                  pl.BlockSpec((tm, ROUTER_LANES), lambda s: (s, 0)),
                  pl.BlockSpec((tm, d), lambda s: (s, 0)),
                  pl.BlockSpec(memory_space=pl.ANY),
                  full(g2.shape), full(b2.shape)],
        out_specs=pl.BlockSpec((tm, d), lambda s: (s, 0)),
        scratch_shapes=[pltpu.VMEM((tm, d), F32), pltpu.VMEM((tm, d), F32), pltpu.SemaphoreType.DMA],
        compiler_params=pltpu.CompilerParams(dimension_semantics=("arbitrary",)),
        name="combine",
    )(slot0, slot1, info, x2d, ys, g2, b2)


def _moe(x2d, layer, w_rg, b_rg, w_re, b_re, w_gate, w_up, w_down, g2, b2, alpha):
    t, d = x2d.shape
    tm = TOKEN_TILE
    wr = jnp.zeros((ROUTER_LANES, d), F32).at[0:N_GROUPS].set(w_rg.T)
    wr = wr.at[ROUTER_OFF:ROUTER_OFF + N_EXPERTS].set(w_re.T)
    br = jnp.zeros((ROUTER_LANES, 1), F32).at[0:N_GROUPS, 0].set(b_rg)
    br = br.at[ROUTER_OFF:ROUTER_OFF + N_EXPERTS, 0].set(b_re)
    whi, wlo = _split_bf16(wr)
    ridx = lax.broadcasted_iota(jnp.int32, (tm, tm), 0)
    cidx = lax.broadcasted_iota(jnp.int32, (tm, tm), 1)
    tri = (ridx < cidx).astype(BF16)
    info, sel, cnt = _router(x2d, whi, wlo, br, tri)

    counts = cnt[ROUTER_OFF:ROUTER_OFF + N_EXPERTS, 0].astype(jnp.int32)
    padded = (counts + tm - 1) // tm * tm
    ends = jnp.cumsum(padded)
    base = ends - padded
    base_col = jnp.zeros((ROUTER_ROWS, 1), F32).at[ROUTER_OFF:ROUTER_OFF + N_EXPERTS, 0].set(base.astype(F32))
    slots = _slots(sel, base_col)
    slot0, slot1 = slots[:, 0:1, :], slots[:, 1:2, :]

    n_slots = 2 * t + N_EXPERTS * tm
    n_tiles = n_slots // tm
    n_used = ends[-1] // tm
    tile_ids = jnp.arange(n_tiles, dtype=jnp.int32)
    tile_expert = jnp.sum((jnp.minimum(tile_ids, n_used - 1)[:, None] * tm >= ends[None, :]).astype(jnp.int32),
                          axis=1)
    lo = jnp.maximum(base[None, :], tile_ids[:, None] * tm)
    hi = jnp.minimum((base + counts)[None, :], (tile_ids[:, None] + 1) * tm)
    tile_valid = jnp.sum(jnp.maximum(hi - lo, 0), axis=1).astype(jnp.int32)

    used = tile_valid > 0
    tile_first = (used & ((tile_ids == 0) | (tile_expert != jnp.roll(tile_expert, 1)))).astype(jnp.int32)
    tile_buf = ((jnp.cumsum(tile_first) - 1) % 2).astype(jnp.int32)
    experts = jnp.arange(N_EXPERTS, dtype=jnp.int32)
    later_nonempty = (experts[None, :] > experts[:, None]) & (counts[None, :] > 0)
    next_expert = jnp.min(jnp.where(later_nonempty, experts[None, :], N_EXPERTS), axis=1)
    next_expert = jnp.where(next_expert < N_EXPERTS, next_expert, -1)
    tile_next = jnp.sum(jnp.where(tile_expert[:, None] == experts[None, :], next_expert[None, :], 0),
                        axis=1).astype(jnp.int32)

    xs = _dispatch(x2d, slot0, slot1, tile_valid)
    ys = _expert_mlp(xs, tile_expert, tile_valid, tile_first, tile_buf, tile_next, w_gate, w_up, w_down, layer)
    return _combine(x2d, info, ys, slot0, slot1, g2, b2, alpha)


def kernel(x, mem, w_mem_kv, conv_w_in, conv_dw_w, conv_dw_b, conv_ln_g, conv_ln_b, moba_w_in, w_o,
           ln1_g, ln1_b, w_rg, b_rg, w_re, b_re, w_gate, w_up, w_down, ln2_g, ln2_b):
    batch, seq, d = x.shape
    depth = w_o.shape[0]
    alpha = (2 * depth) ** 0.25
    t = batch * seq
    row = lambda a: a.reshape(1, -1)

    kbd, vbd = _memkv(mem, w_mem_kv)

    e_shape = w_gate.shape
    w_gate = w_gate.reshape((-1,) + e_shape[-2:])
    w_up = w_up.reshape((-1,) + e_shape[-2:])
    w_down = w_down.reshape((-1,) + w_down.shape[-2:])

    x2d = x.reshape(t, d)
    for i in range(depth):
        j = i // 2
        wo = w_o[i].astype(BF16)
        if i % 2 == 0:
            dw_w = jnp.broadcast_to(conv_dw_w[j][:, None, :], (CONV_WIDTH, SUBLANES, conv_dw_w.shape[-1]))
            w_in = conv_w_in[j].astype(BF16).reshape(d, -1, IN_PROJ_CHUNK).transpose(1, 0, 2)
            x2d = _conv_mixer(x2d, batch, w_in, dw_w, row(conv_dw_b[j]),
                              row(conv_ln_g[j]), row(conv_ln_b[j]), kbd, vbd, wo,
                              row(ln1_g[i]), row(ln1_b[i]), alpha)
        else:
            x2d = _moba_mixer(x2d, batch, moba_w_in[j].astype(BF16), kbd, vbd, wo,
                              row(ln1_g[i]), row(ln1_b[i]), alpha)
        x2d = _moe(x2d, i, w_rg[i], b_rg[i], w_re[i], b_re[i], w_gate, w_up, w_down,
                   row(ln2_g[i]), row(ln2_b[i]), alpha)
    return x2d.reshape(batch, seq, d)
```

```python
import functools

import jax
import jax.numpy as jnp
from jax import lax
from jax.experimental import pallas as pl
from jax.experimental.pallas import tpu as pltpu

F32 = jnp.float32
BF16 = jnp.bfloat16

HEAD_DIM = 64
MEM_LEN = 256
MEM_HEADS = 4
MEM_WIDTH = MEM_HEADS * HEAD_DIM
CONV_WIDTH = 31
MOBA_BLOCK = 256
MOBA_TOPK = 3
HEAD_ROWS = HEAD_DIM + 16
MOBA_LOOKAHEAD = 6
N_GROUPS = 4
EXPERTS_PER_GROUP = 8
N_EXPERTS = N_GROUPS * EXPERTS_PER_GROUP
LN_EPS = 1e-5

LANES = 128
SUBLANES = 8
TOKEN_TILE = 256
CONV_HALO = 32
CONV_CHUNK = 16
CONV_CHUNKS_PER_STEP = 2
IN_PROJ_CHUNK = 256
ROUTER_LANES = 128
ROUTER_ROWS = 40
ROUTER_OFF = N_GROUPS
DMA_UNROLL = 8
ROW_TILE = 512
SLOT_TILES_PER_STEP = 16
VMEM_LIMIT = 56 * 1024 * 1024

NEG_INF = float("-inf")
LOG2_E = 1.4426950408889634


def _ln(z, g, b):
    mu = jnp.mean(z, axis=-1, keepdims=True)
    zc = z - mu
    var = jnp.mean(zc * zc, axis=-1, keepdims=True)
    return zc * lax.rsqrt(var + LN_EPS) * g + b


def _dot(a, b):
    return jnp.dot(a, b, preferred_element_type=F32)


def _dot_nt(a, b):
    return lax.dot_general(a, b, (((1,), (1,)), ((), ())), preferred_element_type=F32)


def _split_bf16(x):
    hi = x.astype(BF16)
    lo = (x - hi.astype(F32)).astype(BF16)
    return hi, lo


def _memkv_kernel(mem_ref, w_ref, kbd_ref, vbd_ref):
    kv = _dot(mem_ref[...].astype(BF16), w_ref[...].astype(BF16))
    k_t = kv[:, 0:MEM_WIDTH].T
    v = kv[:, MEM_WIDTH:]
    r = lax.broadcasted_iota(jnp.int32, kbd_ref.shape, 0)
    c = lax.broadcasted_iota(jnp.int32, kbd_ref.shape, 1)
    kbd_ref[...] = jnp.where(r // HEAD_DIM == c // MEM_LEN,
                             jnp.concatenate([k_t] * MEM_HEADS, axis=1), 0.0).astype(BF16)
    r = lax.broadcasted_iota(jnp.int32, vbd_ref.shape, 0)
    c = lax.broadcasted_iota(jnp.int32, vbd_ref.shape, 1)
    vbd_ref[...] = jnp.where(r // MEM_LEN == c // HEAD_DIM,
                             jnp.concatenate([v] * MEM_HEADS, axis=0), 0.0).astype(BF16)


def _memkv(mem, w):
    batch, m, d = mem.shape
    return pl.pallas_call(
        _memkv_kernel,
        out_shape=(jax.ShapeDtypeStruct((batch, MEM_WIDTH, MEM_HEADS * m), BF16),
                   jax.ShapeDtypeStruct((batch, MEM_HEADS * m, MEM_WIDTH), BF16)),
        grid=(batch,),
        in_specs=[pl.BlockSpec((None, m, d), lambda b: (b, 0, 0)),
                  pl.BlockSpec(w.shape, lambda b: (0, 0))],
        out_specs=(pl.BlockSpec((None, MEM_WIDTH, MEM_HEADS * m), lambda b: (b, 0, 0)),
                   pl.BlockSpec((None, MEM_HEADS * m, MEM_WIDTH), lambda b: (b, 0, 0))),
        compiler_params=pltpu.CompilerParams(dimension_semantics=("arbitrary",)),
        name="memkv",
    )(mem, w)


def _mem_attention(qm, kbd, vbd):
    s = _dot(qm.astype(BF16), kbd) * (HEAD_DIM ** -0.5)
    parts = []
    for h in range(MEM_HEADS):
        seg = s[:, h * MEM_LEN:(h + 1) * MEM_LEN]
        m = jnp.max(seg, axis=-1, keepdims=True)
        e = jnp.exp(seg - m)
        parts.append(e / jnp.sum(e, axis=-1, keepdims=True))
    p = jnp.concatenate(parts, axis=-1)
    return _dot(p.astype(BF16), vbd)


def _out_proj_ln(x, y_mix, y_mem, wo_ref, g_ref, b_ref, alpha, mix_w):
    y = _dot(y_mix.astype(BF16), wo_ref[0:mix_w, :]) + _dot(y_mem.astype(BF16), wo_ref[mix_w:, :])
    return _ln(alpha * x + y, g_ref[...], b_ref[...])


def _conv_mixer_kernel(x_ref, xn_ref, win_ref, dww_ref, dwb_ref, cg_ref, cb_ref, kbd_ref, vbd_ref,
                       wo_ref, g1_ref, b1_ref, o_ref, ustage, xn_sc, hbuf, zbuf, cbuf, *, alpha, mix_w):
    tm = x_ref.shape[0]
    n_cc = win_ref.shape[0]
    per = mix_w // IN_PROJ_CHUNK
    j = pl.program_id(1)

    @pl.when((pl.program_id(0) == 0) & (j == 0))
    def _():
        xb = x_ref[...].astype(BF16)
        for c in range(n_cc):
            ustage[c] = _dot(xb, win_ref[c])

    x = x_ref[...]
    qm = ustage[2 * per]
    h = jnp.concatenate([ustage[c] * jax.nn.sigmoid(ustage[per + c]) for c in range(per)], axis=1)
    xn_sc[...] = xn_ref[...].astype(BF16)

    @pl.when(j == 0)
    def _():
        hbuf[0:CONV_HALO, :] = jnp.zeros((CONV_HALO, mix_w), F32)

    hbuf[CONV_HALO:CONV_HALO + tm, :] = h

    shifted_rows = CONV_HALO + tm - SUBLANES
    for a in range(1, SUBLANES):
        zbuf[a - 1, 0:shifted_rows, :] = hbuf[a:a + shifted_rows, :]

    first = CONV_HALO - (CONV_WIDTH - 1)
    aligned = lambda v, m: v if isinstance(v, int) else pl.multiple_of(v, m)

    def conv_rows(r0):
        acc = jnp.broadcast_to(dwb_ref[...], (CONV_CHUNK, mix_w))
        for a in range(SUBLANES):
            offs = [first + k - a for k in range(CONV_WIDTH) if (first + k) % SUBLANES == a]
            r = aligned(r0 + offs[0], SUBLANES)
            span = offs[-1] - offs[0] + CONV_CHUNK
            z = hbuf[pl.ds(r, span), :] if a == 0 else zbuf[a - 1, pl.ds(r, span), :]
            for off in offs:
                k = off + a - first
                rows = z[off - offs[0]:off - offs[0] + CONV_CHUNK, :]
                acc = acc + jnp.tile(dww_ref[k], (CONV_CHUNK // SUBLANES, 1)) * rows
        cbuf[pl.ds(r0, CONV_CHUNK), :] = acc

    def project_next(c):
        ustage[c] = _dot(xn_sc[...], win_ref[c])

    step_rows = CONV_CHUNK * CONV_CHUNKS_PER_STEP
    paired = min(n_cc, tm // step_rows)

    def paired_step(c, carry):
        project_next(c)
        for q in range(CONV_CHUNKS_PER_STEP):
            conv_rows(pl.multiple_of(c * step_rows + q * CONV_CHUNK, CONV_CHUNK))
        return carry

    lax.fori_loop(0, paired, paired_step, 0)
    for r0 in range(paired * step_rows, tm, CONV_CHUNK):
        conv_rows(r0)
    for c in range(paired, n_cc):
        project_next(c)

    hbuf[0:CONV_HALO, :] = hbuf[tm:tm + CONV_HALO, :]

    cn = _ln(cbuf[...], cg_ref[...], cb_ref[...])
    y_mix = cn * jax.nn.sigmoid(cn)
    y_mem = _mem_attention(qm, kbd_ref[...], vbd_ref[...])
    o_ref[...] = _out_proj_ln(x, y_mix, y_mem, wo_ref, g1_ref, b1_ref, alpha, mix_w)


def _conv_mixer(x2d, batch, w_in, dw_w, dw_b, cg, cb, kbd, vbd, wo, g1, b1, alpha):
    t, d = x2d.shape
    tm = TOKEN_TILE
    nj = t // batch // tm
    mix_w = d - MEM_WIDTH
    full = lambda shape: pl.BlockSpec(shape, lambda b, j: (0,) * len(shape))
    return pl.pallas_call(
        functools.partial(_conv_mixer_kernel, alpha=alpha, mix_w=mix_w),
        out_shape=jax.ShapeDtypeStruct((t, d), F32),
        grid=(batch, nj),
        in_specs=[
            pl.BlockSpec((tm, d), lambda b, j: (b * nj + j, 0)),
            pl.BlockSpec((tm, d), lambda b, j: (jnp.minimum(b * nj + j + 1, batch * nj - 1), 0)),
            full(w_in.shape), full(dw_w.shape), full(dw_b.shape), full(cg.shape), full(cb.shape),
            pl.BlockSpec((None,) + kbd.shape[1:], lambda b, j: (b, 0, 0)),
            pl.BlockSpec((None,) + vbd.shape[1:], lambda b, j: (b, 0, 0)),
            full(wo.shape), full(g1.shape), full(b1.shape),
        ],
        out_specs=pl.BlockSpec((tm, d), lambda b, j: (b * nj + j, 0)),
        scratch_shapes=[pltpu.VMEM((w_in.shape[0], tm, IN_PROJ_CHUNK), F32),
                        pltpu.VMEM((tm, d), BF16),
                        pltpu.VMEM((CONV_HALO + tm, mix_w), F32),
                        pltpu.VMEM((SUBLANES - 1, CONV_HALO + tm, mix_w), F32),
                        pltpu.VMEM((tm, mix_w), F32)],
        compiler_params=pltpu.CompilerParams(
            dimension_semantics=("arbitrary", "arbitrary"), vmem_limit_bytes=VMEM_LIMIT),
        name="conv_mixer",
    )(x2d, x2d, w_in, dw_w, dw_b, cg, cb, kbd, vbd, wo, g1, b1)


def _moba_select_bias(gate, i):
    nb = gate.shape[1]
    blk = lax.broadcasted_iota(jnp.int32, gate.shape, 1).astype(F32)
    past = blk < i.astype(F32)
    gm = jnp.where(past, gate, NEG_INF)
    bias = jnp.full(gate.shape, NEG_INF, F32)
    for _ in range(MOBA_TOPK):
        top = jnp.max(gm, axis=1, keepdims=True)
        first = jnp.min(jnp.where(gm == top, blk, float(nb)), axis=1, keepdims=True)
        taken = blk == first
        bias = jnp.where(taken & past, 0.0, bias)
        gm = jnp.where(taken, NEG_INF, gm)
    return bias


def _moba_mixer_kernel(x_ref, win_ref, kbd_ref, vbd_ref, wo_ref, g1_ref, b1_ref, o_ref,
                       k_sc, vt_sc, kmt_sc, bias_sc, qh_sc, m_sc, acc_sc, yt_sc, *, alpha, mix_w, nb):
    tm = x_ref.shape[0]
    heads = mix_w // HEAD_DIM
    i = pl.program_id(1)
    x = x_ref[...]
    u = _dot(x.astype(BF16), win_ref[...])
    q = u[:, 0:mix_w] * (HEAD_DIM ** -0.5 * LOG2_E)
    k = u[:, mix_w:2 * mix_w]
    v = u[:, 2 * mix_w:3 * mix_w]
    qm = u[:, 3 * mix_w:]

    @pl.when(i == 0)
    def _():
        kmt_sc[...] = jnp.zeros(kmt_sc.shape, F32)

    k_sc[i] = k.astype(BF16)
    v_t = v.T.astype(BF16)
    for h in range(heads):
        vt_sc[i, h * HEAD_ROWS:h * HEAD_ROWS + HEAD_DIM, :] = v_t[h * HEAD_DIM:(h + 1) * HEAD_DIM, :]
        vt_sc[i, h * HEAD_ROWS + HEAD_DIM:(h + 1) * HEAD_ROWS, :] = jnp.ones((HEAD_ROWS - HEAD_DIM, tm), BF16)
    kmean = jnp.mean(k, axis=0, keepdims=True)
    lane = lax.broadcasted_iota(jnp.int32, (1, mix_w), 1)
    for h in range(heads):
        in_head = (lane >= h * HEAD_DIM) & (lane < (h + 1) * HEAD_DIM)
        kmt_sc[pl.ds(h * nb + i, 1), :] = jnp.where(in_head, kmean, 0.0)

    q_hi, q_lo = _split_bf16(q)
    km_hi, km_lo = _split_bf16(kmt_sc[...])
    gate_t = _dot_nt(km_hi, q_hi) + _dot_nt(km_hi, q_lo) + _dot_nt(km_lo, q_hi)
    bias_sc[...] = _moba_select_bias(gate_t.reshape(heads, nb, tm), i)

    lane_p = lax.broadcasted_iota(jnp.int32, (tm, 2 * HEAD_DIM), 1)
    for h in range(heads):
        qp = q_hi[:, (h // 2) * 2 * HEAD_DIM:(h // 2 + 1) * 2 * HEAD_DIM]
        keep = (lane_p < HEAD_DIM) if h % 2 == 0 else (lane_p >= HEAD_DIM)
        qh_sc[h] = jnp.where(keep, qp, jnp.zeros_like(qp))

    def scores(j, h):
        cols = slice((h // 2) * 2 * HEAD_DIM, (h // 2 + 1) * 2 * HEAD_DIM)
        return _dot_nt(k_sc[j, :, cols], qh_sc[h])

    rows = lambda h: slice(h * HEAD_ROWS, (h + 1) * HEAD_ROWS)

    kidx = lax.broadcasted_iota(jnp.int32, (tm, tm), 0)
    qidx = lax.broadcasted_iota(jnp.int32, (tm, tm), 1)
    causal = kidx <= qidx
    def heads_pipelined(j):
        pending = [scores(j, h) for h in range(MOBA_LOOKAHEAD)]
        for h in range(heads):
            if h + MOBA_LOOKAHEAD < heads:
                pending.append(scores(j, h + MOBA_LOOKAHEAD))
            yield h, pending.pop(0)

    def own_block(j, c):
        for h, s in heads_pipelined(j):
            s = jnp.where(causal, s, NEG_INF)
            m = jnp.max(s, axis=0, keepdims=True)
            e = jnp.exp2(s - m)
            m_sc[h] = m
            acc_sc[rows(h), :] = _dot(vt_sc[j, rows(h), :], e.astype(BF16))
        return c

    lax.fori_loop(i, i + 1, own_block, 0)

    def body(j, c):
        for h, s in heads_pipelined(j):
            b = bias_sc[h, pl.ds(j, 1), :]
            m_old = m_sc[h]
            m_new = jnp.maximum(m_old, jnp.max(s, axis=0, keepdims=True) + b)
            e = jnp.exp2(s - (m_new - b))
            corr = jnp.exp2(m_old - m_new)
            m_sc[h] = m_new
            acc_sc[rows(h), :] = corr * acc_sc[rows(h), :] + _dot(vt_sc[j, rows(h), :], e.astype(BF16))
        return c

    lax.fori_loop(0, i, body, 0)
    for h in range(heads):
        r0 = h * HEAD_ROWS
        yt_sc[h * HEAD_DIM:(h + 1) * HEAD_DIM, :] = (acc_sc[r0:r0 + HEAD_DIM, :]
                                                     / acc_sc[r0 + HEAD_DIM:r0 + HEAD_DIM + 1, :])

    y_mix = yt_sc[...].T
    y_mem = _mem_attention(qm, kbd_ref[...], vbd_ref[...])
    o_ref[...] = _out_proj_ln(x, y_mix, y_mem, wo_ref, g1_ref, b1_ref, alpha, mix_w)


def _moba_mixer(x2d, batch, w_in, kbd, vbd, wo, g1, b1, alpha):
    t, d = x2d.shape
    tm = MOBA_BLOCK
    nb = t // batch // tm
    mix_w = d - MEM_WIDTH
    heads = mix_w // HEAD_DIM
    full = lambda shape: pl.BlockSpec(shape, lambda b, j: (0,) * len(shape))
    return pl.pallas_call(
        functools.partial(_moba_mixer_kernel, alpha=alpha, mix_w=mix_w, nb=nb),
        out_shape=jax.ShapeDtypeStruct((t, d), F32),
        grid=(batch, nb),
        in_specs=[
            pl.BlockSpec((tm, d), lambda b, j: (b * nb + j, 0)),
            full(w_in.shape),
            pl.BlockSpec((None,) + kbd.shape[1:], lambda b, j: (b, 0, 0)),
            pl.BlockSpec((None,) + vbd.shape[1:], lambda b, j: (b, 0, 0)),
            full(wo.shape), full(g1.shape), full(b1.shape),
        ],
        out_specs=pl.BlockSpec((tm, d), lambda b, j: (b * nb + j, 0)),
        scratch_shapes=[
            pltpu.VMEM((nb, tm, mix_w), BF16),
            pltpu.VMEM((nb, heads * HEAD_ROWS, tm), BF16),
            pltpu.VMEM((heads * nb, mix_w), F32),
            pltpu.VMEM((heads, nb, tm), F32),
            pltpu.VMEM((heads, tm, 2 * HEAD_DIM), BF16),
            pltpu.VMEM((heads, 1, tm), F32),
            pltpu.VMEM((heads * HEAD_ROWS, tm), F32),
            pltpu.VMEM((mix_w, tm), F32),
        ],
        compiler_params=pltpu.CompilerParams(
            dimension_semantics=("arbitrary", "arbitrary"), vmem_limit_bytes=VMEM_LIMIT),
        name="moba_mixer",
    )(x2d, w_in, kbd, vbd, wo, g1, b1)


def _router_kernel(x_ref, whi_ref, wlo_ref, br_ref, tri_ref, info_ref, sel_ref, cnt_ref, run_sc):
    step = pl.program_id(0)

    @pl.when(step == 0)
    def _():
        run_sc[...] = jnp.zeros(run_sc.shape, F32)

    x_hi, x_lo = _split_bf16(x_ref[...])
    logits = (_dot_nt(whi_ref[...], x_hi) + _dot_nt(wlo_ref[...], x_hi) + _dot_nt(whi_ref[...], x_lo)
              + br_ref[...])[0:ROUTER_ROWS, :]
    row = lax.broadcasted_iota(jnp.int32, logits.shape, 0).astype(F32)
    cmax = lambda a: jnp.max(a, axis=0, keepdims=True)
    cmin = lambda a: jnp.min(a, axis=0, keepdims=True)
    csum = lambda a: jnp.sum(a, axis=0, keepdims=True)
    big = float(2 * ROUTER_ROWS)

    is_g = row < N_GROUPS
    gl = jnp.where(is_g, logits, NEG_INF)
    gmax = cmax(gl)
    gidx = cmin(jnp.where(gl == gmax, row, big))
    g_w = 1.0 / csum(jnp.where(is_g, jnp.exp(gl - gmax), 0.0))

    lo = ROUTER_OFF + EXPERTS_PER_GROUP * gidx
    el = jnp.where((row >= lo) & (row < lo + EXPERTS_PER_GROUP), logits, NEG_INF)
    v0 = cmax(el)
    i0 = cmin(jnp.where(el == v0, row, big))
    el1 = jnp.where(row == i0, NEG_INF, el)
    v1 = cmax(el1)
    i1 = cmin(jnp.where(el1 == v1, row, big))
    t = jnp.exp(v1 - v0)
    w0 = g_w / (1.0 + t)
    w1 = g_w * t / (1.0 + t)

    pick0 = row == i0
    pick1 = row == i1
    onehot = jnp.where(pick0 | pick1, 1.0, 0.0)
    before = _dot(onehot.astype(BF16), tri_ref[...]) + run_sc[...]
    pos0 = csum(jnp.where(pick0, before, 0.0))
    pos1 = csum(jnp.where(pick1, before, 0.0))
    run_sc[...] = run_sc[...] + jnp.sum(onehot, axis=1, keepdims=True)
    cnt_ref[...] = run_sc[...]

    vals = (i0 - ROUTER_OFF, i1 - ROUTER_OFF, pos0, pos1, w0, w1)
    row8 = lax.broadcasted_iota(jnp.int32, sel_ref.shape, 0)
    sel = jnp.zeros(sel_ref.shape, F32)
    for c, val in enumerate(vals):
        sel = jnp.where(row8 == c, val, sel)
    sel_ref[...] = sel
    rowl = lax.broadcasted_iota(jnp.int32, (ROUTER_LANES, x_ref.shape[0]), 0)
    info_ref[...] = jnp.where(rowl == 4, w0, jnp.where(rowl == 5, w1, 0.0)).T


def _router(x2d, whi, wlo, br, tri):
    t, d = x2d.shape
    tm = TOKEN_TILE
    full = lambda shape: pl.BlockSpec(shape, lambda s: (0,) * len(shape))
    return pl.pallas_call(
        _router_kernel,
        out_shape=(jax.ShapeDtypeStruct((t, ROUTER_LANES), F32),
                   jax.ShapeDtypeStruct((t // tm, SUBLANES, tm), F32),
                   jax.ShapeDtypeStruct((ROUTER_ROWS, tm), F32)),
        grid=(t // tm,),
        in_specs=[pl.BlockSpec((tm, d), lambda s: (s, 0)),
                  full(whi.shape), full(wlo.shape), full(br.shape), full(tri.shape)],
        out_specs=(pl.BlockSpec((tm, ROUTER_LANES), lambda s: (s, 0)),
                   pl.BlockSpec((None, SUBLANES, tm), lambda s: (s, 0, 0)),
                   full((ROUTER_ROWS, tm))),
        scratch_shapes=[pltpu.VMEM((ROUTER_ROWS, tm), F32)],
        compiler_params=pltpu.CompilerParams(dimension_semantics=("arbitrary",)),
        name="router",
    )(x2d, whi, wlo, br, tri)


def _row_copy(src, src_row, dst, dst_row, sem):
    return pltpu.make_async_copy(src.at[pl.ds(src_row, 1), :], dst.at[pl.ds(dst_row, 1), :], sem)


def _slots_kernel(sel_ref, base_ref, o_ref):
    sel = sel_ref[...]
    g, _, tm = sel.shape
    row = lax.broadcasted_iota(jnp.int32, (g, ROUTER_ROWS, tm), 1).astype(F32)
    base = base_ref[...][None]
    csum = lambda a: jnp.sum(a, axis=1, keepdims=True)
    s0 = csum(jnp.where(row == sel[:, 0:1, :] + ROUTER_OFF, base, 0.0)) + sel[:, 2:3, :]
    s1 = csum(jnp.where(row == sel[:, 1:2, :] + ROUTER_OFF, base, 0.0)) + sel[:, 3:4, :]
    row8 = lax.broadcasted_iota(jnp.int32, o_ref.shape, 1)
    o_ref[...] = jnp.where(row8 == 0, s0, jnp.where(row8 == 1, s1, 0.0)).astype(jnp.int32)


def _slots(sel, base_col):
    nt, _, tm = sel.shape
    g = SLOT_TILES_PER_STEP if nt % SLOT_TILES_PER_STEP == 0 else nt
    return pl.pallas_call(
        _slots_kernel,
        out_shape=jax.ShapeDtypeStruct((nt, SUBLANES, tm), jnp.int32),
        grid=(nt // g,),
        in_specs=[pl.BlockSpec((g, SUBLANES, tm), lambda s: (s, 0, 0)),
                  pl.BlockSpec(base_col.shape, lambda s: (0, 0))],
        out_specs=pl.BlockSpec((g, SUBLANES, tm), lambda s: (s, 0, 0)),
        compiler_params=pltpu.CompilerParams(dimension_semantics=("arbitrary",)),
        name="slots",
    )(sel, base_col)


def _dispatch_kernel(nv_ref, s0_ref, s1_ref, x_ref, xs_ref, zero_sc, sem, fill_sem, *, tm):

    @pl.when(pl.program_id(0) == 0)
    def _():
        zero_sc[...] = jnp.zeros(zero_sc.shape, F32)

        def fill(t):
            return pltpu.make_async_copy(zero_sc, xs_ref.at[pl.ds(pl.multiple_of(t * tm, tm), tm), :], fill_sem)

        def start_fill(t, c):
            @pl.when(nv_ref[t] < tm)
            def _():
                fill(t).start()
            return c

        def wait_fill(t, c):
            @pl.when(nv_ref[t] < tm)
            def _():
                fill(t).wait()
            return c

        lax.fori_loop(0, nv_ref.shape[0], start_fill, 0)
        lax.fori_loop(0, nv_ref.shape[0], wait_fill, 0)

    rows = x_ref.shape[0]
    for g in range(s0_ref.shape[0]):
        def start(r, c, g=g):
            row = g * s0_ref.shape[2] + r
            _row_copy(x_ref, row, xs_ref, s0_ref[g, 0, r], sem).start(priority=0)
            _row_copy(x_ref, row, xs_ref, s1_ref[g, 0, r], sem).start(priority=1)
            return c

        lax.fori_loop(0, s0_ref.shape[2], start, 0, unroll=DMA_UNROLL)
    for _ in range(2):
        pltpu.make_async_copy(x_ref, xs_ref.at[pl.ds(0, rows), :], sem).wait()


def _dispatch(x2d, slot0, slot1, tile_valid):
    t, d = x2d.shape
    tm = TOKEN_TILE
    rows = ROW_TILE if t % ROW_TILE == 0 else tm
    n_slots = tile_valid.shape[0] * tm
    smem_rows = pl.BlockSpec((rows // tm, 1, tm), lambda s, nv: (s, 0, 0), memory_space=pltpu.SMEM)
    return pl.pallas_call(
        functools.partial(_dispatch_kernel, tm=tm),
        out_shape=jax.ShapeDtypeStruct((n_slots, d), F32),
        grid_spec=pltpu.PrefetchScalarGridSpec(
            num_scalar_prefetch=1,
            grid=(t // rows,),
            in_specs=[smem_rows, smem_rows, pl.BlockSpec((rows, d), lambda s, nv: (s, 0))],
            out_specs=pl.BlockSpec(memory_space=pl.ANY),
            scratch_shapes=[pltpu.VMEM((tm, d), F32), pltpu.SemaphoreType.DMA, pltpu.SemaphoreType.DMA],
        ),
        compiler_params=pltpu.CompilerParams(dimension_semantics=("arbitrary",)),
        name="dispatch",
    )(tile_valid, slot0, slot1, x2d)


def _expert_kernel(te_ref, nv_ref, first_ref, buf_ref, next_ref, xs_ref, wg_hbm, wu_hbm, wd_hbm, ys_ref,
                   wg_buf, wu_buf, wd_buf, wg_sc, wu_sc, wd_sc, sems, *, base):
    t = pl.program_id(0)
    used = nv_ref[t] > 0

    def weight_copies(e, b):
        return (pltpu.make_async_copy(wg_hbm.at[base + e], wg_buf.at[b], sems.at[b, 0]),
                pltpu.make_async_copy(wu_hbm.at[base + e], wu_buf.at[b], sems.at[b, 1]),
                pltpu.make_async_copy(wd_hbm.at[base + e], wd_buf.at[b], sems.at[b, 2]))

    @pl.when(t == 0)
    def _():
        for c in weight_copies(te_ref[0], 0):
            c.start()

    @pl.when(first_ref[t] == 1)
    def _():
        b = buf_ref[t]
        for c in weight_copies(te_ref[t], b):
            c.wait()
        wg_sc[...] = wg_buf[b].astype(BF16)
        wu_sc[...] = wu_buf[b].astype(BF16)
        wd_sc[...] = wd_buf[b].astype(BF16)

        @pl.when(next_ref[t] >= 0)
        def _():
            for c in weight_copies(next_ref[t], 1 - b):
                c.start()

    @pl.when(used)
    def _():
        xb = xs_ref[...].astype(BF16)
        hg = _dot(xb, wg_sc[...])
        hu = _dot(xb, wu_sc[...])
        h = hg * jax.nn.sigmoid(hg) * hu
        ys_ref[...] = _dot(h.astype(BF16), wd_sc[...])

    @pl.when(jnp.logical_not(used))
    def _():
        ys_ref[...] = jnp.zeros(ys_ref.shape, F32)


def _expert_mlp(xs, tile_expert, tile_valid, tile_first, tile_buf, tile_next, w_gate, w_up, w_down, layer):
    ns, d = xs.shape
    f = w_gate.shape[-1]
    tm = TOKEN_TILE
    hbm = pl.BlockSpec(memory_space=pl.ANY)
    return pl.pallas_call(
        functools.partial(_expert_kernel, base=layer * N_EXPERTS),
        out_shape=jax.ShapeDtypeStruct((ns, d), F32),
        grid_spec=pltpu.PrefetchScalarGridSpec(
            num_scalar_prefetch=5,
            grid=(ns // tm,),
            in_specs=[pl.BlockSpec((tm, d), lambda t, *_: (t, 0)), hbm, hbm, hbm],
            out_specs=pl.BlockSpec((tm, d), lambda t, *_: (t, 0)),
            scratch_shapes=[pltpu.VMEM((2, d, f), F32), pltpu.VMEM((2, d, f), F32), pltpu.VMEM((2, f, d), F32),
                            pltpu.VMEM((d, f), BF16), pltpu.VMEM((d, f), BF16), pltpu.VMEM((f, d), BF16),
                            pltpu.SemaphoreType.DMA((2, 3))],
        ),
        compiler_params=pltpu.CompilerParams(
            dimension_semantics=("arbitrary",), vmem_limit_bytes=VMEM_LIMIT),
        name="expert_mlp",
    )(tile_expert, tile_valid, tile_first, tile_buf, tile_next, xs, w_gate, w_up, w_down)


def _combine_kernel(s0_ref, s1_ref, n0_ref, n1_ref, info_ref, x_ref, ys_ref, g_ref, b_ref, o_ref,
                    y0_sc, y1_sc, sems, *, alpha):
    s = pl.program_id(0)
    rows = x_ref.shape[0]

    def start_gathers(t0_ref, t1_ref, buf):
        for g in range(t0_ref.shape[0]):
            def start(r, c, g=g):
                row = g * t0_ref.shape[2] + r
                _row_copy(ys_ref, t0_ref[g, 0, r], y0_sc.at[buf], row, sems.at[buf]).start(priority=0)
                _row_copy(ys_ref, t1_ref[g, 0, r], y1_sc.at[buf], row, sems.at[buf]).start(priority=1)
                return c

            lax.fori_loop(0, t0_ref.shape[2], start, 0, unroll=DMA_UNROLL)

    cur = s % 2

    @pl.when(s == 0)
    def _():
        start_gathers(s0_ref, s1_ref, 0)

    @pl.when(s + 1 < pl.num_programs(0))
    def _():
        start_gathers(n0_ref, n1_ref, 1 - cur)

    for dst in (y0_sc, y1_sc):
        pltpu.make_async_copy(ys_ref.at[pl.ds(0, rows), :], dst.at[cur], sems.at[cur]).wait()
    info = info_ref[...]
    f = info[:, 4:5] * y0_sc[cur] + info[:, 5:6] * y1_sc[cur]
    o_ref[...] = _ln(alpha * x_ref[...] + f, g_ref[...], b_ref[...])


def _combine(x2d, info, ys, slot0, slot1, g2, b2, alpha):
    t, d = x2d.shape
    tm = TOKEN_TILE
    rows = ROW_TILE if t % ROW_TILE == 0 else tm
    n = t // rows
    smem_rows = pl.BlockSpec((rows // tm, 1, tm), lambda s: (s, 0, 0), memory_space=pltpu.SMEM)
    smem_next = pl.BlockSpec((rows // tm, 1, tm), lambda s: (jnp.minimum(s + 1, n - 1), 0, 0),
                             memory_space=pltpu.SMEM)
    full = lambda shape: pl.BlockSpec(shape, lambda s: (0,) * len(shape))
    return pl.pallas_call(
        functools.partial(_combine_kernel, alpha=alpha),
        out_shape=jax.ShapeDtypeStruct((t, d), F32),
        grid=(n,),
        in_specs=[smem_rows, smem_rows, smem_next, smem_next,
                  pl.BlockSpec((rows, ROUTER_LANES), lambda s: (s, 0)),
                  pl.BlockSpec((rows, d), lambda s: (s, 0)),
                  pl.BlockSpec(memory_space=pl.ANY),
                  full(g2.shape), full(b2.shape)],
        out_specs=pl.BlockSpec((rows, d), lambda s: (s, 0)),
        scratch_shapes=[pltpu.VMEM((2, rows, d), F32), pltpu.VMEM((2, rows, d), F32),
                        pltpu.SemaphoreType.DMA((2,))],
        compiler_params=pltpu.CompilerParams(
            dimension_semantics=("arbitrary",), vmem_limit_bytes=VMEM_LIMIT),
        name="combine",
    )(slot0, slot1, slot0, slot1, info, x2d, ys, g2, b2)


def _moe(x2d, layer, w_rg, b_rg, w_re, b_re, w_gate, w_up, w_down, g2, b2, alpha):
    t, d = x2d.shape
    tm = TOKEN_TILE
    wr = jnp.zeros((ROUTER_LANES, d), F32).at[0:N_GROUPS].set(w_rg.T)
    wr = wr.at[ROUTER_OFF:ROUTER_OFF + N_EXPERTS].set(w_re.T)
    br = jnp.zeros((ROUTER_LANES, 1), F32).at[0:N_GROUPS, 0].set(b_rg)
    br = br.at[ROUTER_OFF:ROUTER_OFF + N_EXPERTS, 0].set(b_re)
    whi, wlo = _split_bf16(wr)
    ridx = lax.broadcasted_iota(jnp.int32, (tm, tm), 0)
    cidx = lax.broadcasted_iota(jnp.int32, (tm, tm), 1)
    tri = (ridx < cidx).astype(BF16)
    info, sel, cnt = _router(x2d, whi, wlo, br, tri)

    counts = cnt[ROUTER_OFF:ROUTER_OFF + N_EXPERTS, 0].astype(jnp.int32)
    padded = (counts + tm - 1) // tm * tm
    ends = jnp.cumsum(padded)
    base = ends - padded
    base_col = jnp.zeros((ROUTER_ROWS, 1), F32).at[ROUTER_OFF:ROUTER_OFF + N_EXPERTS, 0].set(base.astype(F32))
    slots = _slots(sel, base_col)
    slot0, slot1 = slots[:, 0:1, :], slots[:, 1:2, :]

    n_slots = 2 * t + N_EXPERTS * tm
    n_tiles = n_slots // tm
    n_used = ends[-1] // tm
    tile_ids = jnp.arange(n_tiles, dtype=jnp.int32)
    tile_expert = jnp.sum((jnp.minimum(tile_ids, n_used - 1)[:, None] * tm >= ends[None, :]).astype(jnp.int32),
                          axis=1)
    lo = jnp.maximum(base[None, :], tile_ids[:, None] * tm)
    hi = jnp.minimum((base + counts)[None, :], (tile_ids[:, None] + 1) * tm)
    tile_valid = jnp.sum(jnp.maximum(hi - lo, 0), axis=1).astype(jnp.int32)

    used = tile_valid > 0
    tile_first = (used & ((tile_ids == 0) | (tile_expert != jnp.roll(tile_expert, 1)))).astype(jnp.int32)
    tile_buf = ((jnp.cumsum(tile_first) - 1) % 2).astype(jnp.int32)
    experts = jnp.arange(N_EXPERTS, dtype=jnp.int32)
    later_nonempty = (experts[None, :] > experts[:, None]) & (counts[None, :] > 0)
    next_expert = jnp.min(jnp.where(later_nonempty, experts[None, :], N_EXPERTS), axis=1)
    next_expert = jnp.where(next_expert < N_EXPERTS, next_expert, -1)
    tile_next = jnp.sum(jnp.where(tile_expert[:, None] == experts[None, :], next_expert[None, :], 0),
                        axis=1).astype(jnp.int32)

    xs = _dispatch(x2d, slot0, slot1, tile_valid)
    ys = _expert_mlp(xs, tile_expert, tile_valid, tile_first, tile_buf, tile_next, w_gate, w_up, w_down, layer)
    return _combine(x2d, info, ys, slot0, slot1, g2, b2, alpha)


def kernel(x, mem, w_mem_kv, conv_w_in, conv_dw_w, conv_dw_b, conv_ln_g, conv_ln_b, moba_w_in, w_o,
           ln1_g, ln1_b, w_rg, b_rg, w_re, b_re, w_gate, w_up, w_down, ln2_g, ln2_b):
    batch, seq, d = x.shape
    depth = w_o.shape[0]
    alpha = (2 * depth) ** 0.25
    t = batch * seq
    row = lambda a: a.reshape(1, -1)

    kbd, vbd = _memkv(mem, w_mem_kv)

    e_shape = w_gate.shape
    w_gate = w_gate.reshape((-1,) + e_shape[-2:])
    w_up = w_up.reshape((-1,) + e_shape[-2:])
    w_down = w_down.reshape((-1,) + w_down.shape[-2:])

    x2d = x.reshape(t, d)
    for i in range(depth):
        j = i // 2
        wo = w_o[i].astype(BF16)
        if i % 2 == 0:
            dw_w = jnp.broadcast_to(conv_dw_w[j][:, None, :], (CONV_WIDTH, SUBLANES, conv_dw_w.shape[-1]))
            w_in = conv_w_in[j].astype(BF16).reshape(d, -1, IN_PROJ_CHUNK).transpose(1, 0, 2)
            x2d = _conv_mixer(x2d, batch, w_in, dw_w, row(conv_dw_b[j]),
                              row(conv_ln_g[j]), row(conv_ln_b[j]), kbd, vbd, wo,
                              row(ln1_g[i]), row(ln1_b[i]), alpha)
        else:
            x2d = _moba_mixer(x2d, batch, moba_w_in[j].astype(BF16), kbd, vbd, wo,
                              row(ln1_g[i]), row(ln1_b[i]), alpha)
        x2d = _moe(x2d, i, w_rg[i], b_rg[i], w_re[i], b_re[i], w_gate, w_up, w_down,
                   row(ln2_g[i]), row(ln2_b[i]), alpha)
    return x2d.reshape(batch, seq, d)
```

```python
import functools

import jax
import jax.numpy as jnp
from jax import lax
from jax.experimental import pallas as pl
from jax.experimental.pallas import tpu as pltpu

F32 = jnp.float32
BF16 = jnp.bfloat16

HEAD_DIM = 64
MEM_LEN = 256
MEM_HEADS = 4
MEM_WIDTH = MEM_HEADS * HEAD_DIM
CONV_WIDTH = 31
MOBA_BLOCK = 256
MOBA_TOPK = 3
HEAD_ROWS = HEAD_DIM + 16
MOBA_LOOKAHEAD = 6
N_GROUPS = 4
EXPERTS_PER_GROUP = 8
N_EXPERTS = N_GROUPS * EXPERTS_PER_GROUP
LN_EPS = 1e-5

LANES = 128
SUBLANES = 8
TOKEN_TILE = 256
CONV_HALO = 32
CONV_CHUNK = 16
CONV_CHUNKS_PER_STEP = 2
IN_PROJ_CHUNK = 256
ROUTER_LANES = 128
ROUTER_ROWS = 40
ROUTER_OFF = N_GROUPS
DMA_UNROLL = 8
ROW_TILE = 512
SLOT_TILES_PER_STEP = 16
VMEM_LIMIT = 56 * 1024 * 1024

NEG_INF = float("-inf")
LOG2_E = 1.4426950408889634


def _ln(z, g, b):
    mu = jnp.mean(z, axis=-1, keepdims=True)
    zc = z - mu
    var = jnp.mean(zc * zc, axis=-1, keepdims=True)
    return zc * lax.rsqrt(var + LN_EPS) * g + b


def _dot(a, b):
    return jnp.dot(a, b, preferred_element_type=F32)


def _dot_nt(a, b):
    return lax.dot_general(a, b, (((1,), (1,)), ((), ())), preferred_element_type=F32)


def _split_bf16(x):
    hi = x.astype(BF16)
    lo = (x - hi.astype(F32)).astype(BF16)
    return hi, lo


def _memkv_kernel(mem_ref, w_ref, kbd_ref, vbd_ref):
    kv = _dot(mem_ref[...].astype(BF16), w_ref[...].astype(BF16))
    k_t = kv[:, 0:MEM_WIDTH].T
    v = kv[:, MEM_WIDTH:]
    r = lax.broadcasted_iota(jnp.int32, kbd_ref.shape, 0)
    c = lax.broadcasted_iota(jnp.int32, kbd_ref.shape, 1)
    kbd_ref[...] = jnp.where(r // HEAD_DIM == c // MEM_LEN,
                             jnp.concatenate([k_t] * MEM_HEADS, axis=1), 0.0).astype(BF16)
    r = lax.broadcasted_iota(jnp.int32, vbd_ref.shape, 0)
    c = lax.broadcasted_iota(jnp.int32, vbd_ref.shape, 1)
    vbd_ref[...] = jnp.where(r // MEM_LEN == c // HEAD_DIM,
                             jnp.concatenate([v] * MEM_HEADS, axis=0), 0.0).astype(BF16)


def _memkv(mem, w):
    batch, m, d = mem.shape
    return pl.pallas_call(
        _memkv_kernel,
        out_shape=(jax.ShapeDtypeStruct((batch, MEM_WIDTH, MEM_HEADS * m), BF16),
                   jax.ShapeDtypeStruct((batch, MEM_HEADS * m, MEM_WIDTH), BF16)),
        grid=(batch,),
        in_specs=[pl.BlockSpec((None, m, d), lambda b: (b, 0, 0)),
                  pl.BlockSpec(w.shape, lambda b: (0, 0))],
        out_specs=(pl.BlockSpec((None, MEM_WIDTH, MEM_HEADS * m), lambda b: (b, 0, 0)),
                   pl.BlockSpec((None, MEM_HEADS * m, MEM_WIDTH), lambda b: (b, 0, 0))),
        compiler_params=pltpu.CompilerParams(dimension_semantics=("arbitrary",)),
        name="memkv",
    )(mem, w)


def _mem_attention(qm, kbd, vbd):
    s = _dot(qm.astype(BF16), kbd) * (HEAD_DIM ** -0.5)
    parts = []
    for h in range(MEM_HEADS):
        seg = s[:, h * MEM_LEN:(h + 1) * MEM_LEN]
        m = jnp.max(seg, axis=-1, keepdims=True)
        e = jnp.exp(seg - m)
        parts.append(e / jnp.sum(e, axis=-1, keepdims=True))
    p = jnp.concatenate(parts, axis=-1)
    return _dot(p.astype(BF16), vbd)


def _out_proj_ln(x, y_mix, y_mem, wo_ref, g_ref, b_ref, alpha, mix_w):
    y = _dot(y_mix.astype(BF16), wo_ref[0:mix_w, :]) + _dot(y_mem.astype(BF16), wo_ref[mix_w:, :])
    return _ln(alpha * x + y, g_ref[...], b_ref[...])


def _conv_mixer_kernel(x_ref, xn_ref, win_ref, dww_ref, dwb_ref, cg_ref, cb_ref, kbd_ref, vbd_ref,
                       wo_ref, g1_ref, b1_ref, o_ref, ustage, xn_sc, hbuf, zbuf, cbuf, *, alpha, mix_w):
    tm = x_ref.shape[0]
    n_cc = win_ref.shape[0]
    per = mix_w // IN_PROJ_CHUNK
    j = pl.program_id(1)

    @pl.when((pl.program_id(0) == 0) & (j == 0))
    def _():
        xb = x_ref[...].astype(BF16)
        for c in range(n_cc):
            ustage[c] = _dot(xb, win_ref[c])

    x = x_ref[...]
    qm = ustage[2 * per]
    h = jnp.concatenate([ustage[c] * jax.nn.sigmoid(ustage[per + c]) for c in range(per)], axis=1)
    xn_sc[...] = xn_ref[...].astype(BF16)

    @pl.when(j == 0)
    def _():
        hbuf[0:CONV_HALO, :] = jnp.zeros((CONV_HALO, mix_w), F32)

    hbuf[CONV_HALO:CONV_HALO + tm, :] = h

    shifted_rows = CONV_HALO + tm - SUBLANES
    for a in range(1, SUBLANES):
        zbuf[a - 1, 0:shifted_rows, :] = hbuf[a:a + shifted_rows, :]

    first = CONV_HALO - (CONV_WIDTH - 1)
    aligned = lambda v, m: v if isinstance(v, int) else pl.multiple_of(v, m)

    def conv_rows(r0):
        acc = jnp.broadcast_to(dwb_ref[...], (CONV_CHUNK, mix_w))
        for a in range(SUBLANES):
            offs = [first + k - a for k in range(CONV_WIDTH) if (first + k) % SUBLANES == a]
            r = aligned(r0 + offs[0], SUBLANES)
            span = offs[-1] - offs[0] + CONV_CHUNK
            z = hbuf[pl.ds(r, span), :] if a == 0 else zbuf[a - 1, pl.ds(r, span), :]
            for off in offs:
                k = off + a - first
                rows = z[off - offs[0]:off - offs[0] + CONV_CHUNK, :]
                acc = acc + jnp.tile(dww_ref[k], (CONV_CHUNK // SUBLANES, 1)) * rows
        cbuf[pl.ds(r0, CONV_CHUNK), :] = acc

    def project_next(c):
        ustage[c] = _dot(xn_sc[...], win_ref[c])

    step_rows = CONV_CHUNK * CONV_CHUNKS_PER_STEP
    paired = min(n_cc, tm // step_rows)

    def paired_step(c, carry):
        project_next(c)
        for q in range(CONV_CHUNKS_PER_STEP):
            conv_rows(pl.multiple_of(c * step_rows + q * CONV_CHUNK, CONV_CHUNK))
        return carry

    lax.fori_loop(0, paired, paired_step, 0)
    for r0 in range(paired * step_rows, tm, CONV_CHUNK):
        conv_rows(r0)
    for c in range(paired, n_cc):
        project_next(c)

    hbuf[0:CONV_HALO, :] = hbuf[tm:tm + CONV_HALO, :]

    cn = _ln(cbuf[...], cg_ref[...], cb_ref[...])
    y_mix = cn * jax.nn.sigmoid(cn)
    y_mem = _mem_attention(qm, kbd_ref[...], vbd_ref[...])
    o_ref[...] = _out_proj_ln(x, y_mix, y_mem, wo_ref, g1_ref, b1_ref, alpha, mix_w)


def _conv_mixer(x2d, batch, w_in, dw_w, dw_b, cg, cb, kbd, vbd, wo, g1, b1, alpha):
    t, d = x2d.shape
    tm = TOKEN_TILE
    nj = t // batch // tm
    mix_w = d - MEM_WIDTH
    full = lambda shape: pl.BlockSpec(shape, lambda b, j: (0,) * len(shape))
    return pl.pallas_call(
        functools.partial(_conv_mixer_kernel, alpha=alpha, mix_w=mix_w),
        out_shape=jax.ShapeDtypeStruct((t, d), F32),
        grid=(batch, nj),
        in_specs=[
            pl.BlockSpec((tm, d), lambda b, j: (b * nj + j, 0)),
            pl.BlockSpec((tm, d), lambda b, j: (jnp.minimum(b * nj + j + 1, batch * nj - 1), 0)),
            full(w_in.shape), full(dw_w.shape), full(dw_b.shape), full(cg.shape), full(cb.shape),
            pl.BlockSpec((None,) + kbd.shape[1:], lambda b, j: (b, 0, 0)),
            pl.BlockSpec((None,) + vbd.shape[1:], lambda b, j: (b, 0, 0)),
            full(wo.shape), full(g1.shape), full(b1.shape),
        ],
        out_specs=pl.BlockSpec((tm, d), lambda b, j: (b * nj + j, 0)),
        scratch_shapes=[pltpu.VMEM((w_in.shape[0], tm, IN_PROJ_CHUNK), F32),
                        pltpu.VMEM((tm, d), BF16),
                        pltpu.VMEM((CONV_HALO + tm, mix_w), F32),
                        pltpu.VMEM((SUBLANES - 1, CONV_HALO + tm, mix_w), F32),
                        pltpu.VMEM((tm, mix_w), F32)],
        compiler_params=pltpu.CompilerParams(
            dimension_semantics=("arbitrary", "arbitrary"), vmem_limit_bytes=VMEM_LIMIT),
        name="conv_mixer",
    )(x2d, x2d, w_in, dw_w, dw_b, cg, cb, kbd, vbd, wo, g1, b1)


def _moba_select_bias(gate, i):
    nb = gate.shape[1]
    blk = lax.broadcasted_iota(jnp.int32, gate.shape, 1).astype(F32)
    past = blk < i.astype(F32)
    gm = jnp.where(past, gate, NEG_INF)
    bias = jnp.full(gate.shape, NEG_INF, F32)
    for _ in range(MOBA_TOPK):
        top = jnp.max(gm, axis=1, keepdims=True)
        first = jnp.min(jnp.where(gm == top, blk, float(nb)), axis=1, keepdims=True)
        taken = blk == first
        bias = jnp.where(taken & past, 0.0, bias)
        gm = jnp.where(taken, NEG_INF, gm)
    return bias


def _moba_mixer_kernel(x_ref, xn_ref, win_ref, kbd_ref, vbd_ref, wo_ref, g1_ref, b1_ref, o_ref,
                       ustage, xn_sc, k_sc, vt_sc, kmt_sc, bias_sc, qh_sc, m_sc, acc_sc, yt_sc,
                       *, alpha, mix_w, nb):
    tm = x_ref.shape[0]
    heads = mix_w // HEAD_DIM
    n_cc = win_ref.shape[0]
    per = mix_w // IN_PROJ_CHUNK
    i = pl.program_id(1)

    @pl.when((pl.program_id(0) == 0) & (i == 0))
    def _():
        xb = x_ref[...].astype(BF16)
        for c in range(n_cc):
            ustage[c] = _dot(xb, win_ref[c])

    x = x_ref[...]
    part = lambda p: jnp.concatenate([ustage[p * per + c] for c in range(per)], axis=1)
    q = part(0) * (HEAD_DIM ** -0.5 * LOG2_E)
    k = part(1)
    v = part(2)
    qm = ustage[3 * per]
    xn_sc[...] = xn_ref[...].astype(BF16)

    def project_next(c):
        ustage[c] = _dot(xn_sc[...], win_ref[c])

    @pl.when(i == 0)
    def _():
        kmt_sc[...] = jnp.zeros(kmt_sc.shape, F32)

    k_sc[i] = k.astype(BF16)
    v_t = v.T.astype(BF16)
    for h in range(heads):
        vt_sc[i, h * HEAD_ROWS:h * HEAD_ROWS + HEAD_DIM, :] = v_t[h * HEAD_DIM:(h + 1) * HEAD_DIM, :]
        vt_sc[i, h * HEAD_ROWS + HEAD_DIM:(h + 1) * HEAD_ROWS, :] = jnp.ones((HEAD_ROWS - HEAD_DIM, tm), BF16)
    kmean = jnp.mean(k, axis=0, keepdims=True)
    lane = lax.broadcasted_iota(jnp.int32, (1, mix_w), 1)
    for h in range(heads):
        in_head = (lane >= h * HEAD_DIM) & (lane < (h + 1) * HEAD_DIM)
        kmt_sc[pl.ds(h * nb + i, 1), :] = jnp.where(in_head, kmean, 0.0)

    q_hi, q_lo = _split_bf16(q)
    km_hi, km_lo = _split_bf16(kmt_sc[...])
    gate_t = _dot_nt(km_hi, q_hi) + _dot_nt(km_hi, q_lo) + _dot_nt(km_lo, q_hi)
    bias_sc[...] = _moba_select_bias(gate_t.reshape(heads, nb, tm), i)

    lane_p = lax.broadcasted_iota(jnp.int32, (tm, 2 * HEAD_DIM), 1)
    for h in range(heads):
        qp = q_hi[:, (h // 2) * 2 * HEAD_DIM:(h // 2 + 1) * 2 * HEAD_DIM]
        keep = (lane_p < HEAD_DIM) if h % 2 == 0 else (lane_p >= HEAD_DIM)
        qh_sc[h] = jnp.where(keep, qp, jnp.zeros_like(qp))

    def scores(j, h):
        cols = slice((h // 2) * 2 * HEAD_DIM, (h // 2 + 1) * 2 * HEAD_DIM)
        return _dot_nt(k_sc[j, :, cols], qh_sc[h])

    rows = lambda h: slice(h * HEAD_ROWS, (h + 1) * HEAD_ROWS)

    kidx = lax.broadcasted_iota(jnp.int32, (tm, tm), 0)
    qidx = lax.broadcasted_iota(jnp.int32, (tm, tm), 1)
    causal = kidx <= qidx
    def heads_pipelined(j):
        pending = [scores(j, h) for h in range(MOBA_LOOKAHEAD)]
        for h in range(heads):
            if h + MOBA_LOOKAHEAD < heads:
                pending.append(scores(j, h + MOBA_LOOKAHEAD))
            yield h, pending.pop(0)

    def own_block(j, c):
        for h, s in heads_pipelined(j):
            if h < n_cc:
                project_next(h)
            s = jnp.where(causal, s, NEG_INF)
            m = jnp.max(s, axis=0, keepdims=True)
            e = jnp.exp2(s - m)
            m_sc[h] = m
            acc_sc[rows(h), :] = _dot(vt_sc[j, rows(h), :], e.astype(BF16))
        return c

    lax.fori_loop(i, i + 1, own_block, 0)
    for c in range(heads, n_cc):
        project_next(c)

    def body(j, c):
        for h, s in heads_pipelined(j):
            b = bias_sc[h, pl.ds(j, 1), :]
            m_old = m_sc[h]
            m_new = jnp.maximum(m_old, jnp.max(s, axis=0, keepdims=True) + b)
            e = jnp.exp2(s - (m_new - b))
            corr = jnp.exp2(m_old - m_new)
            m_sc[h] = m_new
            acc_sc[rows(h), :] = corr * acc_sc[rows(h), :] + _dot(vt_sc[j, rows(h), :], e.astype(BF16))
        return c

    lax.fori_loop(0, i, body, 0)
    for h in range(heads):
        r0 = h * HEAD_ROWS
        yt_sc[h * HEAD_DIM:(h + 1) * HEAD_DIM, :] = (acc_sc[r0:r0 + HEAD_DIM, :]
                                                     / acc_sc[r0 + HEAD_DIM:r0 + HEAD_DIM + 1, :])

    y_mix = yt_sc[...].T
    y_mem = _mem_attention(qm, kbd_ref[...], vbd_ref[...])
    o_ref[...] = _out_proj_ln(x, y_mix, y_mem, wo_ref, g1_ref, b1_ref, alpha, mix_w)


def _moba_mixer(x2d, batch, w_in, kbd, vbd, wo, g1, b1, alpha):
    t, d = x2d.shape
    tm = MOBA_BLOCK
    nb = t // batch // tm
    mix_w = d - MEM_WIDTH
    heads = mix_w // HEAD_DIM
    full = lambda shape: pl.BlockSpec(shape, lambda b, j: (0,) * len(shape))
    return pl.pallas_call(
        functools.partial(_moba_mixer_kernel, alpha=alpha, mix_w=mix_w, nb=nb),
        out_shape=jax.ShapeDtypeStruct((t, d), F32),
        grid=(batch, nb),
        in_specs=[
            pl.BlockSpec((tm, d), lambda b, j: (b * nb + j, 0)),
            pl.BlockSpec((tm, d), lambda b, j: (jnp.minimum(b * nb + j + 1, batch * nb - 1), 0)),
            full(w_in.shape),
            pl.BlockSpec((None,) + kbd.shape[1:], lambda b, j: (b, 0, 0)),
            pl.BlockSpec((None,) + vbd.shape[1:], lambda b, j: (b, 0, 0)),
            full(wo.shape), full(g1.shape), full(b1.shape),
        ],
        out_specs=pl.BlockSpec((tm, d), lambda b, j: (b * nb + j, 0)),
        scratch_shapes=[
            pltpu.VMEM((w_in.shape[0], tm, IN_PROJ_CHUNK), F32),
            pltpu.VMEM((tm, d), BF16),
            pltpu.VMEM((nb, tm, mix_w), BF16),
            pltpu.VMEM((nb, heads * HEAD_ROWS, tm), BF16),
            pltpu.VMEM((heads * nb, mix_w), F32),
            pltpu.VMEM((heads, nb, tm), F32),
            pltpu.VMEM((heads, tm, 2 * HEAD_DIM), BF16),
            pltpu.VMEM((heads, 1, tm), F32),
            pltpu.VMEM((heads * HEAD_ROWS, tm), F32),
            pltpu.VMEM((mix_w, tm), F32),
        ],
        compiler_params=pltpu.CompilerParams(
            dimension_semantics=("arbitrary", "arbitrary"), vmem_limit_bytes=VMEM_LIMIT),
        name="moba_mixer",
    )(x2d, x2d, w_in, kbd, vbd, wo, g1, b1)


def _router_kernel(x_ref, whi_ref, wlo_ref, br_ref, tri_ref, info_ref, sel_ref, cnt_ref, run_sc):
    step = pl.program_id(0)

    @pl.when(step == 0)
    def _():
        run_sc[...] = jnp.zeros(run_sc.shape, F32)

    x_hi, x_lo = _split_bf16(x_ref[...])
    logits = (_dot_nt(whi_ref[...], x_hi) + _dot_nt(wlo_ref[...], x_hi) + _dot_nt(whi_ref[...], x_lo)
              + br_ref[...])[0:ROUTER_ROWS, :]
    row = lax.broadcasted_iota(jnp.int32, logits.shape, 0).astype(F32)
    cmax = lambda a: jnp.max(a, axis=0, keepdims=True)
    cmin = lambda a: jnp.min(a, axis=0, keepdims=True)
    csum = lambda a: jnp.sum(a, axis=0, keepdims=True)
    big = float(2 * ROUTER_ROWS)

    is_g = row < N_GROUPS
    gl = jnp.where(is_g, logits, NEG_INF)
    gmax = cmax(gl)
    gidx = cmin(jnp.where(gl == gmax, row, big))
    g_w = 1.0 / csum(jnp.where(is_g, jnp.exp(gl - gmax), 0.0))

    lo = ROUTER_OFF + EXPERTS_PER_GROUP * gidx
    el = jnp.where((row >= lo) & (row < lo + EXPERTS_PER_GROUP), logits, NEG_INF)
    v0 = cmax(el)
    i0 = cmin(jnp.where(el == v0, row, big))
    el1 = jnp.where(row == i0, NEG_INF, el)
    v1 = cmax(el1)
    i1 = cmin(jnp.where(el1 == v1, row, big))
    t = jnp.exp(v1 - v0)
    w0 = g_w / (1.0 + t)
    w1 = g_w * t / (1.0 + t)

    pick0 = row == i0
    pick1 = row == i1
    onehot = jnp.where(pick0 | pick1, 1.0, 0.0)
    before = _dot(onehot.astype(BF16), tri_ref[...]) + run_sc[...]
    pos0 = csum(jnp.where(pick0, before, 0.0))
    pos1 = csum(jnp.where(pick1, before, 0.0))
    run_sc[...] = run_sc[...] + jnp.sum(onehot, axis=1, keepdims=True)
    cnt_ref[...] = run_sc[...]

    vals = (i0 - ROUTER_OFF, i1 - ROUTER_OFF, pos0, pos1, w0, w1)
    row8 = lax.broadcasted_iota(jnp.int32, sel_ref.shape, 0)
    sel = jnp.zeros(sel_ref.shape, F32)
    for c, val in enumerate(vals):
        sel = jnp.where(row8 == c, val, sel)
    sel_ref[...] = sel
    rowl = lax.broadcasted_iota(jnp.int32, (ROUTER_LANES, x_ref.shape[0]), 0)
    info_ref[...] = jnp.where(rowl == 4, w0, jnp.where(rowl == 5, w1, 0.0)).T


def _router(x2d, whi, wlo, br, tri):
    t, d = x2d.shape
    tm = TOKEN_TILE
    full = lambda shape: pl.BlockSpec(shape, lambda s: (0,) * len(shape))
    return pl.pallas_call(
        _router_kernel,
        out_shape=(jax.ShapeDtypeStruct((t, ROUTER_LANES), F32),
                   jax.ShapeDtypeStruct((t // tm, SUBLANES, tm), F32),
                   jax.ShapeDtypeStruct((ROUTER_ROWS, tm), F32)),
        grid=(t // tm,),
        in_specs=[pl.BlockSpec((tm, d), lambda s: (s, 0)),
                  full(whi.shape), full(wlo.shape), full(br.shape), full(tri.shape)],
        out_specs=(pl.BlockSpec((tm, ROUTER_LANES), lambda s: (s, 0)),
                   pl.BlockSpec((None, SUBLANES, tm), lambda s: (s, 0, 0)),
                   full((ROUTER_ROWS, tm))),
        scratch_shapes=[pltpu.VMEM((ROUTER_ROWS, tm), F32)],
        compiler_params=pltpu.CompilerParams(dimension_semantics=("arbitrary",)),
        name="router",
    )(x2d, whi, wlo, br, tri)


def _row_copy(src, src_row, dst, dst_row, sem):
    return pltpu.make_async_copy(src.at[pl.ds(src_row, 1), :], dst.at[pl.ds(dst_row, 1), :], sem)


def _slots_kernel(sel_ref, base_ref, o_ref):
    sel = sel_ref[...]
    g, _, tm = sel.shape
    row = lax.broadcasted_iota(jnp.int32, (g, ROUTER_ROWS, tm), 1).astype(F32)
    base = base_ref[...][None]
    csum = lambda a: jnp.sum(a, axis=1, keepdims=True)
    s0 = csum(jnp.where(row == sel[:, 0:1, :] + ROUTER_OFF, base, 0.0)) + sel[:, 2:3, :]
    s1 = csum(jnp.where(row == sel[:, 1:2, :] + ROUTER_OFF, base, 0.0)) + sel[:, 3:4, :]
    row8 = lax.broadcasted_iota(jnp.int32, o_ref.shape, 1)
    o_ref[...] = jnp.where(row8 == 0, s0, jnp.where(row8 == 1, s1, 0.0)).astype(jnp.int32)


def _slots(sel, base_col):
    nt, _, tm = sel.shape
    g = SLOT_TILES_PER_STEP if nt % SLOT_TILES_PER_STEP == 0 else nt
    return pl.pallas_call(
        _slots_kernel,
        out_shape=jax.ShapeDtypeStruct((nt, SUBLANES, tm), jnp.int32),
        grid=(nt // g,),
        in_specs=[pl.BlockSpec((g, SUBLANES, tm), lambda s: (s, 0, 0)),
                  pl.BlockSpec(base_col.shape, lambda s: (0, 0))],
        out_specs=pl.BlockSpec((g, SUBLANES, tm), lambda s: (s, 0, 0)),
        compiler_params=pltpu.CompilerParams(dimension_semantics=("arbitrary",)),
        name="slots",
    )(sel, base_col)


def _dispatch_kernel(nv_ref, s0_ref, s1_ref, x_ref, xs_ref, zero_sc, sem, fill_sem, *, tm):

    @pl.when(pl.program_id(0) == 0)
    def _():
        zero_sc[...] = jnp.zeros(zero_sc.shape, F32)

        def fill(t):
            return pltpu.make_async_copy(zero_sc, xs_ref.at[pl.ds(pl.multiple_of(t * tm, tm), tm), :], fill_sem)

        def start_fill(t, c):
            @pl.when(nv_ref[t] < tm)
            def _():
                fill(t).start()
            return c

        def wait_fill(t, c):
            @pl.when(nv_ref[t] < tm)
            def _():
                fill(t).wait()
            return c

        lax.fori_loop(0, nv_ref.shape[0], start_fill, 0)
        lax.fori_loop(0, nv_ref.shape[0], wait_fill, 0)

    rows = x_ref.shape[0]
    for g in range(s0_ref.shape[0]):
        def start(r, c, g=g):
            row = g * s0_ref.shape[2] + r
            _row_copy(x_ref, row, xs_ref, s0_ref[g, 0, r], sem).start(priority=0)
            _row_copy(x_ref, row, xs_ref, s1_ref[g, 0, r], sem).start(priority=1)
            return c

        lax.fori_loop(0, s0_ref.shape[2], start, 0, unroll=DMA_UNROLL)
    for _ in range(2):
        pltpu.make_async_copy(x_ref, xs_ref.at[pl.ds(0, rows), :], sem).wait()


def _dispatch(x2d, slot0, slot1, tile_valid):
    t, d = x2d.shape
    tm = TOKEN_TILE
    rows = ROW_TILE if t % ROW_TILE == 0 else tm
    n_slots = tile_valid.shape[0] * tm
    smem_rows = pl.BlockSpec((rows // tm, 1, tm), lambda s, nv: (s, 0, 0), memory_space=pltpu.SMEM)
    return pl.pallas_call(
        functools.partial(_dispatch_kernel, tm=tm),
        out_shape=jax.ShapeDtypeStruct((n_slots, d), F32),
        grid_spec=pltpu.PrefetchScalarGridSpec(
            num_scalar_prefetch=1,
            grid=(t // rows,),
            in_specs=[smem_rows, smem_rows, pl.BlockSpec((rows, d), lambda s, nv: (s, 0))],
            out_specs=pl.BlockSpec(memory_space=pl.ANY),
            scratch_shapes=[pltpu.VMEM((tm, d), F32), pltpu.SemaphoreType.DMA, pltpu.SemaphoreType.DMA],
        ),
        compiler_params=pltpu.CompilerParams(dimension_semantics=("arbitrary",)),
        name="dispatch",
    )(tile_valid, slot0, slot1, x2d)


def _expert_kernel(te_ref, nv_ref, first_ref, buf_ref, next_ref, xs_ref, wg_hbm, wu_hbm, wd_hbm, ys_ref,
                   wg_buf, wu_buf, wd_buf, wg_sc, wu_sc, wd_sc, sems, *, base):
    t = pl.program_id(0)
    used = nv_ref[t] > 0

    def weight_copies(e, b):
        return (pltpu.make_async_copy(wg_hbm.at[base + e], wg_buf.at[b], sems.at[b, 0]),
                pltpu.make_async_copy(wu_hbm.at[base + e], wu_buf.at[b], sems.at[b, 1]),
                pltpu.make_async_copy(wd_hbm.at[base + e], wd_buf.at[b], sems.at[b, 2]))

    @pl.when(t == 0)
    def _():
        for c in weight_copies(te_ref[0], 0):
            c.start()

    @pl.when(first_ref[t] == 1)
    def _():
        b = buf_ref[t]
        for c in weight_copies(te_ref[t], b):
            c.wait()
        wg_sc[...] = wg_buf[b].astype(BF16)
        wu_sc[...] = wu_buf[b].astype(BF16)
        wd_sc[...] = wd_buf[b].astype(BF16)

        @pl.when(next_ref[t] >= 0)
        def _():
            for c in weight_copies(next_ref[t], 1 - b):
                c.start()

    @pl.when(used)
    def _():
        xb = xs_ref[...].astype(BF16)
        hg = _dot(xb, wg_sc[...])
        hu = _dot(xb, wu_sc[...])
        h = hg * jax.nn.sigmoid(hg) * hu
        ys_ref[...] = _dot(h.astype(BF16), wd_sc[...])

    @pl.when(jnp.logical_not(used))
    def _():
        ys_ref[...] = jnp.zeros(ys_ref.shape, F32)


def _expert_mlp(xs, tile_expert, tile_valid, tile_first, tile_buf, tile_next, w_gate, w_up, w_down, layer):
    ns, d = xs.shape
    f = w_gate.shape[-1]
    tm = TOKEN_TILE
    hbm = pl.BlockSpec(memory_space=pl.ANY)
    return pl.pallas_call(
        functools.partial(_expert_kernel, base=layer * N_EXPERTS),
        out_shape=jax.ShapeDtypeStruct((ns, d), F32),
        grid_spec=pltpu.PrefetchScalarGridSpec(
            num_scalar_prefetch=5,
            grid=(ns // tm,),
            in_specs=[pl.BlockSpec((tm, d), lambda t, *_: (t, 0)), hbm, hbm, hbm],
            out_specs=pl.BlockSpec((tm, d), lambda t, *_: (t, 0)),
            scratch_shapes=[pltpu.VMEM((2, d, f), F32), pltpu.VMEM((2, d, f), F32), pltpu.VMEM((2, f, d), F32),
                            pltpu.VMEM((d, f), BF16), pltpu.VMEM((d, f), BF16), pltpu.VMEM((f, d), BF16),
                            pltpu.SemaphoreType.DMA((2, 3))],
        ),
        compiler_params=pltpu.CompilerParams(
            dimension_semantics=("arbitrary",), vmem_limit_bytes=VMEM_LIMIT),
        name="expert_mlp",
    )(tile_expert, tile_valid, tile_first, tile_buf, tile_next, xs, w_gate, w_up, w_down)


def _combine_kernel(s0_ref, s1_ref, n0_ref, n1_ref, info_ref, x_ref, ys_ref, g_ref, b_ref, o_ref,
                    y0_sc, y1_sc, sems, *, alpha):
    s = pl.program_id(0)
    rows = x_ref.shape[0]

    def start_gathers(t0_ref, t1_ref, buf):
        for g in range(t0_ref.shape[0]):
            def start(r, c, g=g):
                row = g * t0_ref.shape[2] + r
                _row_copy(ys_ref, t0_ref[g, 0, r], y0_sc.at[buf], row, sems.at[buf]).start(priority=0)
                _row_copy(ys_ref, t1_ref[g, 0, r], y1_sc.at[buf], row, sems.at[buf]).start(priority=1)
                return c

            lax.fori_loop(0, t0_ref.shape[2], start, 0, unroll=DMA_UNROLL)

    cur = s % 2

    @pl.when(s == 0)
    def _():
        start_gathers(s0_ref, s1_ref, 0)

    @pl.when(s + 1 < pl.num_programs(0))
    def _():
        start_gathers(n0_ref, n1_ref, 1 - cur)

    for dst in (y0_sc, y1_sc):
        pltpu.make_async_copy(ys_ref.at[pl.ds(0, rows), :], dst.at[cur], sems.at[cur]).wait()
    info = info_ref[...]
    f = info[:, 4:5] * y0_sc[cur] + info[:, 5:6] * y1_sc[cur]
    o_ref[...] = _ln(alpha * x_ref[...] + f, g_ref[...], b_ref[...])


def _combine(x2d, info, ys, slot0, slot1, g2, b2, alpha):
    t, d = x2d.shape
    tm = TOKEN_TILE
    rows = ROW_TILE if t % ROW_TILE == 0 else tm
    n = t // rows
    smem_rows = pl.BlockSpec((rows // tm, 1, tm), lambda s: (s, 0, 0), memory_space=pltpu.SMEM)
    smem_next = pl.BlockSpec((rows // tm, 1, tm), lambda s: (jnp.minimum(s + 1, n - 1), 0, 0),
                             memory_space=pltpu.SMEM)
    full = lambda shape: pl.BlockSpec(shape, lambda s: (0,) * len(shape))
    return pl.pallas_call(
        functools.partial(_combine_kernel, alpha=alpha),
        out_shape=jax.ShapeDtypeStruct((t, d), F32),
        grid=(n,),
        in_specs=[smem_rows, smem_rows, smem_next, smem_next,
                  pl.BlockSpec((rows, ROUTER_LANES), lambda s: (s, 0)),
                  pl.BlockSpec((rows, d), lambda s: (s, 0)),
                  pl.BlockSpec(memory_space=pl.ANY),
                  full(g2.shape), full(b2.shape)],
        out_specs=pl.BlockSpec((rows, d), lambda s: (s, 0)),
        scratch_shapes=[pltpu.VMEM((2, rows, d), F32), pltpu.VMEM((2, rows, d), F32),
                        pltpu.SemaphoreType.DMA((2,))],
        compiler_params=pltpu.CompilerParams(
            dimension_semantics=("arbitrary",), vmem_limit_bytes=VMEM_LIMIT),
        name="combine",
    )(slot0, slot1, slot0, slot1, info, x2d, ys, g2, b2)


def _moe(x2d, layer, w_rg, b_rg, w_re, b_re, w_gate, w_up, w_down, g2, b2, alpha):
    t, d = x2d.shape
    tm = TOKEN_TILE
    wr = jnp.zeros((ROUTER_LANES, d), F32).at[0:N_GROUPS].set(w_rg.T)
    wr = wr.at[ROUTER_OFF:ROUTER_OFF + N_EXPERTS].set(w_re.T)
    br = jnp.zeros((ROUTER_LANES, 1), F32).at[0:N_GROUPS, 0].set(b_rg)
    br = br.at[ROUTER_OFF:ROUTER_OFF + N_EXPERTS, 0].set(b_re)
    whi, wlo = _split_bf16(wr)
    ridx = lax.broadcasted_iota(jnp.int32, (tm, tm), 0)
    cidx = lax.broadcasted_iota(jnp.int32, (tm, tm), 1)
    tri = (ridx < cidx).astype(BF16)
    info, sel, cnt = _router(x2d, whi, wlo, br, tri)

    counts = cnt[ROUTER_OFF:ROUTER_OFF + N_EXPERTS, 0].astype(jnp.int32)
    padded = (counts + tm - 1) // tm * tm
    ends = jnp.cumsum(padded)
    base = ends - padded
    base_col = jnp.zeros((ROUTER_ROWS, 1), F32).at[ROUTER_OFF:ROUTER_OFF + N_EXPERTS, 0].set(base.astype(F32))
    slots = _slots(sel, base_col)
    slot0, slot1 = slots[:, 0:1, :], slots[:, 1:2, :]

    n_slots = 2 * t + N_EXPERTS * tm
    n_tiles = n_slots // tm
    n_used = ends[-1] // tm
    tile_ids = jnp.arange(n_tiles, dtype=jnp.int32)
    tile_expert = jnp.sum((jnp.minimum(tile_ids, n_used - 1)[:, None] * tm >= ends[None, :]).astype(jnp.int32),
                          axis=1)
    lo = jnp.maximum(base[None, :], tile_ids[:, None] * tm)
    hi = jnp.minimum((base + counts)[None, :], (tile_ids[:, None] + 1) * tm)
    tile_valid = jnp.sum(jnp.maximum(hi - lo, 0), axis=1).astype(jnp.int32)

    used = tile_valid > 0
    tile_first = (used & ((tile_ids == 0) | (tile_expert != jnp.roll(tile_expert, 1)))).astype(jnp.int32)
    tile_buf = ((jnp.cumsum(tile_first) - 1) % 2).astype(jnp.int32)
    experts = jnp.arange(N_EXPERTS, dtype=jnp.int32)
    later_nonempty = (experts[None, :] > experts[:, None]) & (counts[None, :] > 0)
    next_expert = jnp.min(jnp.where(later_nonempty, experts[None, :], N_EXPERTS), axis=1)
    next_expert = jnp.where(next_expert < N_EXPERTS, next_expert, -1)
    tile_next = jnp.sum(jnp.where(tile_expert[:, None] == experts[None, :], next_expert[None, :], 0),
                        axis=1).astype(jnp.int32)

    xs = _dispatch(x2d, slot0, slot1, tile_valid)
    ys = _expert_mlp(xs, tile_expert, tile_valid, tile_first, tile_buf, tile_next, w_gate, w_up, w_down, layer)
    return _combine(x2d, info, ys, slot0, slot1, g2, b2, alpha)


def kernel(x, mem, w_mem_kv, conv_w_in, conv_dw_w, conv_dw_b, conv_ln_g, conv_ln_b, moba_w_in, w_o,
           ln1_g, ln1_b, w_rg, b_rg, w_re, b_re, w_gate, w_up, w_down, ln2_g, ln2_b):
    batch, seq, d = x.shape
    depth = w_o.shape[0]
    alpha = (2 * depth) ** 0.25
    t = batch * seq
    row = lambda a: a.reshape(1, -1)

    kbd, vbd = _memkv(mem, w_mem_kv)

    e_shape = w_gate.shape
    w_gate = w_gate.reshape((-1,) + e_shape[-2:])
    w_up = w_up.reshape((-1,) + e_shape[-2:])
    w_down = w_down.reshape((-1,) + w_down.shape[-2:])

    x2d = x.reshape(t, d)
    for i in range(depth):
        j = i // 2
        wo = w_o[i].astype(BF16)
        if i % 2 == 0:
            dw_w = jnp.broadcast_to(conv_dw_w[j][:, None, :], (CONV_WIDTH, SUBLANES, conv_dw_w.shape[-1]))
            w_in = conv_w_in[j].astype(BF16).reshape(d, -1, IN_PROJ_CHUNK).transpose(1, 0, 2)
            x2d = _conv_mixer(x2d, batch, w_in, dw_w, row(conv_dw_b[j]),
                              row(conv_ln_g[j]), row(conv_ln_b[j]), kbd, vbd, wo,
                              row(ln1_g[i]), row(ln1_b[i]), alpha)
        else:
            w_in = moba_w_in[j].astype(BF16).reshape(d, -1, IN_PROJ_CHUNK).transpose(1, 0, 2)
            x2d = _moba_mixer(x2d, batch, w_in, kbd, vbd, wo,
                              row(ln1_g[i]), row(ln1_b[i]), alpha)
        x2d = _moe(x2d, i, w_rg[i], b_rg[i], w_re[i], b_re[i], w_gate, w_up, w_down,
                   row(ln2_g[i]), row(ln2_b[i]), alpha)
    return x2d.reshape(batch, seq, d)
```

```python
import functools

import jax
import jax.numpy as jnp
from jax import lax
from jax.experimental import pallas as pl
from jax.experimental.pallas import tpu as pltpu

F32 = jnp.float32
BF16 = jnp.bfloat16

HEAD_DIM = 64
MEM_LEN = 256
MEM_HEADS = 4
MEM_WIDTH = MEM_HEADS * HEAD_DIM
CONV_WIDTH = 31
MOBA_BLOCK = 256
MOBA_TOPK = 3
HEAD_ROWS = HEAD_DIM + 16
MOBA_LOOKAHEAD = 6
N_GROUPS = 4
EXPERTS_PER_GROUP = 8
N_EXPERTS = N_GROUPS * EXPERTS_PER_GROUP
LN_EPS = 1e-5

LANES = 128
SUBLANES = 8
TOKEN_TILE = 256
CONV_HALO = 32
CONV_CHUNK = 16
CONV_CHUNKS_PER_STEP = 2
IN_PROJ_CHUNK = 256
ROUTER_LANES = 128
ROUTER_ROWS = 40
ROUTER_OFF = N_GROUPS
DMA_UNROLL = 8
ROW_TILE = 1024
XS_RING = 3
SLOT_TILES_PER_STEP = 16
VMEM_LIMIT = 56 * 1024 * 1024

NEG_INF = float("-inf")
LOG2_E = 1.4426950408889634


def _ln(z, g, b):
    mu = jnp.mean(z, axis=-1, keepdims=True)
    zc = z - mu
    var = jnp.mean(zc * zc, axis=-1, keepdims=True)
    return zc * lax.rsqrt(var + LN_EPS) * g + b


def _dot(a, b):
    return jnp.dot(a, b, preferred_element_type=F32)


def _dot_nt(a, b):
    return lax.dot_general(a, b, (((1,), (1,)), ((), ())), preferred_element_type=F32)


def _split_bf16(x):
    hi = x.astype(BF16)
    lo = (x - hi.astype(F32)).astype(BF16)
    return hi, lo


def _memkv_kernel(mem_ref, w_ref, kbd_ref, vbd_ref):
    kv = _dot(mem_ref[...].astype(BF16), w_ref[...].astype(BF16))
    k_t = kv[:, 0:MEM_WIDTH].T
    v = kv[:, MEM_WIDTH:]
    r = lax.broadcasted_iota(jnp.int32, kbd_ref.shape, 0)
    c = lax.broadcasted_iota(jnp.int32, kbd_ref.shape, 1)
    kbd_ref[...] = jnp.where(r // HEAD_DIM == c // MEM_LEN,
                             jnp.concatenate([k_t] * MEM_HEADS, axis=1), 0.0).astype(BF16)
    r = lax.broadcasted_iota(jnp.int32, vbd_ref.shape, 0)
    c = lax.broadcasted_iota(jnp.int32, vbd_ref.shape, 1)
    vbd_ref[...] = jnp.where(r // MEM_LEN == c // HEAD_DIM,
                             jnp.concatenate([v] * MEM_HEADS, axis=0), 0.0).astype(BF16)


def _memkv(mem, w):
    batch, m, d = mem.shape
    return pl.pallas_call(
        _memkv_kernel,
        out_shape=(jax.ShapeDtypeStruct((batch, MEM_WIDTH, MEM_HEADS * m), BF16),
                   jax.ShapeDtypeStruct((batch, MEM_HEADS * m, MEM_WIDTH), BF16)),
        grid=(batch,),
        in_specs=[pl.BlockSpec((None, m, d), lambda b: (b, 0, 0)),
                  pl.BlockSpec(w.shape, lambda b: (0, 0))],
        out_specs=(pl.BlockSpec((None, MEM_WIDTH, MEM_HEADS * m), lambda b: (b, 0, 0)),
                   pl.BlockSpec((None, MEM_HEADS * m, MEM_WIDTH), lambda b: (b, 0, 0))),
        compiler_params=pltpu.CompilerParams(dimension_semantics=("arbitrary",)),
        name="memkv",
    )(mem, w)


def _mem_attention(qm, kbd, vbd):
    s = _dot(qm.astype(BF16), kbd) * (HEAD_DIM ** -0.5)
    parts = []
    for h in range(MEM_HEADS):
        seg = s[:, h * MEM_LEN:(h + 1) * MEM_LEN]
        m = jnp.max(seg, axis=-1, keepdims=True)
        e = jnp.exp(seg - m)
        parts.append(e / jnp.sum(e, axis=-1, keepdims=True))
    p = jnp.concatenate(parts, axis=-1)
    return _dot(p.astype(BF16), vbd)


def _out_proj_ln(x, y_mix, y_mem, wo_ref, g_ref, b_ref, alpha, mix_w):
    y = _dot(y_mix.astype(BF16), wo_ref[0:mix_w, :]) + _dot(y_mem.astype(BF16), wo_ref[mix_w:, :])
    return _ln(alpha * x + y, g_ref[...], b_ref[...])


def _conv_mixer_kernel(x_ref, xn_ref, win_ref, dww_ref, dwb_ref, cg_ref, cb_ref, kbd_ref, vbd_ref,
                       wo_ref, g1_ref, b1_ref, o_ref, ustage, xn_sc, hbuf, zbuf, cbuf, *, alpha, mix_w):
    tm = x_ref.shape[0]
    n_cc = win_ref.shape[0]
    per = mix_w // IN_PROJ_CHUNK
    j = pl.program_id(1)

    @pl.when((pl.program_id(0) == 0) & (j == 0))
    def _():
        xb = x_ref[...].astype(BF16)
        for c in range(n_cc):
            ustage[c] = _dot(xb, win_ref[c])

    x = x_ref[...]
    qm = ustage[2 * per]
    h = jnp.concatenate([ustage[c] * jax.nn.sigmoid(ustage[per + c]) for c in range(per)], axis=1)
    xn_sc[...] = xn_ref[...].astype(BF16)

    @pl.when(j == 0)
    def _():
        hbuf[0:CONV_HALO, :] = jnp.zeros((CONV_HALO, mix_w), F32)

    hbuf[CONV_HALO:CONV_HALO + tm, :] = h

    shifted_rows = CONV_HALO + tm - SUBLANES
    for a in range(1, SUBLANES):
        zbuf[a - 1, 0:shifted_rows, :] = hbuf[a:a + shifted_rows, :]

    first = CONV_HALO - (CONV_WIDTH - 1)
    aligned = lambda v, m: v if isinstance(v, int) else pl.multiple_of(v, m)

    def conv_rows(r0):
        acc = jnp.broadcast_to(dwb_ref[...], (CONV_CHUNK, mix_w))
        for a in range(SUBLANES):
            offs = [first + k - a for k in range(CONV_WIDTH) if (first + k) % SUBLANES == a]
            r = aligned(r0 + offs[0], SUBLANES)
            span = offs[-1] - offs[0] + CONV_CHUNK
            z = hbuf[pl.ds(r, span), :] if a == 0 else zbuf[a - 1, pl.ds(r, span), :]
            for off in offs:
                k = off + a - first
                rows = z[off - offs[0]:off - offs[0] + CONV_CHUNK, :]
                acc = acc + jnp.tile(dww_ref[k], (CONV_CHUNK // SUBLANES, 1)) * rows
        cbuf[pl.ds(r0, CONV_CHUNK), :] = acc

    def project_next(c):
        ustage[c] = _dot(xn_sc[...], win_ref[c])

    step_rows = CONV_CHUNK * CONV_CHUNKS_PER_STEP
    paired = min(n_cc, tm // step_rows)

    def paired_step(c, carry):
        project_next(c)
        for q in range(CONV_CHUNKS_PER_STEP):
            conv_rows(pl.multiple_of(c * step_rows + q * CONV_CHUNK, CONV_CHUNK))
        return carry

    lax.fori_loop(0, paired, paired_step, 0)
    for r0 in range(paired * step_rows, tm, CONV_CHUNK):
        conv_rows(r0)
    for c in range(paired, n_cc):
        project_next(c)

    hbuf[0:CONV_HALO, :] = hbuf[tm:tm + CONV_HALO, :]

    cn = _ln(cbuf[...], cg_ref[...], cb_ref[...])
    y_mix = cn * jax.nn.sigmoid(cn)
    y_mem = _mem_attention(qm, kbd_ref[...], vbd_ref[...])
    o_ref[...] = _out_proj_ln(x, y_mix, y_mem, wo_ref, g1_ref, b1_ref, alpha, mix_w)


def _conv_mixer(x2d, batch, w_in, dw_w, dw_b, cg, cb, kbd, vbd, wo, g1, b1, alpha):
    t, d = x2d.shape
    tm = TOKEN_TILE
    nj = t // batch // tm
    mix_w = d - MEM_WIDTH
    full = lambda shape: pl.BlockSpec(shape, lambda b, j: (0,) * len(shape))
    return pl.pallas_call(
        functools.partial(_conv_mixer_kernel, alpha=alpha, mix_w=mix_w),
        out_shape=jax.ShapeDtypeStruct((t, d), F32),
        grid=(batch, nj),
        in_specs=[
            pl.BlockSpec((tm, d), lambda b, j: (b * nj + j, 0)),
            pl.BlockSpec((tm, d), lambda b, j: (jnp.minimum(b * nj + j + 1, batch * nj - 1), 0)),
            full(w_in.shape), full(dw_w.shape), full(dw_b.shape), full(cg.shape), full(cb.shape),
            pl.BlockSpec((None,) + kbd.shape[1:], lambda b, j: (b, 0, 0)),
            pl.BlockSpec((None,) + vbd.shape[1:], lambda b, j: (b, 0, 0)),
            full(wo.shape), full(g1.shape), full(b1.shape),
        ],
        out_specs=pl.BlockSpec((tm, d), lambda b, j: (b * nj + j, 0)),
        scratch_shapes=[pltpu.VMEM((w_in.shape[0], tm, IN_PROJ_CHUNK), F32),
                        pltpu.VMEM((tm, d), BF16),
                        pltpu.VMEM((CONV_HALO + tm, mix_w), F32),
                        pltpu.VMEM((SUBLANES - 1, CONV_HALO + tm, mix_w), F32),
                        pltpu.VMEM((tm, mix_w), F32)],
        compiler_params=pltpu.CompilerParams(
            dimension_semantics=("arbitrary", "arbitrary"), vmem_limit_bytes=VMEM_LIMIT),
        name="conv_mixer",
    )(x2d, x2d, w_in, dw_w, dw_b, cg, cb, kbd, vbd, wo, g1, b1)


def _moba_select_bias(gate, i):
    nb = gate.shape[1]
    blk = lax.broadcasted_iota(jnp.int32, gate.shape, 1).astype(F32)
    past = blk < i.astype(F32)
    gm = jnp.where(past, gate, NEG_INF)
    bias = jnp.full(gate.shape, NEG_INF, F32)
    for _ in range(MOBA_TOPK):
        top = jnp.max(gm, axis=1, keepdims=True)
        first = jnp.min(jnp.where(gm == top, blk, float(nb)), axis=1, keepdims=True)
        taken = blk == first
        bias = jnp.where(taken & past, 0.0, bias)
        gm = jnp.where(taken, NEG_INF, gm)
    return bias


def _moba_mixer_kernel(x_ref, xn_ref, win_ref, kbd_ref, vbd_ref, wo_ref, g1_ref, b1_ref, o_ref,
                       ustage, xn_sc, k_sc, vt_sc, kmt_sc, bias_sc, qh_sc, m_sc, acc_sc, yt_sc,
                       *, alpha, mix_w, nb):
    tm = x_ref.shape[0]
    heads = mix_w // HEAD_DIM
    n_cc = win_ref.shape[0]
    per = mix_w // IN_PROJ_CHUNK
    i = pl.program_id(1)

    @pl.when((pl.program_id(0) == 0) & (i == 0))
    def _():
        xb = x_ref[...].astype(BF16)
        for c in range(n_cc):
            ustage[c] = _dot(xb, win_ref[c])

    x = x_ref[...]
    part = lambda p: jnp.concatenate([ustage[p * per + c] for c in range(per)], axis=1)
    q = part(0) * (HEAD_DIM ** -0.5 * LOG2_E)
    k = part(1)
    v = part(2)
    qm = ustage[3 * per]
    xn_sc[...] = xn_ref[...].astype(BF16)

    def project_next(c):
        ustage[c] = _dot(xn_sc[...], win_ref[c])

    @pl.when(i == 0)
    def _():
        kmt_sc[...] = jnp.zeros(kmt_sc.shape, F32)

    k_sc[i] = k.astype(BF16)
    v_t = v.T.astype(BF16)
    for h in range(heads):
        vt_sc[i, h * HEAD_ROWS:h * HEAD_ROWS + HEAD_DIM, :] = v_t[h * HEAD_DIM:(h + 1) * HEAD_DIM, :]
        vt_sc[i, h * HEAD_ROWS + HEAD_DIM:(h + 1) * HEAD_ROWS, :] = jnp.ones((HEAD_ROWS - HEAD_DIM, tm), BF16)
    kmean = jnp.mean(k, axis=0, keepdims=True)
    lane = lax.broadcasted_iota(jnp.int32, (1, mix_w), 1)
    for h in range(heads):
        in_head = (lane >= h * HEAD_DIM) & (lane < (h + 1) * HEAD_DIM)
        kmt_sc[pl.ds(h * nb + i, 1), :] = jnp.where(in_head, kmean, 0.0)

    q_hi, q_lo = _split_bf16(q)
    km_hi, km_lo = _split_bf16(kmt_sc[...])
    gate_t = _dot_nt(km_hi, q_hi) + _dot_nt(km_hi, q_lo) + _dot_nt(km_lo, q_hi)
    bias_sc[...] = _moba_select_bias(gate_t.reshape(heads, nb, tm), i)

    lane_p = lax.broadcasted_iota(jnp.int32, (tm, 2 * HEAD_DIM), 1)
    for h in range(heads):
        qp = q_hi[:, (h // 2) * 2 * HEAD_DIM:(h // 2 + 1) * 2 * HEAD_DIM]
        keep = (lane_p < HEAD_DIM) if h % 2 == 0 else (lane_p >= HEAD_DIM)
        qh_sc[h] = jnp.where(keep, qp, jnp.zeros_like(qp))

    def scores(j, h):
        cols = slice((h // 2) * 2 * HEAD_DIM, (h // 2 + 1) * 2 * HEAD_DIM)
        return _dot_nt(k_sc[j, :, cols], qh_sc[h])

    rows = lambda h: slice(h * HEAD_ROWS, (h + 1) * HEAD_ROWS)

    kidx = lax.broadcasted_iota(jnp.int32, (tm, tm), 0)
    qidx = lax.broadcasted_iota(jnp.int32, (tm, tm), 1)
    causal = kidx <= qidx
    def heads_pipelined(j):
        pending = [scores(j, h) for h in range(MOBA_LOOKAHEAD)]
        for h in range(heads):
            if h + MOBA_LOOKAHEAD < heads:
                pending.append(scores(j, h + MOBA_LOOKAHEAD))
            yield h, pending.pop(0)

    def own_block(j, c):
        for h, s in heads_pipelined(j):
            if h < n_cc:
                project_next(h)
            s = jnp.where(causal, s, NEG_INF)
            m = jnp.max(s, axis=0, keepdims=True)
            e = jnp.exp2(s - m)
            m_sc[h] = m
            acc_sc[rows(h), :] = _dot(vt_sc[j, rows(h), :], e.astype(BF16))
        return c

    lax.fori_loop(i, i + 1, own_block, 0)
    for c in range(heads, n_cc):
        project_next(c)

    def body(j, c):
        for h, s in heads_pipelined(j):
            b = bias_sc[h, pl.ds(j, 1), :]
            m_old = m_sc[h]
            m_new = jnp.maximum(m_old, jnp.max(s, axis=0, keepdims=True) + b)
            e = jnp.exp2(s - (m_new - b))
            corr = jnp.exp2(m_old - m_new)
            m_sc[h] = m_new
            acc_sc[rows(h), :] = corr * acc_sc[rows(h), :] + _dot(vt_sc[j, rows(h), :], e.astype(BF16))
        return c

    lax.fori_loop(0, i, body, 0)
    for h in range(heads):
        r0 = h * HEAD_ROWS
        yt_sc[h * HEAD_DIM:(h + 1) * HEAD_DIM, :] = (acc_sc[r0:r0 + HEAD_DIM, :]
                                                     / acc_sc[r0 + HEAD_DIM:r0 + HEAD_DIM + 1, :])

    y_mix = yt_sc[...].T
    y_mem = _mem_attention(qm, kbd_ref[...], vbd_ref[...])
    o_ref[...] = _out_proj_ln(x, y_mix, y_mem, wo_ref, g1_ref, b1_ref, alpha, mix_w)


def _moba_mixer(x2d, batch, w_in, kbd, vbd, wo, g1, b1, alpha):
    t, d = x2d.shape
    tm = MOBA_BLOCK
    nb = t // batch // tm
    mix_w = d - MEM_WIDTH
    heads = mix_w // HEAD_DIM
    full = lambda shape: pl.BlockSpec(shape, lambda b, j: (0,) * len(shape))
    return pl.pallas_call(
        functools.partial(_moba_mixer_kernel, alpha=alpha, mix_w=mix_w, nb=nb),
        out_shape=jax.ShapeDtypeStruct((t, d), F32),
        grid=(batch, nb),
        in_specs=[
            pl.BlockSpec((tm, d), lambda b, j: (b * nb + j, 0)),
            pl.BlockSpec((tm, d), lambda b, j: (jnp.minimum(b * nb + j + 1, batch * nb - 1), 0)),
            full(w_in.shape),
            pl.BlockSpec((None,) + kbd.shape[1:], lambda b, j: (b, 0, 0)),
            pl.BlockSpec((None,) + vbd.shape[1:], lambda b, j: (b, 0, 0)),
            full(wo.shape), full(g1.shape), full(b1.shape),
        ],
        out_specs=pl.BlockSpec((tm, d), lambda b, j: (b * nb + j, 0)),
        scratch_shapes=[
            pltpu.VMEM((w_in.shape[0], tm, IN_PROJ_CHUNK), F32),
            pltpu.VMEM((tm, d), BF16),
            pltpu.VMEM((nb, tm, mix_w), BF16),
            pltpu.VMEM((nb, heads * HEAD_ROWS, tm), BF16),
            pltpu.VMEM((heads * nb, mix_w), F32),
            pltpu.VMEM((heads, nb, tm), F32),
            pltpu.VMEM((heads, tm, 2 * HEAD_DIM), BF16),
            pltpu.VMEM((heads, 1, tm), F32),
            pltpu.VMEM((heads * HEAD_ROWS, tm), F32),
            pltpu.VMEM((mix_w, tm), F32),
        ],
        compiler_params=pltpu.CompilerParams(
            dimension_semantics=("arbitrary", "arbitrary"), vmem_limit_bytes=VMEM_LIMIT),
        name="moba_mixer",
    )(x2d, x2d, w_in, kbd, vbd, wo, g1, b1)


def _router_kernel(x_ref, whi_ref, wlo_ref, br_ref, tri_ref, info_ref, sel_ref, cnt_ref, run_sc):
    step = pl.program_id(0)

    @pl.when(step == 0)
    def _():
        run_sc[...] = jnp.zeros(run_sc.shape, F32)

    x_hi, x_lo = _split_bf16(x_ref[...])
    logits = (_dot_nt(whi_ref[...], x_hi) + _dot_nt(wlo_ref[...], x_hi) + _dot_nt(whi_ref[...], x_lo)
              + br_ref[...])[0:ROUTER_ROWS, :]
    row = lax.broadcasted_iota(jnp.int32, logits.shape, 0).astype(F32)
    cmax = lambda a: jnp.max(a, axis=0, keepdims=True)
    cmin = lambda a: jnp.min(a, axis=0, keepdims=True)
    csum = lambda a: jnp.sum(a, axis=0, keepdims=True)
    big = float(2 * ROUTER_ROWS)

    is_g = row < N_GROUPS
    gl = jnp.where(is_g, logits, NEG_INF)
    gmax = cmax(gl)
    gidx = cmin(jnp.where(gl == gmax, row, big))
    g_w = 1.0 / csum(jnp.where(is_g, jnp.exp(gl - gmax), 0.0))

    lo = ROUTER_OFF + EXPERTS_PER_GROUP * gidx
    el = jnp.where((row >= lo) & (row < lo + EXPERTS_PER_GROUP), logits, NEG_INF)
    v0 = cmax(el)
    i0 = cmin(jnp.where(el == v0, row, big))
    el1 = jnp.where(row == i0, NEG_INF, el)
    v1 = cmax(el1)
    i1 = cmin(jnp.where(el1 == v1, row, big))
    t = jnp.exp(v1 - v0)
    w0 = g_w / (1.0 + t)
    w1 = g_w * t / (1.0 + t)

    pick0 = row == i0
    pick1 = row == i1
    onehot = jnp.where(pick0 | pick1, 1.0, 0.0)
    before = _dot(onehot.astype(BF16), tri_ref[...]) + run_sc[...]
    pos0 = csum(jnp.where(pick0, before, 0.0))
    pos1 = csum(jnp.where(pick1, before, 0.0))
    run_sc[...] = run_sc[...] + jnp.sum(onehot, axis=1, keepdims=True)
    cnt_ref[...] = run_sc[...]

    vals = (i0 - ROUTER_OFF, i1 - ROUTER_OFF, pos0, pos1, w0, w1)
    row8 = lax.broadcasted_iota(jnp.int32, sel_ref.shape, 0)
    sel = jnp.zeros(sel_ref.shape, F32)
    for c, val in enumerate(vals):
        sel = jnp.where(row8 == c, val, sel)
    sel_ref[...] = sel
    rowl = lax.broadcasted_iota(jnp.int32, (ROUTER_LANES, x_ref.shape[0]), 0)
    info_ref[...] = jnp.where(rowl == 4, w0, jnp.where(rowl == 5, w1, 0.0)).T


def _router(x2d, whi, wlo, br, tri):
    t, d = x2d.shape
    tm = TOKEN_TILE
    full = lambda shape: pl.BlockSpec(shape, lambda s: (0,) * len(shape))
    return pl.pallas_call(
        _router_kernel,
        out_shape=(jax.ShapeDtypeStruct((t, ROUTER_LANES), F32),
                   jax.ShapeDtypeStruct((t // tm, SUBLANES, tm), F32),
                   jax.ShapeDtypeStruct((ROUTER_ROWS, tm), F32)),
        grid=(t // tm,),
        in_specs=[pl.BlockSpec((tm, d), lambda s: (s, 0)),
                  full(whi.shape), full(wlo.shape), full(br.shape), full(tri.shape)],
        out_specs=(pl.BlockSpec((tm, ROUTER_LANES), lambda s: (s, 0)),
                   pl.BlockSpec((None, SUBLANES, tm), lambda s: (s, 0, 0)),
                   full((ROUTER_ROWS, tm))),
        scratch_shapes=[pltpu.VMEM((ROUTER_ROWS, tm), F32)],
        compiler_params=pltpu.CompilerParams(dimension_semantics=("arbitrary",)),
        name="router",
    )(x2d, whi, wlo, br, tri)


def _row_copy(src, src_row, dst, dst_row, sem):
    return pltpu.make_async_copy(src.at[pl.ds(src_row, 1), :], dst.at[pl.ds(dst_row, 1), :], sem)


def _slots_kernel(sel_ref, base_ref, o_ref):
    sel = sel_ref[...]
    g, _, tm = sel.shape
    row = lax.broadcasted_iota(jnp.int32, (g, ROUTER_ROWS, tm), 1).astype(F32)
    base = base_ref[...][None]
    csum = lambda a: jnp.sum(a, axis=1, keepdims=True)
    s0 = csum(jnp.where(row == sel[:, 0:1, :] + ROUTER_OFF, base, 0.0)) + sel[:, 2:3, :]
    s1 = csum(jnp.where(row == sel[:, 1:2, :] + ROUTER_OFF, base, 0.0)) + sel[:, 3:4, :]
    row8 = lax.broadcasted_iota(jnp.int32, o_ref.shape, 1)
    o_ref[...] = jnp.where(row8 == 0, s0, jnp.where(row8 == 1, s1, 0.0)).astype(jnp.int32)


def _slots(sel, base_col):
    nt, _, tm = sel.shape
    g = SLOT_TILES_PER_STEP if nt % SLOT_TILES_PER_STEP == 0 else nt
    return pl.pallas_call(
        _slots_kernel,
        out_shape=jax.ShapeDtypeStruct((nt, SUBLANES, tm), jnp.int32),
        grid=(nt // g,),
        in_specs=[pl.BlockSpec((g, SUBLANES, tm), lambda s: (s, 0, 0)),
                  pl.BlockSpec(base_col.shape, lambda s: (0, 0))],
        out_specs=pl.BlockSpec((g, SUBLANES, tm), lambda s: (s, 0, 0)),
        compiler_params=pltpu.CompilerParams(dimension_semantics=("arbitrary",)),
        name="slots",
    )(sel, base_col)


def _dispatch_kernel(nv_ref, s0_ref, s1_ref, x_ref, xs_ref, zero_sc, sem, fill_sem, *, tm):

    @pl.when(pl.program_id(0) == 0)
    def _():
        zero_sc[...] = jnp.zeros(zero_sc.shape, F32)

        def fill(t):
            return pltpu.make_async_copy(zero_sc, xs_ref.at[pl.ds(pl.multiple_of(t * tm, tm), tm), :], fill_sem)

        def start_fill(t, c):
            @pl.when(nv_ref[t] < tm)
            def _():
                fill(t).start()
            return c

        def wait_fill(t, c):
            @pl.when(nv_ref[t] < tm)
            def _():
                fill(t).wait()
            return c

        lax.fori_loop(0, nv_ref.shape[0], start_fill, 0)
        lax.fori_loop(0, nv_ref.shape[0], wait_fill, 0)

    rows = x_ref.shape[0]
    for g in range(s0_ref.shape[0]):
        def start(r, c, g=g):
            row = g * s0_ref.shape[2] + r
            _row_copy(x_ref, row, xs_ref, s0_ref[g, 0, r], sem).start(priority=0)
            _row_copy(x_ref, row, xs_ref, s1_ref[g, 0, r], sem).start(priority=1)
            return c

        lax.fori_loop(0, s0_ref.shape[2], start, 0, unroll=DMA_UNROLL)
    for _ in range(2):
        pltpu.make_async_copy(x_ref, xs_ref.at[pl.ds(0, rows), :], sem).wait()


def _dispatch(x2d, slot0, slot1, tile_valid):
    t, d = x2d.shape
    tm = TOKEN_TILE
    rows = ROW_TILE if t % ROW_TILE == 0 else tm
    n_slots = tile_valid.shape[0] * tm
    smem_rows = pl.BlockSpec((rows // tm, 1, tm), lambda s, nv: (s, 0, 0), memory_space=pltpu.SMEM)
    return pl.pallas_call(
        functools.partial(_dispatch_kernel, tm=tm),
        out_shape=jax.ShapeDtypeStruct((n_slots, d), F32),
        grid_spec=pltpu.PrefetchScalarGridSpec(
            num_scalar_prefetch=1,
            grid=(t // rows,),
            in_specs=[smem_rows, smem_rows, pl.BlockSpec((rows, d), lambda s, nv: (s, 0))],
            out_specs=pl.BlockSpec(memory_space=pl.ANY),
            scratch_shapes=[pltpu.VMEM((tm, d), F32), pltpu.SemaphoreType.DMA, pltpu.SemaphoreType.DMA],
        ),
        compiler_params=pltpu.CompilerParams(dimension_semantics=("arbitrary",)),
        name="dispatch",
    )(tile_valid, slot0, slot1, x2d)


def _expert_kernel(te_ref, nv_ref, first_ref, buf_ref, next_ref, xs_hbm, wg_hbm, wu_hbm, wd_hbm, ys_ref,
                   xs_buf, wg_buf, wu_buf, wd_buf, wg_sc, wu_sc, wd_sc, xs_sems, sems, *, base):
    t = pl.program_id(0)
    n_steps = pl.num_programs(0)
    tm = ys_ref.shape[0]
    used = nv_ref[t] > 0

    def xs_copy(tile):
        row0 = tile * tm if isinstance(tile, int) else pl.multiple_of(tile * tm, tm)
        slot = tile % XS_RING
        return pltpu.make_async_copy(xs_hbm.at[pl.ds(row0, tm), :], xs_buf.at[slot], xs_sems.at[slot])

    @pl.when(t == 0)
    def _():
        for ahead in range(XS_RING - 1):
            @pl.when(ahead < n_steps)
            def _():
                xs_copy(ahead).start()

    @pl.when(t + XS_RING - 1 < n_steps)
    def _():
        xs_copy(t + XS_RING - 1).start()

    xs_copy(t).wait()

    def weight_copies(e, b):
        return (pltpu.make_async_copy(wg_hbm.at[base + e], wg_buf.at[b], sems.at[b, 0]),
                pltpu.make_async_copy(wu_hbm.at[base + e], wu_buf.at[b], sems.at[b, 1]),
                pltpu.make_async_copy(wd_hbm.at[base + e], wd_buf.at[b], sems.at[b, 2]))

    @pl.when(t == 0)
    def _():
        for c in weight_copies(te_ref[0], 0):
            c.start()

    @pl.when(first_ref[t] == 1)
    def _():
        b = buf_ref[t]
        for c in weight_copies(te_ref[t], b):
            c.wait()
        wg_sc[...] = wg_buf[b].astype(BF16)
        wu_sc[...] = wu_buf[b].astype(BF16)
        wd_sc[...] = wd_buf[b].astype(BF16)

        @pl.when(next_ref[t] >= 0)
        def _():
            for c in weight_copies(next_ref[t], 1 - b):
                c.start()

    @pl.when(used)
    def _():
        xb = xs_buf[t % XS_RING].astype(BF16)
        hg = _dot(xb, wg_sc[...])
        hu = _dot(xb, wu_sc[...])
        h = hg * jax.nn.sigmoid(hg) * hu
        ys_ref[...] = _dot(h.astype(BF16), wd_sc[...])

    @pl.when(jnp.logical_not(used))
    def _():
        ys_ref[...] = jnp.zeros(ys_ref.shape, F32)


def _expert_mlp(xs, tile_expert, tile_valid, tile_first, tile_buf, tile_next, w_gate, w_up, w_down, layer):
    ns, d = xs.shape
    f = w_gate.shape[-1]
    tm = TOKEN_TILE
    hbm = pl.BlockSpec(memory_space=pl.ANY)
    return pl.pallas_call(
        functools.partial(_expert_kernel, base=layer * N_EXPERTS),
        out_shape=jax.ShapeDtypeStruct((ns, d), F32),
        grid_spec=pltpu.PrefetchScalarGridSpec(
            num_scalar_prefetch=5,
            grid=(ns // tm,),
            in_specs=[hbm, hbm, hbm, hbm],
            out_specs=pl.BlockSpec((tm, d), lambda t, *_: (t, 0)),
            scratch_shapes=[pltpu.VMEM((XS_RING, tm, d), F32),
                            pltpu.VMEM((2, d, f), F32), pltpu.VMEM((2, d, f), F32), pltpu.VMEM((2, f, d), F32),
                            pltpu.VMEM((d, f), BF16), pltpu.VMEM((d, f), BF16), pltpu.VMEM((f, d), BF16),
                            pltpu.SemaphoreType.DMA((XS_RING,)), pltpu.SemaphoreType.DMA((2, 3))],
        ),
        compiler_params=pltpu.CompilerParams(
            dimension_semantics=("arbitrary",), vmem_limit_bytes=VMEM_LIMIT),
        name="expert_mlp",
    )(tile_expert, tile_valid, tile_first, tile_buf, tile_next, xs, w_gate, w_up, w_down)


def _combine_kernel(s0_ref, s1_ref, n0_ref, n1_ref, info_ref, x_ref, ys_ref, g_ref, b_ref, o_ref,
                    y0_sc, y1_sc, sems, *, alpha):
    s = pl.program_id(0)
    rows = x_ref.shape[0]

    def start_gathers(t0_ref, t1_ref, buf):
        for g in range(t0_ref.shape[0]):
            def start(r, c, g=g):
                row = g * t0_ref.shape[2] + r
                _row_copy(ys_ref, t0_ref[g, 0, r], y0_sc.at[buf], row, sems.at[buf]).start(priority=0)
                _row_copy(ys_ref, t1_ref[g, 0, r], y1_sc.at[buf], row, sems.at[buf]).start(priority=1)
                return c

            lax.fori_loop(0, t0_ref.shape[2], start, 0, unroll=DMA_UNROLL)

    cur = s % 2

    @pl.when(s == 0)
    def _():
        start_gathers(s0_ref, s1_ref, 0)

    @pl.when(s + 1 < pl.num_programs(0))
    def _():
        start_gathers(n0_ref, n1_ref, 1 - cur)

    for dst in (y0_sc, y1_sc):
        pltpu.make_async_copy(ys_ref.at[pl.ds(0, rows), :], dst.at[cur], sems.at[cur]).wait()
    info = info_ref[...]
    f = info[:, 4:5] * y0_sc[cur] + info[:, 5:6] * y1_sc[cur]
    o_ref[...] = _ln(alpha * x_ref[...] + f, g_ref[...], b_ref[...])


def _combine(x2d, info, ys, slot0, slot1, g2, b2, alpha):
    t, d = x2d.shape
    tm = TOKEN_TILE
    rows = ROW_TILE if t % ROW_TILE == 0 else tm
    n = t // rows
    smem_rows = pl.BlockSpec((rows // tm, 1, tm), lambda s: (s, 0, 0), memory_space=pltpu.SMEM)
    smem_next = pl.BlockSpec((rows // tm, 1, tm), lambda s: (jnp.minimum(s + 1, n - 1), 0, 0),
                             memory_space=pltpu.SMEM)
    full = lambda shape: pl.BlockSpec(shape, lambda s: (0,) * len(shape))
    return pl.pallas_call(
        functools.partial(_combine_kernel, alpha=alpha),
        out_shape=jax.ShapeDtypeStruct((t, d), F32),
        grid=(n,),
        in_specs=[smem_rows, smem_rows, smem_next, smem_next,
                  pl.BlockSpec((rows, ROUTER_LANES), lambda s: (s, 0)),
                  pl.BlockSpec((rows, d), lambda s: (s, 0)),
                  pl.BlockSpec(memory_space=pl.ANY),
                  full(g2.shape), full(b2.shape)],
        out_specs=pl.BlockSpec((rows, d), lambda s: (s, 0)),
        scratch_shapes=[pltpu.VMEM((2, rows, d), F32), pltpu.VMEM((2, rows, d), F32),
                        pltpu.SemaphoreType.DMA((2,))],
        compiler_params=pltpu.CompilerParams(
            dimension_semantics=("arbitrary",), vmem_limit_bytes=VMEM_LIMIT),
        name="combine",
    )(slot0, slot1, slot0, slot1, info, x2d, ys, g2, b2)


def _moe(x2d, layer, w_rg, b_rg, w_re, b_re, w_gate, w_up, w_down, g2, b2, alpha):
    t, d = x2d.shape
    tm = TOKEN_TILE
    wr = jnp.zeros((ROUTER_LANES, d), F32).at[0:N_GROUPS].set(w_rg.T)
    wr = wr.at[ROUTER_OFF:ROUTER_OFF + N_EXPERTS].set(w_re.T)
    br = jnp.zeros((ROUTER_LANES, 1), F32).at[0:N_GROUPS, 0].set(b_rg)
    br = br.at[ROUTER_OFF:ROUTER_OFF + N_EXPERTS, 0].set(b_re)
    whi, wlo = _split_bf16(wr)
    ridx = lax.broadcasted_iota(jnp.int32, (tm, tm), 0)
    cidx = lax.broadcasted_iota(jnp.int32, (tm, tm), 1)
    tri = (ridx < cidx).astype(BF16)
    info, sel, cnt = _router(x2d, whi, wlo, br, tri)

    counts = cnt[ROUTER_OFF:ROUTER_OFF + N_EXPERTS, 0].astype(jnp.int32)
    padded = (counts + tm - 1) // tm * tm
    ends = jnp.cumsum(padded)
    base = ends - padded
    base_col = jnp.zeros((ROUTER_ROWS, 1), F32).at[ROUTER_OFF:ROUTER_OFF + N_EXPERTS, 0].set(base.astype(F32))
    slots = _slots(sel, base_col)
    slot0, slot1 = slots[:, 0:1, :], slots[:, 1:2, :]

    n_slots = 2 * t + N_EXPERTS * tm
    n_tiles = n_slots // tm
    n_used = ends[-1] // tm
    tile_ids = jnp.arange(n_tiles, dtype=jnp.int32)
    tile_expert = jnp.sum((jnp.minimum(tile_ids, n_used - 1)[:, None] * tm >= ends[None, :]).astype(jnp.int32),
                          axis=1)
    lo = jnp.maximum(base[None, :], tile_ids[:, None] * tm)
    hi = jnp.minimum((base + counts)[None, :], (tile_ids[:, None] + 1) * tm)
    tile_valid = jnp.sum(jnp.maximum(hi - lo, 0), axis=1).astype(jnp.int32)

    used = tile_valid > 0
    tile_first = (used & ((tile_ids == 0) | (tile_expert != jnp.roll(tile_expert, 1)))).astype(jnp.int32)
    tile_buf = ((jnp.cumsum(tile_first) - 1) % 2).astype(jnp.int32)
    experts = jnp.arange(N_EXPERTS, dtype=jnp.int32)
    later_nonempty = (experts[None, :] > experts[:, None]) & (counts[None, :] > 0)
    next_expert = jnp.min(jnp.where(later_nonempty, experts[None, :], N_EXPERTS), axis=1)
    next_expert = jnp.where(next_expert < N_EXPERTS, next_expert, -1)
    tile_next = jnp.sum(jnp.where(tile_expert[:, None] == experts[None, :], next_expert[None, :], 0),
                        axis=1).astype(jnp.int32)

    xs = _dispatch(x2d, slot0, slot1, tile_valid)
    ys = _expert_mlp(xs, tile_expert, tile_valid, tile_first, tile_buf, tile_next, w_gate, w_up, w_down, layer)
    return _combine(x2d, info, ys, slot0, slot1, g2, b2, alpha)


def kernel(x, mem, w_mem_kv, conv_w_in, conv_dw_w, conv_dw_b, conv_ln_g, conv_ln_b, moba_w_in, w_o,
           ln1_g, ln1_b, w_rg, b_rg, w_re, b_re, w_gate, w_up, w_down, ln2_g, ln2_b):
    batch, seq, d = x.shape
    depth = w_o.shape[0]
    alpha = (2 * depth) ** 0.25
    t = batch * seq
    row = lambda a: a.reshape(1, -1)

    kbd, vbd = _memkv(mem, w_mem_kv)

    e_shape = w_gate.shape
    w_gate = w_gate.reshape((-1,) + e_shape[-2:])
    w_up = w_up.reshape((-1,) + e_shape[-2:])
    w_down = w_down.reshape((-1,) + w_down.shape[-2:])

    x2d = x.reshape(t, d)
    for i in range(depth):
        j = i // 2
        wo = w_o[i].astype(BF16)
        if i % 2 == 0:
            dw_w = jnp.broadcast_to(conv_dw_w[j][:, None, :], (CONV_WIDTH, SUBLANES, conv_dw_w.shape[-1]))
            w_in = conv_w_in[j].astype(BF16).reshape(d, -1, IN_PROJ_CHUNK).transpose(1, 0, 2)
            x2d = _conv_mixer(x2d, batch, w_in, dw_w, row(conv_dw_b[j]),
                              row(conv_ln_g[j]), row(conv_ln_b[j]), kbd, vbd, wo,
                              row(ln1_g[i]), row(ln1_b[i]), alpha)
        else:
            w_in = moba_w_in[j].astype(BF16).reshape(d, -1, IN_PROJ_CHUNK).transpose(1, 0, 2)
            x2d = _moba_mixer(x2d, batch, w_in, kbd, vbd, wo,
                              row(ln1_g[i]), row(ln1_b[i]), alpha)
        x2d = _moe(x2d, i, w_rg[i], b_rg[i], w_re[i], b_re[i], w_gate, w_up, w_down,
                   row(ln2_g[i]), row(ln2_b[i]), alpha)
    return x2d.reshape(batch, seq, d)
```

```python
import functools

import jax
import jax.numpy as jnp
from jax import lax
from jax.experimental import pallas as pl
from jax.experimental.pallas import tpu as pltpu

F32 = jnp.float32
BF16 = jnp.bfloat16

HEAD_DIM = 64
MEM_LEN = 256
MEM_HEADS = 4
MEM_WIDTH = MEM_HEADS * HEAD_DIM
CONV_WIDTH = 31
MOBA_BLOCK = 256
MOBA_TOPK = 3
HEAD_ROWS = HEAD_DIM + 16
MOBA_LOOKAHEAD = 6
N_GROUPS = 4
EXPERTS_PER_GROUP = 8
N_EXPERTS = N_GROUPS * EXPERTS_PER_GROUP
LN_EPS = 1e-5

LANES = 128
SUBLANES = 8
TOKEN_TILE = 256
CONV_HALO = 32
CONV_CHUNK = 16
CONV_CHUNKS_PER_STEP = 2
IN_PROJ_CHUNK = 256
ROUTER_LANES = 128
ROUTER_ROWS = 40
ROUTER_OFF = N_GROUPS
DMA_UNROLL = 8
ROUTER_TILE = 512
DISPATCH_ROWS = 1024
COMBINE_ROWS = 512
XS_RING = 3
SLOT_TILES_PER_STEP = 16
VMEM_LIMIT = 56 * 1024 * 1024

NEG_INF = float("-inf")
LOG2_E = 1.4426950408889634


def _ln(z, g, b):
    mu = jnp.mean(z, axis=-1, keepdims=True)
    zc = z - mu
    var = jnp.mean(zc * zc, axis=-1, keepdims=True)
    return zc * lax.rsqrt(var + LN_EPS) * g + b


def _dot(a, b):
    return jnp.dot(a, b, preferred_element_type=F32)


def _dot_nt(a, b):
    return lax.dot_general(a, b, (((1,), (1,)), ((), ())), preferred_element_type=F32)


def _split_bf16(x):
    hi = x.astype(BF16)
    lo = (x - hi.astype(F32)).astype(BF16)
    return hi, lo


def _memkv_kernel(mem_ref, w_ref, kbd_ref, vbd_ref):
    kv = _dot(mem_ref[...].astype(BF16), w_ref[...].astype(BF16))
    k_t = kv[:, 0:MEM_WIDTH].T
    v = kv[:, MEM_WIDTH:]
    r = lax.broadcasted_iota(jnp.int32, kbd_ref.shape, 0)
    c = lax.broadcasted_iota(jnp.int32, kbd_ref.shape, 1)
    kbd_ref[...] = jnp.where(r // HEAD_DIM == c // MEM_LEN,
                             jnp.concatenate([k_t] * MEM_HEADS, axis=1), 0.0).astype(BF16)
    r = lax.broadcasted_iota(jnp.int32, vbd_ref.shape, 0)
    c = lax.broadcasted_iota(jnp.int32, vbd_ref.shape, 1)
    vbd_ref[...] = jnp.where(r // MEM_LEN == c // HEAD_DIM,
                             jnp.concatenate([v] * MEM_HEADS, axis=0), 0.0).astype(BF16)


def _memkv(mem, w):
    batch, m, d = mem.shape
    return pl.pallas_call(
        _memkv_kernel,
        out_shape=(jax.ShapeDtypeStruct((batch, MEM_WIDTH, MEM_HEADS * m), BF16),
                   jax.ShapeDtypeStruct((batch, MEM_HEADS * m, MEM_WIDTH), BF16)),
        grid=(batch,),
        in_specs=[pl.BlockSpec((None, m, d), lambda b: (b, 0, 0)),
                  pl.BlockSpec(w.shape, lambda b: (0, 0))],
        out_specs=(pl.BlockSpec((None, MEM_WIDTH, MEM_HEADS * m), lambda b: (b, 0, 0)),
                   pl.BlockSpec((None, MEM_HEADS * m, MEM_WIDTH), lambda b: (b, 0, 0))),
        compiler_params=pltpu.CompilerParams(dimension_semantics=("arbitrary",)),
        name="memkv",
    )(mem, w)


def _mem_attention(qm, kbd, vbd):
    s = _dot(qm.astype(BF16), kbd) * (HEAD_DIM ** -0.5)
    parts = []
    for h in range(MEM_HEADS):
        seg = s[:, h * MEM_LEN:(h + 1) * MEM_LEN]
        m = jnp.max(seg, axis=-1, keepdims=True)
        e = jnp.exp(seg - m)
        parts.append(e / jnp.sum(e, axis=-1, keepdims=True))
    p = jnp.concatenate(parts, axis=-1)
    return _dot(p.astype(BF16), vbd)


def _out_proj_ln(x, y_mix, y_mem, wo_ref, g_ref, b_ref, alpha, mix_w):
    y = _dot(y_mix.astype(BF16), wo_ref[0:mix_w, :]) + _dot(y_mem.astype(BF16), wo_ref[mix_w:, :])
    return _ln(alpha * x + y, g_ref[...], b_ref[...])


def _conv_mixer_kernel(x_ref, xn_ref, win_ref, dww_ref, dwb_ref, cg_ref, cb_ref, kbd_ref, vbd_ref,
                       wo_ref, g1_ref, b1_ref, o_ref, ustage, xn_sc, hbuf, zbuf, cbuf, *, alpha, mix_w):
    tm = x_ref.shape[0]
    n_cc = win_ref.shape[0]
    per = mix_w // IN_PROJ_CHUNK
    j = pl.program_id(1)

    @pl.when((pl.program_id(0) == 0) & (j == 0))
    def _():
        xb = x_ref[...].astype(BF16)
        for c in range(n_cc):
            ustage[c] = _dot(xb, win_ref[c])

    x = x_ref[...]
    qm = ustage[2 * per]
    h = jnp.concatenate([ustage[c] * jax.nn.sigmoid(ustage[per + c]) for c in range(per)], axis=1)
    xn_sc[...] = xn_ref[...].astype(BF16)

    @pl.when(j == 0)
    def _():
        hbuf[0:CONV_HALO, :] = jnp.zeros((CONV_HALO, mix_w), F32)

    hbuf[CONV_HALO:CONV_HALO + tm, :] = h

    shifted_rows = CONV_HALO + tm - SUBLANES
    for a in range(1, SUBLANES):
        zbuf[a - 1, 0:shifted_rows, :] = hbuf[a:a + shifted_rows, :]

    first = CONV_HALO - (CONV_WIDTH - 1)
    aligned = lambda v, m: v if isinstance(v, int) else pl.multiple_of(v, m)

    def conv_rows(r0):
        acc = jnp.broadcast_to(dwb_ref[...], (CONV_CHUNK, mix_w))
        for a in range(SUBLANES):
            offs = [first + k - a for k in range(CONV_WIDTH) if (first + k) % SUBLANES == a]
            r = aligned(r0 + offs[0], SUBLANES)
            span = offs[-1] - offs[0] + CONV_CHUNK
            z = hbuf[pl.ds(r, span), :] if a == 0 else zbuf[a - 1, pl.ds(r, span), :]
            for off in offs:
                k = off + a - first
                rows = z[off - offs[0]:off - offs[0] + CONV_CHUNK, :]
                acc = acc + jnp.tile(dww_ref[k], (CONV_CHUNK // SUBLANES, 1)) * rows
        cbuf[pl.ds(r0, CONV_CHUNK), :] = acc

    def project_next(c):
        ustage[c] = _dot(xn_sc[...], win_ref[c])

    step_rows = CONV_CHUNK * CONV_CHUNKS_PER_STEP
    paired = min(n_cc, tm // step_rows)

    def paired_step(c, carry):
        project_next(c)
        for q in range(CONV_CHUNKS_PER_STEP):
            conv_rows(pl.multiple_of(c * step_rows + q * CONV_CHUNK, CONV_CHUNK))
        return carry

    lax.fori_loop(0, paired, paired_step, 0)
    for r0 in range(paired * step_rows, tm, CONV_CHUNK):
        conv_rows(r0)
    for c in range(paired, n_cc):
        project_next(c)

    hbuf[0:CONV_HALO, :] = hbuf[tm:tm + CONV_HALO, :]

    cn = _ln(cbuf[...], cg_ref[...], cb_ref[...])
    y_mix = cn * jax.nn.sigmoid(cn)
    y_mem = _mem_attention(qm, kbd_ref[...], vbd_ref[...])
    o_ref[...] = _out_proj_ln(x, y_mix, y_mem, wo_ref, g1_ref, b1_ref, alpha, mix_w)


def _conv_mixer(x2d, batch, w_in, dw_w, dw_b, cg, cb, kbd, vbd, wo, g1, b1, alpha):
    t, d = x2d.shape
    tm = TOKEN_TILE
    nj = t // batch // tm
    mix_w = d - MEM_WIDTH
    full = lambda shape: pl.BlockSpec(shape, lambda b, j: (0,) * len(shape))
    return pl.pallas_call(
        functools.partial(_conv_mixer_kernel, alpha=alpha, mix_w=mix_w),
        out_shape=jax.ShapeDtypeStruct((t, d), F32),
        grid=(batch, nj),
        in_specs=[
            pl.BlockSpec((tm, d), lambda b, j: (b * nj + j, 0)),
            pl.BlockSpec((tm, d), lambda b, j: (jnp.minimum(b * nj + j + 1, batch * nj - 1), 0)),
            full(w_in.shape), full(dw_w.shape), full(dw_b.shape), full(cg.shape), full(cb.shape),
            pl.BlockSpec((None,) + kbd.shape[1:], lambda b, j: (b, 0, 0)),
            pl.BlockSpec((None,) + vbd.shape[1:], lambda b, j: (b, 0, 0)),
            full(wo.shape), full(g1.shape), full(b1.shape),
        ],
        out_specs=pl.BlockSpec((tm, d), lambda b, j: (b * nj + j, 0)),
        scratch_shapes=[pltpu.VMEM((w_in.shape[0], tm, IN_PROJ_CHUNK), F32),
                        pltpu.VMEM((tm, d), BF16),
                        pltpu.VMEM((CONV_HALO + tm, mix_w), F32),
                        pltpu.VMEM((SUBLANES - 1, CONV_HALO + tm, mix_w), F32),
                        pltpu.VMEM((tm, mix_w), F32)],
        compiler_params=pltpu.CompilerParams(
            dimension_semantics=("arbitrary", "arbitrary"), vmem_limit_bytes=VMEM_LIMIT),
        name="conv_mixer",
    )(x2d, x2d, w_in, dw_w, dw_b, cg, cb, kbd, vbd, wo, g1, b1)


def _moba_select_bias(gate, i):
    nb = gate.shape[1]
    blk = lax.broadcasted_iota(jnp.int32, gate.shape, 1).astype(F32)
    past = blk < i.astype(F32)
    gm = jnp.where(past, gate, NEG_INF)
    bias = jnp.full(gate.shape, NEG_INF, F32)
    for _ in range(MOBA_TOPK):
        top = jnp.max(gm, axis=1, keepdims=True)
        first = jnp.min(jnp.where(gm == top, blk, float(nb)), axis=1, keepdims=True)
        taken = blk == first
        bias = jnp.where(taken & past, 0.0, bias)
        gm = jnp.where(taken, NEG_INF, gm)
    return bias


def _moba_mixer_kernel(x_ref, xn_ref, win_ref, kbd_ref, vbd_ref, wo_ref, g1_ref, b1_ref, o_ref,
                       ustage, xn_sc, k_sc, vt_sc, kmt_sc, bias_sc, qh_sc, m_sc, acc_sc, yt_sc,
                       *, alpha, mix_w, nb):
    tm = x_ref.shape[0]
    heads = mix_w // HEAD_DIM
    n_cc = win_ref.shape[0]
    per = mix_w // IN_PROJ_CHUNK
    i = pl.program_id(1)

    @pl.when((pl.program_id(0) == 0) & (i == 0))
    def _():
        xb = x_ref[...].astype(BF16)
        for c in range(n_cc):
            ustage[c] = _dot(xb, win_ref[c])

    x = x_ref[...]
    part = lambda p: jnp.concatenate([ustage[p * per + c] for c in range(per)], axis=1)
    q = part(0) * (HEAD_DIM ** -0.5 * LOG2_E)
    k = part(1)
    v = part(2)
    qm = ustage[3 * per]
    xn_sc[...] = xn_ref[...].astype(BF16)

    def project_next(c):
        ustage[c] = _dot(xn_sc[...], win_ref[c])

    @pl.when(i == 0)
    def _():
        kmt_sc[...] = jnp.zeros(kmt_sc.shape, F32)

    k_sc[i] = k.astype(BF16)
    v_t = v.T.astype(BF16)
    for h in range(heads):
        vt_sc[i, h * HEAD_ROWS:h * HEAD_ROWS + HEAD_DIM, :] = v_t[h * HEAD_DIM:(h + 1) * HEAD_DIM, :]
        vt_sc[i, h * HEAD_ROWS + HEAD_DIM:(h + 1) * HEAD_ROWS, :] = jnp.ones((HEAD_ROWS - HEAD_DIM, tm), BF16)
    kmean = jnp.mean(k, axis=0, keepdims=True)
    lane = lax.broadcasted_iota(jnp.int32, (1, mix_w), 1)
    for h in range(heads):
        in_head = (lane >= h * HEAD_DIM) & (lane < (h + 1) * HEAD_DIM)
        kmt_sc[pl.ds(h * nb + i, 1), :] = jnp.where(in_head, kmean, 0.0)

    q_hi, q_lo = _split_bf16(q)
    km_hi, km_lo = _split_bf16(kmt_sc[...])
    gate_t = _dot_nt(km_hi, q_hi) + _dot_nt(km_hi, q_lo) + _dot_nt(km_lo, q_hi)
    bias_sc[...] = _moba_select_bias(gate_t.reshape(heads, nb, tm), i)

    lane_p = lax.broadcasted_iota(jnp.int32, (tm, 2 * HEAD_DIM), 1)
    for h in range(heads):
        qp = q_hi[:, (h // 2) * 2 * HEAD_DIM:(h // 2 + 1) * 2 * HEAD_DIM]
        keep = (lane_p < HEAD_DIM) if h % 2 == 0 else (lane_p >= HEAD_DIM)
        qh_sc[h] = jnp.where(keep, qp, jnp.zeros_like(qp))

    def scores(j, h):
        cols = slice((h // 2) * 2 * HEAD_DIM, (h // 2 + 1) * 2 * HEAD_DIM)
        return _dot_nt(k_sc[j, :, cols], qh_sc[h])

    rows = lambda h: slice(h * HEAD_ROWS, (h + 1) * HEAD_ROWS)

    kidx = lax.broadcasted_iota(jnp.int32, (tm, tm), 0)
    qidx = lax.broadcasted_iota(jnp.int32, (tm, tm), 1)
    causal = kidx <= qidx
    def heads_pipelined(j):
        pending = [scores(j, h) for h in range(MOBA_LOOKAHEAD)]
        for h in range(heads):
            if h + MOBA_LOOKAHEAD < heads:
                pending.append(scores(j, h + MOBA_LOOKAHEAD))
            yield h, pending.pop(0)

    def own_block(j, c):
        for h, s in heads_pipelined(j):
            if h < n_cc:
                project_next(h)
            s = jnp.where(causal, s, NEG_INF)
            m = jnp.max(s, axis=0, keepdims=True)
            e = jnp.exp2(s - m)
            m_sc[h] = m
            acc_sc[rows(h), :] = _dot(vt_sc[j, rows(h), :], e.astype(BF16))
        return c

    lax.fori_loop(i, i + 1, own_block, 0)
    for c in range(heads, n_cc):
        project_next(c)

    def body(j, c):
        for h, s in heads_pipelined(j):
            b = bias_sc[h, pl.ds(j, 1), :]
            m_old = m_sc[h]
            m_new = jnp.maximum(m_old, jnp.max(s, axis=0, keepdims=True) + b)
            e = jnp.exp2(s - (m_new - b))
            corr = jnp.exp2(m_old - m_new)
            m_sc[h] = m_new
            acc_sc[rows(h), :] = corr * acc_sc[rows(h), :] + _dot(vt_sc[j, rows(h), :], e.astype(BF16))
        return c

    lax.fori_loop(0, i, body, 0)
    for h in range(heads):
        r0 = h * HEAD_ROWS
        yt_sc[h * HEAD_DIM:(h + 1) * HEAD_DIM, :] = (acc_sc[r0:r0 + HEAD_DIM, :]
                                                     / acc_sc[r0 + HEAD_DIM:r0 + HEAD_DIM + 1, :])

    y_mix = yt_sc[...].T
    y_mem = _mem_attention(qm, kbd_ref[...], vbd_ref[...])
    o_ref[...] = _out_proj_ln(x, y_mix, y_mem, wo_ref, g1_ref, b1_ref, alpha, mix_w)


def _moba_mixer(x2d, batch, w_in, kbd, vbd, wo, g1, b1, alpha):
    t, d = x2d.shape
    tm = MOBA_BLOCK
    nb = t // batch // tm
    mix_w = d - MEM_WIDTH
    heads = mix_w // HEAD_DIM
    full = lambda shape: pl.BlockSpec(shape, lambda b, j: (0,) * len(shape))
    return pl.pallas_call(
        functools.partial(_moba_mixer_kernel, alpha=alpha, mix_w=mix_w, nb=nb),
        out_shape=jax.ShapeDtypeStruct((t, d), F32),
        grid=(batch, nb),
        in_specs=[
            pl.BlockSpec((tm, d), lambda b, j: (b * nb + j, 0)),
            pl.BlockSpec((tm, d), lambda b, j: (jnp.minimum(b * nb + j + 1, batch * nb - 1), 0)),
            full(w_in.shape),
            pl.BlockSpec((None,) + kbd.shape[1:], lambda b, j: (b, 0, 0)),
            pl.BlockSpec((None,) + vbd.shape[1:], lambda b, j: (b, 0, 0)),
            full(wo.shape), full(g1.shape), full(b1.shape),
        ],
        out_specs=pl.BlockSpec((tm, d), lambda b, j: (b * nb + j, 0)),
        scratch_shapes=[
            pltpu.VMEM((w_in.shape[0], tm, IN_PROJ_CHUNK), F32),
            pltpu.VMEM((tm, d), BF16),
            pltpu.VMEM((nb, tm, mix_w), BF16),
            pltpu.VMEM((nb, heads * HEAD_ROWS, tm), BF16),
            pltpu.VMEM((heads * nb, mix_w), F32),
            pltpu.VMEM((heads, nb, tm), F32),
            pltpu.VMEM((heads, tm, 2 * HEAD_DIM), BF16),
            pltpu.VMEM((heads, 1, tm), F32),
            pltpu.VMEM((heads * HEAD_ROWS, tm), F32),
            pltpu.VMEM((mix_w, tm), F32),
        ],
        compiler_params=pltpu.CompilerParams(
            dimension_semantics=("arbitrary", "arbitrary"), vmem_limit_bytes=VMEM_LIMIT),
        name="moba_mixer",
    )(x2d, x2d, w_in, kbd, vbd, wo, g1, b1)


def _router_kernel(x_ref, whi_ref, wlo_ref, br_ref, tri_ref, info_ref, sel_ref, cnt_ref, run_sc):
    step = pl.program_id(0)

    @pl.when(step == 0)
    def _():
        run_sc[...] = jnp.zeros(run_sc.shape, F32)

    x_hi, x_lo = _split_bf16(x_ref[...])
    logits = (_dot_nt(whi_ref[...], x_hi) + _dot_nt(wlo_ref[...], x_hi) + _dot_nt(whi_ref[...], x_lo)
              + br_ref[...])[0:ROUTER_ROWS, :]
    row = lax.broadcasted_iota(jnp.int32, logits.shape, 0).astype(F32)
    cmax = lambda a: jnp.max(a, axis=0, keepdims=True)
    cmin = lambda a: jnp.min(a, axis=0, keepdims=True)
    csum = lambda a: jnp.sum(a, axis=0, keepdims=True)
    big = float(2 * ROUTER_ROWS)

    is_g = row < N_GROUPS
    gl = jnp.where(is_g, logits, NEG_INF)
    gmax = cmax(gl)
    gidx = cmin(jnp.where(gl == gmax, row, big))
    g_w = 1.0 / csum(jnp.where(is_g, jnp.exp(gl - gmax), 0.0))

    lo = ROUTER_OFF + EXPERTS_PER_GROUP * gidx
    el = jnp.where((row >= lo) & (row < lo + EXPERTS_PER_GROUP), logits, NEG_INF)
    v0 = cmax(el)
    i0 = cmin(jnp.where(el == v0, row, big))
    el1 = jnp.where(row == i0, NEG_INF, el)
    v1 = cmax(el1)
    i1 = cmin(jnp.where(el1 == v1, row, big))
    t = jnp.exp(v1 - v0)
    w0 = g_w / (1.0 + t)
    w1 = g_w * t / (1.0 + t)

    pick0 = row == i0
    pick1 = row == i1
    onehot = jnp.where(pick0 | pick1, 1.0, 0.0)
    before = _dot(onehot.astype(BF16), tri_ref[...]) + run_sc[...]
    pos0 = csum(jnp.where(pick0, before, 0.0))
    pos1 = csum(jnp.where(pick1, before, 0.0))
    run_sc[...] = run_sc[...] + jnp.sum(onehot, axis=1, keepdims=True)
    cnt_ref[...] = run_sc[...]

    vals = (i0 - ROUTER_OFF, i1 - ROUTER_OFF, pos0, pos1, w0, w1)
    row8 = lax.broadcasted_iota(jnp.int32, sel_ref.shape, 0)
    sel = jnp.zeros(sel_ref.shape, F32)
    for c, val in enumerate(vals):
        sel = jnp.where(row8 == c, val, sel)
    sel_ref[...] = sel
    rowl = lax.broadcasted_iota(jnp.int32, (ROUTER_LANES, x_ref.shape[0]), 0)
    info_ref[...] = jnp.where(rowl == 4, w0, jnp.where(rowl == 5, w1, 0.0)).T


def _router(x2d, whi, wlo, br, tri):
    t, d = x2d.shape
    tm = tri.shape[0]
    full = lambda shape: pl.BlockSpec(shape, lambda s: (0,) * len(shape))
    return pl.pallas_call(
        _router_kernel,
        out_shape=(jax.ShapeDtypeStruct((t, ROUTER_LANES), F32),
                   jax.ShapeDtypeStruct((t // tm, SUBLANES, tm), F32),
                   jax.ShapeDtypeStruct((ROUTER_ROWS, tm), F32)),
        grid=(t // tm,),
        in_specs=[pl.BlockSpec((tm, d), lambda s: (s, 0)),
                  full(whi.shape), full(wlo.shape), full(br.shape), full(tri.shape)],
        out_specs=(pl.BlockSpec((tm, ROUTER_LANES), lambda s: (s, 0)),
                   pl.BlockSpec((None, SUBLANES, tm), lambda s: (s, 0, 0)),
                   full((ROUTER_ROWS, tm))),
        scratch_shapes=[pltpu.VMEM((ROUTER_ROWS, tm), F32)],
        compiler_params=pltpu.CompilerParams(dimension_semantics=("arbitrary",)),
        name="router",
    )(x2d, whi, wlo, br, tri)


def _row_copy(src, src_row, dst, dst_row, sem):
    return pltpu.make_async_copy(src.at[pl.ds(src_row, 1), :], dst.at[pl.ds(dst_row, 1), :], sem)


def _slots_kernel(sel_ref, base_ref, o_ref):
    sel = sel_ref[...]
    g, _, tm = sel.shape
    row = lax.broadcasted_iota(jnp.int32, (g, ROUTER_ROWS, tm), 1).astype(F32)
    base = base_ref[...][None]
    csum = lambda a: jnp.sum(a, axis=1, keepdims=True)
    s0 = csum(jnp.where(row == sel[:, 0:1, :] + ROUTER_OFF, base, 0.0)) + sel[:, 2:3, :]
    s1 = csum(jnp.where(row == sel[:, 1:2, :] + ROUTER_OFF, base, 0.0)) + sel[:, 3:4, :]
    row8 = lax.broadcasted_iota(jnp.int32, o_ref.shape, 1)
    o_ref[...] = jnp.where(row8 == 0, s0, jnp.where(row8 == 1, s1, 0.0)).astype(jnp.int32)


def _slots(sel, base_col):
    nt, _, tm = sel.shape
    g = SLOT_TILES_PER_STEP if nt % SLOT_TILES_PER_STEP == 0 else nt
    return pl.pallas_call(
        _slots_kernel,
        out_shape=jax.ShapeDtypeStruct((nt, SUBLANES, tm), jnp.int32),
        grid=(nt // g,),
        in_specs=[pl.BlockSpec((g, SUBLANES, tm), lambda s: (s, 0, 0)),
                  pl.BlockSpec(base_col.shape, lambda s: (0, 0))],
        out_specs=pl.BlockSpec((g, SUBLANES, tm), lambda s: (s, 0, 0)),
        compiler_params=pltpu.CompilerParams(dimension_semantics=("arbitrary",)),
        name="slots",
    )(sel, base_col)


def _dispatch_kernel(nv_ref, s0_ref, s1_ref, x_ref, xs_ref, zero_sc, sem, fill_sem, *, tm):

    @pl.when(pl.program_id(0) == 0)
    def _():
        zero_sc[...] = jnp.zeros(zero_sc.shape, F32)

        def fill(t):
            return pltpu.make_async_copy(zero_sc, xs_ref.at[pl.ds(pl.multiple_of(t * tm, tm), tm), :], fill_sem)

        def start_fill(t, c):
            @pl.when(nv_ref[t] < tm)
            def _():
                fill(t).start()
            return c

        def wait_fill(t, c):
            @pl.when(nv_ref[t] < tm)
            def _():
                fill(t).wait()
            return c

        lax.fori_loop(0, nv_ref.shape[0], start_fill, 0)
        lax.fori_loop(0, nv_ref.shape[0], wait_fill, 0)

    rows = x_ref.shape[0]
    for g in range(s0_ref.shape[0]):
        def start(r, c, g=g):
            row = g * s0_ref.shape[2] + r
            _row_copy(x_ref, row, xs_ref, s0_ref[g, 0, r], sem).start(priority=0)
            _row_copy(x_ref, row, xs_ref, s1_ref[g, 0, r], sem).start(priority=1)
            return c

        lax.fori_loop(0, s0_ref.shape[2], start, 0, unroll=DMA_UNROLL)
    for _ in range(2):
        pltpu.make_async_copy(x_ref, xs_ref.at[pl.ds(0, rows), :], sem).wait()


def _dispatch(x2d, slot0, slot1, tile_valid):
    t, d = x2d.shape
    tm = TOKEN_TILE
    st = slot0.shape[2]
    rows = DISPATCH_ROWS if t % DISPATCH_ROWS == 0 else st
    n_slots = tile_valid.shape[0] * tm
    smem_rows = pl.BlockSpec((rows // st, 1, st), lambda s, nv: (s, 0, 0), memory_space=pltpu.SMEM)
    return pl.pallas_call(
        functools.partial(_dispatch_kernel, tm=tm),
        out_shape=jax.ShapeDtypeStruct((n_slots, d), F32),
        grid_spec=pltpu.PrefetchScalarGridSpec(
            num_scalar_prefetch=1,
            grid=(t // rows,),
            in_specs=[smem_rows, smem_rows, pl.BlockSpec((rows, d), lambda s, nv: (s, 0))],
            out_specs=pl.BlockSpec(memory_space=pl.ANY),
            scratch_shapes=[pltpu.VMEM((tm, d), F32), pltpu.SemaphoreType.DMA, pltpu.SemaphoreType.DMA],
        ),
        compiler_params=pltpu.CompilerParams(dimension_semantics=("arbitrary",)),
        name="dispatch",
    )(tile_valid, slot0, slot1, x2d)


def _expert_kernel(te_ref, nv_ref, first_ref, buf_ref, next_ref, xs_hbm, wg_hbm, wu_hbm, wd_hbm, ys_ref,
                   xs_buf, wg_buf, wu_buf, wd_buf, wg_sc, wu_sc, wd_sc, xs_sems, sems, *, base):
    t = pl.program_id(0)
    n_steps = pl.num_programs(0)
    tm = ys_ref.shape[0]
    used = nv_ref[t] > 0

    def xs_copy(tile):
        row0 = tile * tm if isinstance(tile, int) else pl.multiple_of(tile * tm, tm)
        slot = tile % XS_RING
        return pltpu.make_async_copy(xs_hbm.at[pl.ds(row0, tm), :], xs_buf.at[slot], xs_sems.at[slot])

    @pl.when(t == 0)
    def _():
        for ahead in range(XS_RING - 1):
            @pl.when(ahead < n_steps)
            def _():
                xs_copy(ahead).start()

    @pl.when(t + XS_RING - 1 < n_steps)
    def _():
        xs_copy(t + XS_RING - 1).start()

    xs_copy(t).wait()

    def weight_copies(e, b):
        return (pltpu.make_async_copy(wg_hbm.at[base + e], wg_buf.at[b], sems.at[b, 0]),
                pltpu.make_async_copy(wu_hbm.at[base + e], wu_buf.at[b], sems.at[b, 1]),
                pltpu.make_async_copy(wd_hbm.at[base + e], wd_buf.at[b], sems.at[b, 2]))

    @pl.when(t == 0)
    def _():
        for c in weight_copies(te_ref[0], 0):
            c.start()

    @pl.when(first_ref[t] == 1)
    def _():
        b = buf_ref[t]
        for c in weight_copies(te_ref[t], b):
            c.wait()
        wg_sc[...] = wg_buf[b].astype(BF16)
        wu_sc[...] = wu_buf[b].astype(BF16)
        wd_sc[...] = wd_buf[b].astype(BF16)

        @pl.when(next_ref[t] >= 0)
        def _():
            for c in weight_copies(next_ref[t], 1 - b):
                c.start()

    @pl.when(used)
    def _():
        xb = xs_buf[t % XS_RING].astype(BF16)
        hg = _dot(xb, wg_sc[...])
        hu = _dot(xb, wu_sc[...])
        h = hg * jax.nn.sigmoid(hg) * hu
        ys_ref[...] = _dot(h.astype(BF16), wd_sc[...])

    @pl.when(jnp.logical_not(used))
    def _():
        ys_ref[...] = jnp.zeros(ys_ref.shape, F32)


def _expert_mlp(xs, tile_expert, tile_valid, tile_first, tile_buf, tile_next, w_gate, w_up, w_down, layer):
    ns, d = xs.shape
    f = w_gate.shape[-1]
    tm = TOKEN_TILE
    hbm = pl.BlockSpec(memory_space=pl.ANY)
    return pl.pallas_call(
        functools.partial(_expert_kernel, base=layer * N_EXPERTS),
        out_shape=jax.ShapeDtypeStruct((ns, d), F32),
        grid_spec=pltpu.PrefetchScalarGridSpec(
            num_scalar_prefetch=5,
            grid=(ns // tm,),
            in_specs=[hbm, hbm, hbm, hbm],
            out_specs=pl.BlockSpec((tm, d), lambda t, *_: (t, 0)),
            scratch_shapes=[pltpu.VMEM((XS_RING, tm, d), F32),
                            pltpu.VMEM((2, d, f), F32), pltpu.VMEM((2, d, f), F32), pltpu.VMEM((2, f, d), F32),
                            pltpu.VMEM((d, f), BF16), pltpu.VMEM((d, f), BF16), pltpu.VMEM((f, d), BF16),
                            pltpu.SemaphoreType.DMA((XS_RING,)), pltpu.SemaphoreType.DMA((2, 3))],
        ),
        compiler_params=pltpu.CompilerParams(
            dimension_semantics=("arbitrary",), vmem_limit_bytes=VMEM_LIMIT),
        name="expert_mlp",
    )(tile_expert, tile_valid, tile_first, tile_buf, tile_next, xs, w_gate, w_up, w_down)


def _combine_kernel(s0_ref, s1_ref, n0_ref, n1_ref, info_ref, x_ref, ys_ref, g_ref, b_ref, o_ref,
                    y0_sc, y1_sc, sems, *, alpha):
    s = pl.program_id(0)
    rows = x_ref.shape[0]

    def start_gathers(t0_ref, t1_ref, buf):
        for g in range(t0_ref.shape[0]):
            def start(r, c, g=g):
                row = g * t0_ref.shape[2] + r
                _row_copy(ys_ref, t0_ref[g, 0, r], y0_sc.at[buf], row, sems.at[buf]).start(priority=0)
                _row_copy(ys_ref, t1_ref[g, 0, r], y1_sc.at[buf], row, sems.at[buf]).start(priority=1)
                return c

            lax.fori_loop(0, t0_ref.shape[2], start, 0, unroll=DMA_UNROLL)

    cur = s % 2

    @pl.when(s == 0)
    def _():
        start_gathers(s0_ref, s1_ref, 0)

    @pl.when(s + 1 < pl.num_programs(0))
    def _():
        start_gathers(n0_ref, n1_ref, 1 - cur)

    for dst in (y0_sc, y1_sc):
        pltpu.make_async_copy(ys_ref.at[pl.ds(0, rows), :], dst.at[cur], sems.at[cur]).wait()
    info = info_ref[...]
    f = info[:, 4:5] * y0_sc[cur] + info[:, 5:6] * y1_sc[cur]
    o_ref[...] = _ln(alpha * x_ref[...] + f, g_ref[...], b_ref[...])


def _combine(x2d, info, ys, slot0, slot1, g2, b2, alpha):
    t, d = x2d.shape
    st = slot0.shape[2]
    rows = COMBINE_ROWS if t % COMBINE_ROWS == 0 else st
    n = t // rows
    smem_rows = pl.BlockSpec((rows // st, 1, st), lambda s: (s, 0, 0), memory_space=pltpu.SMEM)
    smem_next = pl.BlockSpec((rows // st, 1, st), lambda s: (jnp.minimum(s + 1, n - 1), 0, 0),
                             memory_space=pltpu.SMEM)
    full = lambda shape: pl.BlockSpec(shape, lambda s: (0,) * len(shape))
    return pl.pallas_call(
        functools.partial(_combine_kernel, alpha=alpha),
        out_shape=jax.ShapeDtypeStruct((t, d), F32),
        grid=(n,),
        in_specs=[smem_rows, smem_rows, smem_next, smem_next,
                  pl.BlockSpec((rows, ROUTER_LANES), lambda s: (s, 0)),
                  pl.BlockSpec((rows, d), lambda s: (s, 0)),
                  pl.BlockSpec(memory_space=pl.ANY),
                  full(g2.shape), full(b2.shape)],
        out_specs=pl.BlockSpec((rows, d), lambda s: (s, 0)),
        scratch_shapes=[pltpu.VMEM((2, rows, d), F32), pltpu.VMEM((2, rows, d), F32),
                        pltpu.SemaphoreType.DMA((2,))],
        compiler_params=pltpu.CompilerParams(
            dimension_semantics=("arbitrary",), vmem_limit_bytes=VMEM_LIMIT),
        name="combine",
    )(slot0, slot1, slot0, slot1, info, x2d, ys, g2, b2)


def _moe(x2d, layer, w_rg, b_rg, w_re, b_re, w_gate, w_up, w_down, g2, b2, alpha):
    t, d = x2d.shape
    tm = TOKEN_TILE
    wr = jnp.zeros((ROUTER_LANES, d), F32).at[0:N_GROUPS].set(w_rg.T)
    wr = wr.at[ROUTER_OFF:ROUTER_OFF + N_EXPERTS].set(w_re.T)
    br = jnp.zeros((ROUTER_LANES, 1), F32).at[0:N_GROUPS, 0].set(b_rg)
    br = br.at[ROUTER_OFF:ROUTER_OFF + N_EXPERTS, 0].set(b_re)
    whi, wlo = _split_bf16(wr)
    rt = ROUTER_TILE if t % ROUTER_TILE == 0 else tm
    ridx = lax.broadcasted_iota(jnp.int32, (rt, rt), 0)
    cidx = lax.broadcasted_iota(jnp.int32, (rt, rt), 1)
    tri = (ridx < cidx).astype(BF16)
    info, sel, cnt = _router(x2d, whi, wlo, br, tri)

    counts = cnt[ROUTER_OFF:ROUTER_OFF + N_EXPERTS, 0].astype(jnp.int32)
    padded = (counts + tm - 1) // tm * tm
    ends = jnp.cumsum(padded)
    base = ends - padded
    base_col = jnp.zeros((ROUTER_ROWS, 1), F32).at[ROUTER_OFF:ROUTER_OFF + N_EXPERTS, 0].set(base.astype(F32))
    slots = _slots(sel, base_col)
    slot0, slot1 = slots[:, 0:1, :], slots[:, 1:2, :]

    n_slots = 2 * t + N_EXPERTS * tm
    n_tiles = n_slots // tm
    n_used = ends[-1] // tm
    tile_ids = jnp.arange(n_tiles, dtype=jnp.int32)
    tile_expert = jnp.sum((jnp.minimum(tile_ids, n_used - 1)[:, None] * tm >= ends[None, :]).astype(jnp.int32),
                          axis=1)
    lo = jnp.maximum(base[None, :], tile_ids[:, None] * tm)
    hi = jnp.minimum((base + counts)[None, :], (tile_ids[:, None] + 1) * tm)
    tile_valid = jnp.sum(jnp.maximum(hi - lo, 0), axis=1).astype(jnp.int32)

    used = tile_valid > 0
    tile_first = (used & ((tile_ids == 0) | (tile_expert != jnp.roll(tile_expert, 1)))).astype(jnp.int32)
    tile_buf = ((jnp.cumsum(tile_first) - 1) % 2).astype(jnp.int32)
    experts = jnp.arange(N_EXPERTS, dtype=jnp.int32)
    later_nonempty = (experts[None, :] > experts[:, None]) & (counts[None, :] > 0)
    next_expert = jnp.min(jnp.where(later_nonempty, experts[None, :], N_EXPERTS), axis=1)
    next_expert = jnp.where(next_expert < N_EXPERTS, next_expert, -1)
    tile_next = jnp.sum(jnp.where(tile_expert[:, None] == experts[None, :], next_expert[None, :], 0),
                        axis=1).astype(jnp.int32)

    xs = _dispatch(x2d, slot0, slot1, tile_valid)
    ys = _expert_mlp(xs, tile_expert, tile_valid, tile_first, tile_buf, tile_next, w_gate, w_up, w_down, layer)
    return _combine(x2d, info, ys, slot0, slot1, g2, b2, alpha)


def kernel(x, mem, w_mem_kv, conv_w_in, conv_dw_w, conv_dw_b, conv_ln_g, conv_ln_b, moba_w_in, w_o,
           ln1_g, ln1_b, w_rg, b_rg, w_re, b_re, w_gate, w_up, w_down, ln2_g, ln2_b):
    batch, seq, d = x.shape
    depth = w_o.shape[0]
    alpha = (2 * depth) ** 0.25
    t = batch * seq
    row = lambda a: a.reshape(1, -1)

    kbd, vbd = _memkv(mem, w_mem_kv)

    e_shape = w_gate.shape
    w_gate = w_gate.reshape((-1,) + e_shape[-2:])
    w_up = w_up.reshape((-1,) + e_shape[-2:])
    w_down = w_down.reshape((-1,) + w_down.shape[-2:])

    x2d = x.reshape(t, d)
    for i in range(depth):
        j = i // 2
        wo = w_o[i].astype(BF16)
        if i % 2 == 0:
            dw_w = jnp.broadcast_to(conv_dw_w[j][:, None, :], (CONV_WIDTH, SUBLANES, conv_dw_w.shape[-1]))
            w_in = conv_w_in[j].astype(BF16).reshape(d, -1, IN_PROJ_CHUNK).transpose(1, 0, 2)
            x2d = _conv_mixer(x2d, batch, w_in, dw_w, row(conv_dw_b[j]),
                              row(conv_ln_g[j]), row(conv_ln_b[j]), kbd, vbd, wo,
                              row(ln1_g[i]), row(ln1_b[i]), alpha)
        else:
            w_in = moba_w_in[j].astype(BF16).reshape(d, -1, IN_PROJ_CHUNK).transpose(1, 0, 2)
            x2d = _moba_mixer(x2d, batch, w_in, kbd, vbd, wo,
                              row(ln1_g[i]), row(ln1_b[i]), alpha)
        x2d = _moe(x2d, i, w_rg[i], b_rg[i], w_re[i], b_re[i], w_gate, w_up, w_down,
                   row(ln2_g[i]), row(ln2_b[i]), alpha)
    return x2d.reshape(batch, seq, d)
```

```python
import functools

import jax
import jax.numpy as jnp
from jax import lax
from jax.experimental import pallas as pl
from jax.experimental.pallas import tpu as pltpu

F32 = jnp.float32
BF16 = jnp.bfloat16

HEAD_DIM = 64
MEM_LEN = 256
MEM_HEADS = 4
MEM_WIDTH = MEM_HEADS * HEAD_DIM
CONV_WIDTH = 31
MOBA_BLOCK = 256
MOBA_TOPK = 3
HEAD_ROWS = HEAD_DIM + 16
MOBA_LOOKAHEAD = 6
N_GROUPS = 4
EXPERTS_PER_GROUP = 8
N_EXPERTS = N_GROUPS * EXPERTS_PER_GROUP
LN_EPS = 1e-5

SUBLANES = 8
TOKEN_TILE = 256
CONV_HALO = 32
CONV_CHUNK = 16
CONV_CHUNKS_PER_STEP = 2
IN_PROJ_CHUNK = 256
ROUTER_LANES = 128
ROUTER_ROWS = 40
ROUTER_OFF = N_GROUPS
DMA_UNROLL = 16
ROUTER_TILE = 512
DISPATCH_ROWS = 1024
COMBINE_ROWS = 512
XS_RING = 3
SLOT_TILES_PER_STEP = 16
VMEM_LIMIT = 56 * 1024 * 1024

NEG_INF = float("-inf")
LOG2_E = 1.4426950408889634


def _ln(z, g, b):
    mu = jnp.mean(z, axis=-1, keepdims=True)
    zc = z - mu
    var = jnp.mean(zc * zc, axis=-1, keepdims=True)
    return zc * lax.rsqrt(var + LN_EPS) * g + b


def _dot(a, b):
    return jnp.dot(a, b, preferred_element_type=F32)


def _dot_nt(a, b):
    return lax.dot_general(a, b, (((1,), (1,)), ((), ())), preferred_element_type=F32)


def _split_bf16(x):
    hi = x.astype(BF16)
    lo = (x - hi.astype(F32)).astype(BF16)
    return hi, lo


def _memkv_kernel(mem_ref, w_ref, kbd_ref, vbd_ref):
    kv = _dot(mem_ref[...].astype(BF16), w_ref[...].astype(BF16))
    k_t = kv[:, 0:MEM_WIDTH].T
    v = kv[:, MEM_WIDTH:]
    r = lax.broadcasted_iota(jnp.int32, kbd_ref.shape, 0)
    c = lax.broadcasted_iota(jnp.int32, kbd_ref.shape, 1)
    kbd_ref[...] = jnp.where(r // HEAD_DIM == c // MEM_LEN,
                             jnp.concatenate([k_t] * MEM_HEADS, axis=1), 0.0).astype(BF16)
    r = lax.broadcasted_iota(jnp.int32, vbd_ref.shape, 0)
    c = lax.broadcasted_iota(jnp.int32, vbd_ref.shape, 1)
    vbd_ref[...] = jnp.where(r // MEM_LEN == c // HEAD_DIM,
                             jnp.concatenate([v] * MEM_HEADS, axis=0), 0.0).astype(BF16)


def _memkv(mem, w):
    batch, m, d = mem.shape
    return pl.pallas_call(
        _memkv_kernel,
        out_shape=(jax.ShapeDtypeStruct((batch, MEM_WIDTH, MEM_HEADS * m), BF16),
                   jax.ShapeDtypeStruct((batch, MEM_HEADS * m, MEM_WIDTH), BF16)),
        grid=(batch,),
        in_specs=[pl.BlockSpec((None, m, d), lambda b: (b, 0, 0)),
                  pl.BlockSpec(w.shape, lambda b: (0, 0))],
        out_specs=(pl.BlockSpec((None, MEM_WIDTH, MEM_HEADS * m), lambda b: (b, 0, 0)),
                   pl.BlockSpec((None, MEM_HEADS * m, MEM_WIDTH), lambda b: (b, 0, 0))),
        compiler_params=pltpu.CompilerParams(dimension_semantics=("arbitrary",)),
        name="memkv",
    )(mem, w)


def _mem_attention(qm, kbd, vbd):
    s = _dot(qm.astype(BF16), kbd) * (HEAD_DIM ** -0.5)
    parts = []
    for h in range(MEM_HEADS):
        seg = s[:, h * MEM_LEN:(h + 1) * MEM_LEN]
        m = jnp.max(seg, axis=-1, keepdims=True)
        e = jnp.exp(seg - m)
        parts.append(e / jnp.sum(e, axis=-1, keepdims=True))
    p = jnp.concatenate(parts, axis=-1)
    return _dot(p.astype(BF16), vbd)


def _out_proj_ln(x, y_mix, y_mem, wo_ref, g_ref, b_ref, alpha, mix_w):
    y = _dot(y_mix.astype(BF16), wo_ref[0:mix_w, :]) + _dot(y_mem.astype(BF16), wo_ref[mix_w:, :])
    return _ln(alpha * x + y, g_ref[...], b_ref[...])


def _conv_mixer_kernel(x_ref, xn_ref, win_ref, dww_ref, dwb_ref, cg_ref, cb_ref, kbd_ref, vbd_ref,
                       wo_ref, g1_ref, b1_ref, o_ref, ustage, xn_sc, hbuf, zbuf, cbuf, *, alpha, mix_w):
    tm = x_ref.shape[0]
    n_cc = win_ref.shape[0]
    per = mix_w // IN_PROJ_CHUNK
    j = pl.program_id(1)

    @pl.when((pl.program_id(0) == 0) & (j == 0))
    def _():
        xb = x_ref[...].astype(BF16)
        for c in range(n_cc):
            ustage[c] = _dot(xb, win_ref[c])

    x = x_ref[...]
    qm = ustage[2 * per]
    h = jnp.concatenate([ustage[c] * jax.nn.sigmoid(ustage[per + c]) for c in range(per)], axis=1)
    xn_sc[...] = xn_ref[...].astype(BF16)

    @pl.when(j == 0)
    def _():
        hbuf[0:CONV_HALO, :] = jnp.zeros((CONV_HALO, mix_w), F32)

    hbuf[CONV_HALO:CONV_HALO + tm, :] = h

    shifted_rows = CONV_HALO + tm - SUBLANES
    for a in range(1, SUBLANES):
        zbuf[a - 1, 0:shifted_rows, :] = hbuf[a:a + shifted_rows, :]

    first = CONV_HALO - (CONV_WIDTH - 1)
    aligned = lambda v, m: v if isinstance(v, int) else pl.multiple_of(v, m)

    def conv_rows(r0):
        acc = jnp.broadcast_to(dwb_ref[...], (CONV_CHUNK, mix_w))
        for a in range(SUBLANES):
            offs = [first + k - a for k in range(CONV_WIDTH) if (first + k) % SUBLANES == a]
            r = aligned(r0 + offs[0], SUBLANES)
            span = offs[-1] - offs[0] + CONV_CHUNK
            z = hbuf[pl.ds(r, span), :] if a == 0 else zbuf[a - 1, pl.ds(r, span), :]
            for off in offs:
                k = off + a - first
                rows = z[off - offs[0]:off - offs[0] + CONV_CHUNK, :]
                acc = acc + jnp.tile(dww_ref[k], (CONV_CHUNK // SUBLANES, 1)) * rows
        cbuf[pl.ds(r0, CONV_CHUNK), :] = acc

    def project_next(c):
        ustage[c] = _dot(xn_sc[...], win_ref[c])

    step_rows = CONV_CHUNK * CONV_CHUNKS_PER_STEP
    paired = min(n_cc, tm // step_rows)

    def paired_step(c, carry):
        project_next(c)
        for q in range(CONV_CHUNKS_PER_STEP):
            conv_rows(pl.multiple_of(c * step_rows + q * CONV_CHUNK, CONV_CHUNK))
        return carry

    lax.fori_loop(0, paired, paired_step, 0)
    for r0 in range(paired * step_rows, tm, CONV_CHUNK):
        conv_rows(r0)
    for c in range(paired, n_cc):
        project_next(c)

    hbuf[0:CONV_HALO, :] = hbuf[tm:tm + CONV_HALO, :]

    cn = _ln(cbuf[...], cg_ref[...], cb_ref[...])
    y_mix = cn * jax.nn.sigmoid(cn)
    y_mem = _mem_attention(qm, kbd_ref[...], vbd_ref[...])
    o_ref[...] = _out_proj_ln(x, y_mix, y_mem, wo_ref, g1_ref, b1_ref, alpha, mix_w)


def _conv_mixer(x2d, batch, w_in, dw_w, dw_b, cg, cb, kbd, vbd, wo, g1, b1, alpha):
    t, d = x2d.shape
    tm = TOKEN_TILE
    nj = t // batch // tm
    mix_w = d - MEM_WIDTH
    full = lambda shape: pl.BlockSpec(shape, lambda b, j: (0,) * len(shape))
    return pl.pallas_call(
        functools.partial(_conv_mixer_kernel, alpha=alpha, mix_w=mix_w),
        out_shape=jax.ShapeDtypeStruct((t, d), F32),
        grid=(batch, nj),
        in_specs=[
            pl.BlockSpec((tm, d), lambda b, j: (b * nj + j, 0)),
            pl.BlockSpec((tm, d), lambda b, j: (jnp.minimum(b * nj + j + 1, batch * nj - 1), 0)),
            full(w_in.shape), full(dw_w.shape), full(dw_b.shape), full(cg.shape), full(cb.shape),
            pl.BlockSpec((None,) + kbd.shape[1:], lambda b, j: (b, 0, 0)),
            pl.BlockSpec((None,) + vbd.shape[1:], lambda b, j: (b, 0, 0)),
            full(wo.shape), full(g1.shape), full(b1.shape),
        ],
        out_specs=pl.BlockSpec((tm, d), lambda b, j: (b * nj + j, 0)),
        scratch_shapes=[pltpu.VMEM((w_in.shape[0], tm, IN_PROJ_CHUNK), F32),
                        pltpu.VMEM((tm, d), BF16),
                        pltpu.VMEM((CONV_HALO + tm, mix_w), F32),
                        pltpu.VMEM((SUBLANES - 1, CONV_HALO + tm, mix_w), F32),
                        pltpu.VMEM((tm, mix_w), F32)],
        compiler_params=pltpu.CompilerParams(
            dimension_semantics=("arbitrary", "arbitrary"), vmem_limit_bytes=VMEM_LIMIT),
        name="conv_mixer",
    )(x2d, x2d, w_in, dw_w, dw_b, cg, cb, kbd, vbd, wo, g1, b1)


def _moba_select_bias(gate, i):
    nb = gate.shape[1]
    blk = lax.broadcasted_iota(jnp.int32, gate.shape, 1).astype(F32)
    past = blk < i.astype(F32)
    gm = jnp.where(past, gate, NEG_INF)
    bias = jnp.full(gate.shape, NEG_INF, F32)
    for _ in range(MOBA_TOPK):
        top = jnp.max(gm, axis=1, keepdims=True)
        first = jnp.min(jnp.where(gm == top, blk, float(nb)), axis=1, keepdims=True)
        taken = blk == first
        bias = jnp.where(taken & past, 0.0, bias)
        gm = jnp.where(taken, NEG_INF, gm)
    return bias


def _moba_mixer_kernel(x_ref, xn_ref, win_ref, kbd_ref, vbd_ref, wo_ref, g1_ref, b1_ref, o_ref,
                       ustage, xn_sc, k_sc, vt_sc, kmt_sc, bias_sc, qh_sc, m_sc, acc_sc, yt_sc,
                       *, alpha, mix_w, nb):
    tm = x_ref.shape[0]
    heads = mix_w // HEAD_DIM
    n_cc = win_ref.shape[0]
    per = mix_w // IN_PROJ_CHUNK
    i = pl.program_id(1)

    @pl.when((pl.program_id(0) == 0) & (i == 0))
    def _():
        xb = x_ref[...].astype(BF16)
        for c in range(n_cc):
            ustage[c] = _dot(xb, win_ref[c])

    x = x_ref[...]
    part = lambda p: jnp.concatenate([ustage[p * per + c] for c in range(per)], axis=1)
    q = part(0) * (HEAD_DIM ** -0.5 * LOG2_E)
    k = part(1)
    v = part(2)
    qm = ustage[3 * per]
    xn_sc[...] = xn_ref[...].astype(BF16)

    def project_next(c):
        ustage[c] = _dot(xn_sc[...], win_ref[c])

    @pl.when(i == 0)
    def _():
        kmt_sc[...] = jnp.zeros(kmt_sc.shape, F32)

    k_sc[i] = k.astype(BF16)
    v_t = v.T.astype(BF16)
    for h in range(heads):
        vt_sc[i, h * HEAD_ROWS:h * HEAD_ROWS + HEAD_DIM, :] = v_t[h * HEAD_DIM:(h + 1) * HEAD_DIM, :]
        vt_sc[i, h * HEAD_ROWS + HEAD_DIM:(h + 1) * HEAD_ROWS, :] = jnp.ones((HEAD_ROWS - HEAD_DIM, tm), BF16)
    kmean = jnp.mean(k, axis=0, keepdims=True)
    lane = lax.broadcasted_iota(jnp.int32, (1, mix_w), 1)
    for h in range(heads):
        in_head = (lane >= h * HEAD_DIM) & (lane < (h + 1) * HEAD_DIM)
        kmt_sc[pl.ds(h * nb + i, 1), :] = jnp.where(in_head, kmean, 0.0)

    q_hi, q_lo = _split_bf16(q)
    km_hi, km_lo = _split_bf16(kmt_sc[...])
    gate_t = _dot_nt(km_hi, q_hi) + _dot_nt(km_hi, q_lo) + _dot_nt(km_lo, q_hi)
    bias_sc[...] = _moba_select_bias(gate_t.reshape(heads, nb, tm), i)

    lane_p = lax.broadcasted_iota(jnp.int32, (tm, 2 * HEAD_DIM), 1)
    for h in range(heads):
        qp = q_hi[:, (h // 2) * 2 * HEAD_DIM:(h // 2 + 1) * 2 * HEAD_DIM]
        keep = (lane_p < HEAD_DIM) if h % 2 == 0 else (lane_p >= HEAD_DIM)
        qh_sc[h] = jnp.where(keep, qp, jnp.zeros_like(qp))

    def scores(j, h):
        cols = slice((h // 2) * 2 * HEAD_DIM, (h // 2 + 1) * 2 * HEAD_DIM)
        return _dot_nt(k_sc[j, :, cols], qh_sc[h])

    rows = lambda h: slice(h * HEAD_ROWS, (h + 1) * HEAD_ROWS)

    kidx = lax.broadcasted_iota(jnp.int32, (tm, tm), 0)
    qidx = lax.broadcasted_iota(jnp.int32, (tm, tm), 1)
    causal = kidx <= qidx
    def heads_pipelined(j):
        pending = [scores(j, h) for h in range(MOBA_LOOKAHEAD)]
        for h in range(heads):
            if h + MOBA_LOOKAHEAD < heads:
                pending.append(scores(j, h + MOBA_LOOKAHEAD))
            yield h, pending.pop(0)

    def own_block(j, c):
        for h, s in heads_pipelined(j):
            if h < n_cc:
                project_next(h)
            s = jnp.where(causal, s, NEG_INF)
            m = jnp.max(s, axis=0, keepdims=True)
            e = jnp.exp2(s - m)
            m_sc[h] = m
            acc_sc[rows(h), :] = _dot(vt_sc[j, rows(h), :], e.astype(BF16))
        return c

    lax.fori_loop(i, i + 1, own_block, 0)
    for c in range(heads, n_cc):
        project_next(c)

    def body(j, c):
        for h, s in heads_pipelined(j):
            b = bias_sc[h, pl.ds(j, 1), :]
            m_old = m_sc[h]
            m_new = jnp.maximum(m_old, jnp.max(s, axis=0, keepdims=True) + b)
            e = jnp.exp2(s - (m_new - b))
            corr = jnp.exp2(m_old - m_new)
            m_sc[h] = m_new
            acc_sc[rows(h), :] = corr * acc_sc[rows(h), :] + _dot(vt_sc[j, rows(h), :], e.astype(BF16))
        return c

    lax.fori_loop(0, i, body, 0)
    for h in range(heads):
        r0 = h * HEAD_ROWS
        yt_sc[h * HEAD_DIM:(h + 1) * HEAD_DIM, :] = (acc_sc[r0:r0 + HEAD_DIM, :]
                                                     / acc_sc[r0 + HEAD_DIM:r0 + HEAD_DIM + 1, :])

    y_mix = yt_sc[...].T
    y_mem = _mem_attention(qm, kbd_ref[...], vbd_ref[...])
    o_ref[...] = _out_proj_ln(x, y_mix, y_mem, wo_ref, g1_ref, b1_ref, alpha, mix_w)


def _moba_mixer(x2d, batch, w_in, kbd, vbd, wo, g1, b1, alpha):
    t, d = x2d.shape
    tm = MOBA_BLOCK
    nb = t // batch // tm
    mix_w = d - MEM_WIDTH
    heads = mix_w // HEAD_DIM
    full = lambda shape: pl.BlockSpec(shape, lambda b, j: (0,) * len(shape))
    return pl.pallas_call(
        functools.partial(_moba_mixer_kernel, alpha=alpha, mix_w=mix_w, nb=nb),
        out_shape=jax.ShapeDtypeStruct((t, d), F32),
        grid=(batch, nb),
        in_specs=[
            pl.BlockSpec((tm, d), lambda b, j: (b * nb + j, 0)),
            pl.BlockSpec((tm, d), lambda b, j: (jnp.minimum(b * nb + j + 1, batch * nb - 1), 0)),
            full(w_in.shape),
            pl.BlockSpec((None,) + kbd.shape[1:], lambda b, j: (b, 0, 0)),
            pl.BlockSpec((None,) + vbd.shape[1:], lambda b, j: (b, 0, 0)),
            full(wo.shape), full(g1.shape), full(b1.shape),
        ],
        out_specs=pl.BlockSpec((tm, d), lambda b, j: (b * nb + j, 0)),
        scratch_shapes=[
            pltpu.VMEM((w_in.shape[0], tm, IN_PROJ_CHUNK), F32),
            pltpu.VMEM((tm, d), BF16),
            pltpu.VMEM((nb, tm, mix_w), BF16),
            pltpu.VMEM((nb, heads * HEAD_ROWS, tm), BF16),
            pltpu.VMEM((heads * nb, mix_w), F32),
            pltpu.VMEM((heads, nb, tm), F32),
            pltpu.VMEM((heads, tm, 2 * HEAD_DIM), BF16),
            pltpu.VMEM((heads, 1, tm), F32),
            pltpu.VMEM((heads * HEAD_ROWS, tm), F32),
            pltpu.VMEM((mix_w, tm), F32),
        ],
        compiler_params=pltpu.CompilerParams(
            dimension_semantics=("arbitrary", "arbitrary"), vmem_limit_bytes=VMEM_LIMIT),
        name="moba_mixer",
    )(x2d, x2d, w_in, kbd, vbd, wo, g1, b1)


def _router_kernel(x_ref, whi_ref, wlo_ref, br_ref, tri_ref, info_ref, sel_ref, cnt_ref, run_sc):
    step = pl.program_id(0)

    @pl.when(step == 0)
    def _():
        run_sc[...] = jnp.zeros(run_sc.shape, F32)

    x_hi, x_lo = _split_bf16(x_ref[...])
    logits = (_dot_nt(whi_ref[...], x_hi) + _dot_nt(wlo_ref[...], x_hi) + _dot_nt(whi_ref[...], x_lo)
              + br_ref[...])[0:ROUTER_ROWS, :]
    row = lax.broadcasted_iota(jnp.int32, logits.shape, 0).astype(F32)
    cmax = lambda a: jnp.max(a, axis=0, keepdims=True)
    cmin = lambda a: jnp.min(a, axis=0, keepdims=True)
    csum = lambda a: jnp.sum(a, axis=0, keepdims=True)
    big = float(2 * ROUTER_ROWS)

    is_g = row < N_GROUPS
    gl = jnp.where(is_g, logits, NEG_INF)
    gmax = cmax(gl)
    gidx = cmin(jnp.where(gl == gmax, row, big))
    g_w = 1.0 / csum(jnp.where(is_g, jnp.exp(gl - gmax), 0.0))

    lo = ROUTER_OFF + EXPERTS_PER_GROUP * gidx
    el = jnp.where((row >= lo) & (row < lo + EXPERTS_PER_GROUP), logits, NEG_INF)
    v0 = cmax(el)
    i0 = cmin(jnp.where(el == v0, row, big))
    el1 = jnp.where(row == i0, NEG_INF, el)
    v1 = cmax(el1)
    i1 = cmin(jnp.where(el1 == v1, row, big))
    t = jnp.exp(v1 - v0)
    w0 = g_w / (1.0 + t)
    w1 = g_w * t / (1.0 + t)

    pick0 = row == i0
    pick1 = row == i1
    onehot = jnp.where(pick0 | pick1, 1.0, 0.0)
    before = _dot(onehot.astype(BF16), tri_ref[...]) + run_sc[...]
    pos0 = csum(jnp.where(pick0, before, 0.0))
    pos1 = csum(jnp.where(pick1, before, 0.0))
    run_sc[...] = run_sc[...] + jnp.sum(onehot, axis=1, keepdims=True)
    cnt_ref[...] = run_sc[...]

    vals = (i0 - ROUTER_OFF, i1 - ROUTER_OFF, pos0, pos1, w0, w1)
    row8 = lax.broadcasted_iota(jnp.int32, sel_ref.shape, 0)
    sel = jnp.zeros(sel_ref.shape, F32)
    for c, val in enumerate(vals):
        sel = jnp.where(row8 == c, val, sel)
    sel_ref[...] = sel
    rowl = lax.broadcasted_iota(jnp.int32, (ROUTER_LANES, x_ref.shape[0]), 0)
    info_ref[...] = jnp.where(rowl == 4, w0, jnp.where(rowl == 5, w1, 0.0)).T


def _router(x2d, whi, wlo, br, tri):
    t, d = x2d.shape
    tm = tri.shape[0]
    full = lambda shape: pl.BlockSpec(shape, lambda s: (0,) * len(shape))
    return pl.pallas_call(
        _router_kernel,
        out_shape=(jax.ShapeDtypeStruct((t, ROUTER_LANES), F32),
                   jax.ShapeDtypeStruct((t // tm, SUBLANES, tm), F32),
                   jax.ShapeDtypeStruct((ROUTER_ROWS, tm), F32)),
        grid=(t // tm,),
        in_specs=[pl.BlockSpec((tm, d), lambda s: (s, 0)),
                  full(whi.shape), full(wlo.shape), full(br.shape), full(tri.shape)],
        out_specs=(pl.BlockSpec((tm, ROUTER_LANES), lambda s: (s, 0)),
                   pl.BlockSpec((None, SUBLANES, tm), lambda s: (s, 0, 0)),
                   full((ROUTER_ROWS, tm))),
        scratch_shapes=[pltpu.VMEM((ROUTER_ROWS, tm), F32)],
        compiler_params=pltpu.CompilerParams(dimension_semantics=("arbitrary",)),
        name="router",
    )(x2d, whi, wlo, br, tri)


def _row_copy(src, src_row, dst, dst_row, sem):
    return pltpu.make_async_copy(src.at[pl.ds(src_row, 1), :], dst.at[pl.ds(dst_row, 1), :], sem)


def _slots_kernel(sel_ref, base_ref, o_ref):
    sel = sel_ref[...]
    g, _, tm = sel.shape
    row = lax.broadcasted_iota(jnp.int32, (g, ROUTER_ROWS, tm), 1).astype(F32)
    base = base_ref[...][None]
    csum = lambda a: jnp.sum(a, axis=1, keepdims=True)
    s0 = csum(jnp.where(row == sel[:, 0:1, :] + ROUTER_OFF, base, 0.0)) + sel[:, 2:3, :]
    s1 = csum(jnp.where(row == sel[:, 1:2, :] + ROUTER_OFF, base, 0.0)) + sel[:, 3:4, :]
    row8 = lax.broadcasted_iota(jnp.int32, o_ref.shape, 1)
    o_ref[...] = jnp.where(row8 == 0, s0, jnp.where(row8 == 1, s1, 0.0)).astype(jnp.int32)


def _slots(sel, base_col):
    nt, _, tm = sel.shape
    g = SLOT_TILES_PER_STEP if nt % SLOT_TILES_PER_STEP == 0 else nt
    return pl.pallas_call(
        _slots_kernel,
        out_shape=jax.ShapeDtypeStruct((nt, SUBLANES, tm), jnp.int32),
        grid=(nt // g,),
        in_specs=[pl.BlockSpec((g, SUBLANES, tm), lambda s: (s, 0, 0)),
                  pl.BlockSpec(base_col.shape, lambda s: (0, 0))],
        out_specs=pl.BlockSpec((g, SUBLANES, tm), lambda s: (s, 0, 0)),
        compiler_params=pltpu.CompilerParams(dimension_semantics=("arbitrary",)),
        name="slots",
    )(sel, base_col)


def _dispatch_kernel(nv_ref, s0_ref, s1_ref, x_ref, xs_ref, zero_sc, sem, fill_sem, *, tm):

    @pl.when(pl.program_id(0) == 0)
    def _():
        zero_sc[...] = jnp.zeros(zero_sc.shape, F32)

        def fill(t):
            return pltpu.make_async_copy(zero_sc, xs_ref.at[pl.ds(pl.multiple_of(t * tm, tm), tm), :], fill_sem)

        def start_fill(t, c):
            @pl.when(nv_ref[t] < tm)
            def _():
                fill(t).start()
            return c

        def wait_fill(t, c):
            @pl.when(nv_ref[t] < tm)
            def _():
                fill(t).wait()
            return c

        lax.fori_loop(0, nv_ref.shape[0], start_fill, 0)
        lax.fori_loop(0, nv_ref.shape[0], wait_fill, 0)

    rows = x_ref.shape[0]
    for g in range(s0_ref.shape[0]):
        def start(r, c, g=g):
            row = g * s0_ref.shape[2] + r
            _row_copy(x_ref, row, xs_ref, s0_ref[g, 0, r], sem).start(priority=0)
            _row_copy(x_ref, row, xs_ref, s1_ref[g, 0, r], sem).start(priority=1)
            return c

        lax.fori_loop(0, s0_ref.shape[2], start, 0, unroll=DMA_UNROLL)
    for _ in range(2):
        pltpu.make_async_copy(x_ref, xs_ref.at[pl.ds(0, rows), :], sem).wait()


def _dispatch(x2d, slot0, slot1, tile_valid):
    t, d = x2d.shape
    tm = TOKEN_TILE
    st = slot0.shape[2]
    rows = DISPATCH_ROWS if t % DISPATCH_ROWS == 0 else st
    n_slots = tile_valid.shape[0] * tm
    smem_rows = pl.BlockSpec((rows // st, 1, st), lambda s, nv: (s, 0, 0), memory_space=pltpu.SMEM)
    return pl.pallas_call(
        functools.partial(_dispatch_kernel, tm=tm),
        out_shape=jax.ShapeDtypeStruct((n_slots, d), F32),
        grid_spec=pltpu.PrefetchScalarGridSpec(
            num_scalar_prefetch=1,
            grid=(t // rows,),
            in_specs=[smem_rows, smem_rows, pl.BlockSpec((rows, d), lambda s, nv: (s, 0))],
            out_specs=pl.BlockSpec(memory_space=pl.ANY),
            scratch_shapes=[pltpu.VMEM((tm, d), F32), pltpu.SemaphoreType.DMA, pltpu.SemaphoreType.DMA],
        ),
        compiler_params=pltpu.CompilerParams(dimension_semantics=("arbitrary",)),
        name="dispatch",
    )(tile_valid, slot0, slot1, x2d)


def _expert_kernel(te_ref, nv_ref, first_ref, buf_ref, next_ref, xs_hbm, wg_hbm, wu_hbm, wd_hbm, ys_ref,
                   xs_buf, wg_buf, wu_buf, wd_buf, wg_sc, wu_sc, wd_sc, xs_sems, sems, *, base):
    t = pl.program_id(0)
    n_steps = pl.num_programs(0)
    tm = ys_ref.shape[0]
    used = nv_ref[t] > 0

    def xs_copy(tile):
        row0 = tile * tm if isinstance(tile, int) else pl.multiple_of(tile * tm, tm)
        slot = tile % XS_RING
        return pltpu.make_async_copy(xs_hbm.at[pl.ds(row0, tm), :], xs_buf.at[slot], xs_sems.at[slot])

    @pl.when(t == 0)
    def _():
        for ahead in range(XS_RING - 1):
            @pl.when(ahead < n_steps)
            def _():
                xs_copy(ahead).start()

    @pl.when(t + XS_RING - 1 < n_steps)
    def _():
        xs_copy(t + XS_RING - 1).start()

    xs_copy(t).wait()

    def weight_copies(e, b):
        return (pltpu.make_async_copy(wg_hbm.at[base + e], wg_buf.at[b], sems.at[b, 0]),
                pltpu.make_async_copy(wu_hbm.at[base + e], wu_buf.at[b], sems.at[b, 1]),
                pltpu.make_async_copy(wd_hbm.at[base + e], wd_buf.at[b], sems.at[b, 2]))

    @pl.when(t == 0)
    def _():
        for c in weight_copies(te_ref[0], 0):
            c.start()

    @pl.when(first_ref[t] == 1)
    def _():
        b = buf_ref[t]
        for c in weight_copies(te_ref[t], b):
            c.wait()
        wg_sc[...] = wg_buf[b].astype(BF16)
        wu_sc[...] = wu_buf[b].astype(BF16)
        wd_sc[...] = wd_buf[b].astype(BF16)

        @pl.when(next_ref[t] >= 0)
        def _():
            for c in weight_copies(next_ref[t], 1 - b):
                c.start()

    @pl.when(used)
    def _():
        xb = xs_buf[t % XS_RING].astype(BF16)
        hg = _dot(xb, wg_sc[...])
        hu = _dot(xb, wu_sc[...])
        h = hg * jax.nn.sigmoid(hg) * hu
        ys_ref[...] = _dot(h.astype(BF16), wd_sc[...])

    @pl.when(jnp.logical_not(used))
    def _():
        ys_ref[...] = jnp.zeros(ys_ref.shape, F32)


def _expert_mlp(xs, tile_expert, tile_valid, tile_first, tile_buf, tile_next, w_gate, w_up, w_down, layer):
    ns, d = xs.shape
    f = w_gate.shape[-1]
    tm = TOKEN_TILE
    hbm = pl.BlockSpec(memory_space=pl.ANY)
    return pl.pallas_call(
        functools.partial(_expert_kernel, base=layer * N_EXPERTS),
        out_shape=jax.ShapeDtypeStruct((ns, d), F32),
        grid_spec=pltpu.PrefetchScalarGridSpec(
            num_scalar_prefetch=5,
            grid=(ns // tm,),
            in_specs=[hbm, hbm, hbm, hbm],
            out_specs=pl.BlockSpec((tm, d), lambda t, *_: (t, 0)),
            scratch_shapes=[pltpu.VMEM((XS_RING, tm, d), F32),
                            pltpu.VMEM((2, d, f), F32), pltpu.VMEM((2, d, f), F32), pltpu.VMEM((2, f, d), F32),
                            pltpu.VMEM((d, f), BF16), pltpu.VMEM((d, f), BF16), pltpu.VMEM((f, d), BF16),
                            pltpu.SemaphoreType.DMA((XS_RING,)), pltpu.SemaphoreType.DMA((2, 3))],
        ),
        compiler_params=pltpu.CompilerParams(
            dimension_semantics=("arbitrary",), vmem_limit_bytes=VMEM_LIMIT),
        name="expert_mlp",
    )(tile_expert, tile_valid, tile_first, tile_buf, tile_next, xs, w_gate, w_up, w_down)


def _combine_kernel(s0_ref, s1_ref, n0_ref, n1_ref, info_ref, x_ref, ys_ref, g_ref, b_ref, o_ref,
                    y0_sc, y1_sc, sems, *, alpha):
    s = pl.program_id(0)
    rows = x_ref.shape[0]

    def start_gathers(t0_ref, t1_ref, buf):
        for g in range(t0_ref.shape[0]):
            def start(r, c, g=g):
                row = g * t0_ref.shape[2] + r
                _row_copy(ys_ref, t0_ref[g, 0, r], y0_sc.at[buf], row, sems.at[buf]).start(priority=0)
                _row_copy(ys_ref, t1_ref[g, 0, r], y1_sc.at[buf], row, sems.at[buf]).start(priority=1)
                return c

            lax.fori_loop(0, t0_ref.shape[2], start, 0, unroll=DMA_UNROLL)

    cur = s % 2

    @pl.when(s == 0)
    def _():
        start_gathers(s0_ref, s1_ref, 0)

    @pl.when(s + 1 < pl.num_programs(0))
    def _():
        start_gathers(n0_ref, n1_ref, 1 - cur)

    for dst in (y0_sc, y1_sc):
        pltpu.make_async_copy(ys_ref.at[pl.ds(0, rows), :], dst.at[cur], sems.at[cur]).wait()
    info = info_ref[...]
    f = info[:, 4:5] * y0_sc[cur] + info[:, 5:6] * y1_sc[cur]
    o_ref[...] = _ln(alpha * x_ref[...] + f, g_ref[...], b_ref[...])


def _combine(x2d, info, ys, slot0, slot1, g2, b2, alpha):
    t, d = x2d.shape
    st = slot0.shape[2]
    rows = COMBINE_ROWS if t % COMBINE_ROWS == 0 else st
    n = t // rows
    smem_rows = pl.BlockSpec((rows // st, 1, st), lambda s: (s, 0, 0), memory_space=pltpu.SMEM)
    smem_next = pl.BlockSpec((rows // st, 1, st), lambda s: (jnp.minimum(s + 1, n - 1), 0, 0),
                             memory_space=pltpu.SMEM)
    full = lambda shape: pl.BlockSpec(shape, lambda s: (0,) * len(shape))
    return pl.pallas_call(
        functools.partial(_combine_kernel, alpha=alpha),
        out_shape=jax.ShapeDtypeStruct((t, d), F32),
        grid=(n,),
        in_specs=[smem_rows, smem_rows, smem_next, smem_next,
                  pl.BlockSpec((rows, ROUTER_LANES), lambda s: (s, 0)),
                  pl.BlockSpec((rows, d), lambda s: (s, 0)),
                  pl.BlockSpec(memory_space=pl.ANY),
                  full(g2.shape), full(b2.shape)],
        out_specs=pl.BlockSpec((rows, d), lambda s: (s, 0)),
        scratch_shapes=[pltpu.VMEM((2, rows, d), F32), pltpu.VMEM((2, rows, d), F32),
                        pltpu.SemaphoreType.DMA((2,))],
        compiler_params=pltpu.CompilerParams(
            dimension_semantics=("arbitrary",), vmem_limit_bytes=VMEM_LIMIT),
        name="combine",
    )(slot0, slot1, slot0, slot1, info, x2d, ys, g2, b2)


def _moe(x2d, layer, w_rg, b_rg, w_re, b_re, w_gate, w_up, w_down, g2, b2, alpha):
    t, d = x2d.shape
    tm = TOKEN_TILE
    wr = jnp.zeros((ROUTER_LANES, d), F32).at[0:N_GROUPS].set(w_rg.T)
    wr = wr.at[ROUTER_OFF:ROUTER_OFF + N_EXPERTS].set(w_re.T)
    br = jnp.zeros((ROUTER_LANES, 1), F32).at[0:N_GROUPS, 0].set(b_rg)
    br = br.at[ROUTER_OFF:ROUTER_OFF + N_EXPERTS, 0].set(b_re)
    whi, wlo = _split_bf16(wr)
    rt = ROUTER_TILE if t % ROUTER_TILE == 0 else tm
    ridx = lax.broadcasted_iota(jnp.int32, (rt, rt), 0)
    cidx = lax.broadcasted_iota(jnp.int32, (rt, rt), 1)
    tri = (ridx < cidx).astype(BF16)
    info, sel, cnt = _router(x2d, whi, wlo, br, tri)

    counts = cnt[ROUTER_OFF:ROUTER_OFF + N_EXPERTS, 0].astype(jnp.int32)
    padded = (counts + tm - 1) // tm * tm
    ends = jnp.cumsum(padded)
    base = ends - padded
    base_col = jnp.zeros((ROUTER_ROWS, 1), F32).at[ROUTER_OFF:ROUTER_OFF + N_EXPERTS, 0].set(base.astype(F32))
    slots = _slots(sel, base_col)
    slot0, slot1 = slots[:, 0:1, :], slots[:, 1:2, :]

    n_slots = 2 * t + N_EXPERTS * tm
    n_tiles = n_slots // tm
    n_used = ends[-1] // tm
    tile_ids = jnp.arange(n_tiles, dtype=jnp.int32)
    tile_expert = jnp.sum((jnp.minimum(tile_ids, n_used - 1)[:, None] * tm >= ends[None, :]).astype(jnp.int32),
                          axis=1)
    lo = jnp.maximum(base[None, :], tile_ids[:, None] * tm)
    hi = jnp.minimum((base + counts)[None, :], (tile_ids[:, None] + 1) * tm)
    tile_valid = jnp.sum(jnp.maximum(hi - lo, 0), axis=1).astype(jnp.int32)

    used = tile_valid > 0
    tile_first = (used & ((tile_ids == 0) | (tile_expert != jnp.roll(tile_expert, 1)))).astype(jnp.int32)
    tile_buf = ((jnp.cumsum(tile_first) - 1) % 2).astype(jnp.int32)
    experts = jnp.arange(N_EXPERTS, dtype=jnp.int32)
    later_nonempty = (experts[None, :] > experts[:, None]) & (counts[None, :] > 0)
    next_expert = jnp.min(jnp.where(later_nonempty, experts[None, :], N_EXPERTS), axis=1)
    next_expert = jnp.where(next_expert < N_EXPERTS, next_expert, -1)
    tile_next = jnp.sum(jnp.where(tile_expert[:, None] == experts[None, :], next_expert[None, :], 0),
                        axis=1).astype(jnp.int32)

    xs = _dispatch(x2d, slot0, slot1, tile_valid)
    ys = _expert_mlp(xs, tile_expert, tile_valid, tile_first, tile_buf, tile_next, w_gate, w_up, w_down, layer)
    return _combine(x2d, info, ys, slot0, slot1, g2, b2, alpha)


def kernel(x, mem, w_mem_kv, conv_w_in, conv_dw_w, conv_dw_b, conv_ln_g, conv_ln_b, moba_w_in, w_o,
           ln1_g, ln1_b, w_rg, b_rg, w_re, b_re, w_gate, w_up, w_down, ln2_g, ln2_b):
    batch, seq, d = x.shape
    depth = w_o.shape[0]
    alpha = (2 * depth) ** 0.25
    t = batch * seq
    row = lambda a: a.reshape(1, -1)

    kbd, vbd = _memkv(mem, w_mem_kv)

    e_shape = w_gate.shape
    w_gate = w_gate.reshape((-1,) + e_shape[-2:])
    w_up = w_up.reshape((-1,) + e_shape[-2:])
    w_down = w_down.reshape((-1,) + w_down.shape[-2:])

    x2d = x.reshape(t, d)
    for i in range(depth):
        j = i // 2
        wo = w_o[i].astype(BF16)
        if i % 2 == 0:
            dw_w = jnp.broadcast_to(conv_dw_w[j][:, None, :], (CONV_WIDTH, SUBLANES, conv_dw_w.shape[-1]))
            w_in = conv_w_in[j].astype(BF16).reshape(d, -1, IN_PROJ_CHUNK).transpose(1, 0, 2)
            x2d = _conv_mixer(x2d, batch, w_in, dw_w, row(conv_dw_b[j]),
                              row(conv_ln_g[j]), row(conv_ln_b[j]), kbd, vbd, wo,
                              row(ln1_g[i]), row(ln1_b[i]), alpha)
        else:
            w_in = moba_w_in[j].astype(BF16).reshape(d, -1, IN_PROJ_CHUNK).transpose(1, 0, 2)
            x2d = _moba_mixer(x2d, batch, w_in, kbd, vbd, wo,
                              row(ln1_g[i]), row(ln1_b[i]), alpha)
        x2d = _moe(x2d, i, w_rg[i], b_rg[i], w_re[i], b_re[i], w_gate, w_up, w_down,
                   row(ln2_g[i]), row(ln2_b[i]), alpha)
    return x2d.reshape(batch, seq, d)
```

```python
import functools

import jax
import jax.numpy as jnp
from jax import lax
from jax.experimental import pallas as pl
from jax.experimental.pallas import tpu as pltpu

F32 = jnp.float32
BF16 = jnp.bfloat16

HEAD_DIM = 64
MEM_LEN = 256
MEM_HEADS = 4
MEM_WIDTH = MEM_HEADS * HEAD_DIM
CONV_WIDTH = 31
MOBA_BLOCK = 256
MOBA_TOPK = 3
HEAD_ROWS = HEAD_DIM + 16
MOBA_LOOKAHEAD = 6
N_GROUPS = 4
EXPERTS_PER_GROUP = 8
N_EXPERTS = N_GROUPS * EXPERTS_PER_GROUP
LN_EPS = 1e-5

SUBLANES = 8
TOKEN_TILE = 256
CONV_HALO = 32
CONV_CHUNK = 16
CONV_CHUNKS_PER_STEP = 2
IN_PROJ_CHUNK = 256
ROUTER_LANES = 128
ROUTER_ROWS = 40
ROUTER_OFF = N_GROUPS
DMA_UNROLL = 32
ROUTER_TILE = 512
DISPATCH_ROWS = 512
COMBINE_ROWS = 512
XS_RING = 3
SLOT_TILES_PER_STEP = 16
VMEM_LIMIT = 56 * 1024 * 1024

NEG_INF = float("-inf")
LOG2_E = 1.4426950408889634


def _ln(z, g, b):
    mu = jnp.mean(z, axis=-1, keepdims=True)
    zc = z - mu
    var = jnp.mean(zc * zc, axis=-1, keepdims=True)
    return zc * lax.rsqrt(var + LN_EPS) * g + b


def _dot(a, b):
    return jnp.dot(a, b, preferred_element_type=F32)


def _dot_nt(a, b):
    return lax.dot_general(a, b, (((1,), (1,)), ((), ())), preferred_element_type=F32)


def _split_bf16(x):
    hi = x.astype(BF16)
    lo = (x - hi.astype(F32)).astype(BF16)
    return hi, lo


def _memkv_kernel(mem_ref, w_ref, kbd_ref, vbd_ref):
    kv = _dot(mem_ref[...].astype(BF16), w_ref[...].astype(BF16))
    k_t = kv[:, 0:MEM_WIDTH].T
    v = kv[:, MEM_WIDTH:]
    r = lax.broadcasted_iota(jnp.int32, kbd_ref.shape, 0)
    c = lax.broadcasted_iota(jnp.int32, kbd_ref.shape, 1)
    kbd_ref[...] = jnp.where(r // HEAD_DIM == c // MEM_LEN,
                             jnp.concatenate([k_t] * MEM_HEADS, axis=1), 0.0).astype(BF16)
    r = lax.broadcasted_iota(jnp.int32, vbd_ref.shape, 0)
    c = lax.broadcasted_iota(jnp.int32, vbd_ref.shape, 1)
    vbd_ref[...] = jnp.where(r // MEM_LEN == c // HEAD_DIM,
                             jnp.concatenate([v] * MEM_HEADS, axis=0), 0.0).astype(BF16)


def _memkv(mem, w):
    batch, m, d = mem.shape
    return pl.pallas_call(
        _memkv_kernel,
        out_shape=(jax.ShapeDtypeStruct((batch, MEM_WIDTH, MEM_HEADS * m), BF16),
                   jax.ShapeDtypeStruct((batch, MEM_HEADS * m, MEM_WIDTH), BF16)),
        grid=(batch,),
        in_specs=[pl.BlockSpec((None, m, d), lambda b: (b, 0, 0)),
                  pl.BlockSpec(w.shape, lambda b: (0, 0))],
        out_specs=(pl.BlockSpec((None, MEM_WIDTH, MEM_HEADS * m), lambda b: (b, 0, 0)),
                   pl.BlockSpec((None, MEM_HEADS * m, MEM_WIDTH), lambda b: (b, 0, 0))),
        compiler_params=pltpu.CompilerParams(dimension_semantics=("arbitrary",)),
        name="memkv",
    )(mem, w)


def _mem_attention(qm, kbd, vbd):
    s = _dot(qm.astype(BF16), kbd) * (HEAD_DIM ** -0.5)
    parts = []
    for h in range(MEM_HEADS):
        seg = s[:, h * MEM_LEN:(h + 1) * MEM_LEN]
        m = jnp.max(seg, axis=-1, keepdims=True)
        e = jnp.exp(seg - m)
        parts.append(e / jnp.sum(e, axis=-1, keepdims=True))
    p = jnp.concatenate(parts, axis=-1)
    return _dot(p.astype(BF16), vbd)


def _out_proj_ln(x, y_mix, y_mem, wo_ref, g_ref, b_ref, alpha, mix_w):
    y = _dot(y_mix.astype(BF16), wo_ref[0:mix_w, :]) + _dot(y_mem.astype(BF16), wo_ref[mix_w:, :])
    return _ln(alpha * x + y, g_ref[...], b_ref[...])


def _conv_mixer_kernel(x_ref, xn_ref, win_ref, dww_ref, dwb_ref, cg_ref, cb_ref, kbd_ref, vbd_ref,
                       wo_ref, g1_ref, b1_ref, o_ref, ustage, xn_sc, hbuf, zbuf, cbuf, *, alpha, mix_w):
    tm = x_ref.shape[0]
    n_cc = win_ref.shape[0]
    per = mix_w // IN_PROJ_CHUNK
    j = pl.program_id(1)

    @pl.when((pl.program_id(0) == 0) & (j == 0))
    def _():
        xb = x_ref[...].astype(BF16)
        for c in range(n_cc):
            ustage[c] = _dot(xb, win_ref[c])

    x = x_ref[...]
    qm = ustage[2 * per]
    h = jnp.concatenate([ustage[c] * jax.nn.sigmoid(ustage[per + c]) for c in range(per)], axis=1)
    xn_sc[...] = xn_ref[...].astype(BF16)

    @pl.when(j == 0)
    def _():
        hbuf[0:CONV_HALO, :] = jnp.zeros((CONV_HALO, mix_w), F32)

    hbuf[CONV_HALO:CONV_HALO + tm, :] = h

    shifted_rows = CONV_HALO + tm - SUBLANES
    for a in range(1, SUBLANES):
        zbuf[a - 1, 0:shifted_rows, :] = hbuf[a:a + shifted_rows, :]

    first = CONV_HALO - (CONV_WIDTH - 1)
    aligned = lambda v, m: v if isinstance(v, int) else pl.multiple_of(v, m)

    def conv_rows(r0):
        acc = jnp.broadcast_to(dwb_ref[...], (CONV_CHUNK, mix_w))
        for a in range(SUBLANES):
            offs = [first + k - a for k in range(CONV_WIDTH) if (first + k) % SUBLANES == a]
            r = aligned(r0 + offs[0], SUBLANES)
            span = offs[-1] - offs[0] + CONV_CHUNK
            z = hbuf[pl.ds(r, span), :] if a == 0 else zbuf[a - 1, pl.ds(r, span), :]
            for off in offs:
                k = off + a - first
                rows = z[off - offs[0]:off - offs[0] + CONV_CHUNK, :]
                acc = acc + jnp.tile(dww_ref[k], (CONV_CHUNK // SUBLANES, 1)) * rows
        cbuf[pl.ds(r0, CONV_CHUNK), :] = acc

    def project_next(c):
        ustage[c] = _dot(xn_sc[...], win_ref[c])

    step_rows = CONV_CHUNK * CONV_CHUNKS_PER_STEP
    paired = min(n_cc, tm // step_rows)

    def paired_step(c, carry):
        project_next(c)
        for q in range(CONV_CHUNKS_PER_STEP):
            conv_rows(pl.multiple_of(c * step_rows + q * CONV_CHUNK, CONV_CHUNK))
        return carry

    lax.fori_loop(0, paired, paired_step, 0)
    for r0 in range(paired * step_rows, tm, CONV_CHUNK):
        conv_rows(r0)
    for c in range(paired, n_cc):
        project_next(c)

    hbuf[0:CONV_HALO, :] = hbuf[tm:tm + CONV_HALO, :]

    cn = _ln(cbuf[...], cg_ref[...], cb_ref[...])
    y_mix = cn * jax.nn.sigmoid(cn)
    y_mem = _mem_attention(qm, kbd_ref[...], vbd_ref[...])
    o_ref[...] = _out_proj_ln(x, y_mix, y_mem, wo_ref, g1_ref, b1_ref, alpha, mix_w)


def _conv_mixer(x2d, batch, w_in, dw_w, dw_b, cg, cb, kbd, vbd, wo, g1, b1, alpha):
    t, d = x2d.shape
    tm = TOKEN_TILE
    nj = t // batch // tm
    mix_w = d - MEM_WIDTH
    full = lambda shape: pl.BlockSpec(shape, lambda b, j: (0,) * len(shape))
    return pl.pallas_call(
        functools.partial(_conv_mixer_kernel, alpha=alpha, mix_w=mix_w),
        out_shape=jax.ShapeDtypeStruct((t, d), F32),
        grid=(batch, nj),
        in_specs=[
            pl.BlockSpec((tm, d), lambda b, j: (b * nj + j, 0)),
            pl.BlockSpec((tm, d), lambda b, j: (jnp.minimum(b * nj + j + 1, batch * nj - 1), 0)),
            full(w_in.shape), full(dw_w.shape), full(dw_b.shape), full(cg.shape), full(cb.shape),
            pl.BlockSpec((None,) + kbd.shape[1:], lambda b, j: (b, 0, 0)),
            pl.BlockSpec((None,) + vbd.shape[1:], lambda b, j: (b, 0, 0)),
            full(wo.shape), full(g1.shape), full(b1.shape),
        ],
        out_specs=pl.BlockSpec((tm, d), lambda b, j: (b * nj + j, 0)),
        scratch_shapes=[pltpu.VMEM((w_in.shape[0], tm, IN_PROJ_CHUNK), F32),
                        pltpu.VMEM((tm, d), BF16),
                        pltpu.VMEM((CONV_HALO + tm, mix_w), F32),
                        pltpu.VMEM((SUBLANES - 1, CONV_HALO + tm, mix_w), F32),
                        pltpu.VMEM((tm, mix_w), F32)],
        compiler_params=pltpu.CompilerParams(
            dimension_semantics=("arbitrary", "arbitrary"), vmem_limit_bytes=VMEM_LIMIT),
        name="conv_mixer",
    )(x2d, x2d, w_in, dw_w, dw_b, cg, cb, kbd, vbd, wo, g1, b1)


def _moba_select_bias(gate, i):
    nb = gate.shape[1]
    blk = lax.broadcasted_iota(jnp.int32, gate.shape, 1).astype(F32)
    past = blk < i.astype(F32)
    gm = jnp.where(past, gate, NEG_INF)
    bias = jnp.full(gate.shape, NEG_INF, F32)
    for _ in range(MOBA_TOPK):
        top = jnp.max(gm, axis=1, keepdims=True)
        first = jnp.min(jnp.where(gm == top, blk, float(nb)), axis=1, keepdims=True)
        taken = blk == first
        bias = jnp.where(taken & past, 0.0, bias)
        gm = jnp.where(taken, NEG_INF, gm)
    return bias


def _moba_mixer_kernel(x_ref, xn_ref, win_ref, kbd_ref, vbd_ref, wo_ref, g1_ref, b1_ref, o_ref,
                       ustage, xn_sc, k_sc, vt_sc, kmt_sc, bias_sc, qh_sc, m_sc, acc_sc, yt_sc,
                       *, alpha, mix_w, nb):
    tm = x_ref.shape[0]
    heads = mix_w // HEAD_DIM
    n_cc = win_ref.shape[0]
    per = mix_w // IN_PROJ_CHUNK
    i = pl.program_id(1)

    @pl.when((pl.program_id(0) == 0) & (i == 0))
    def _():
        xb = x_ref[...].astype(BF16)
        for c in range(n_cc):
            ustage[c] = _dot(xb, win_ref[c])

    x = x_ref[...]
    part = lambda p: jnp.concatenate([ustage[p * per + c] for c in range(per)], axis=1)
    q = part(0) * (HEAD_DIM ** -0.5 * LOG2_E)
    k = part(1)
    v = part(2)
    qm = ustage[3 * per]
    xn_sc[...] = xn_ref[...].astype(BF16)

    def project_next(c):
        ustage[c] = _dot(xn_sc[...], win_ref[c])

    @pl.when(i == 0)
    def _():
        kmt_sc[...] = jnp.zeros(kmt_sc.shape, F32)

    k_sc[i] = k.astype(BF16)
    v_t = v.T.astype(BF16)
    for h in range(heads):
        vt_sc[i, h * HEAD_ROWS:h * HEAD_ROWS + HEAD_DIM, :] = v_t[h * HEAD_DIM:(h + 1) * HEAD_DIM, :]
        vt_sc[i, h * HEAD_ROWS + HEAD_DIM:(h + 1) * HEAD_ROWS, :] = jnp.ones((HEAD_ROWS - HEAD_DIM, tm), BF16)
    kmean = jnp.mean(k, axis=0, keepdims=True)
    lane = lax.broadcasted_iota(jnp.int32, (1, mix_w), 1)
    for h in range(heads):
        in_head = (lane >= h * HEAD_DIM) & (lane < (h + 1) * HEAD_DIM)
        kmt_sc[pl.ds(h * nb + i, 1), :] = jnp.where(in_head, kmean, 0.0)

    q_hi, q_lo = _split_bf16(q)
    km_hi, km_lo = _split_bf16(kmt_sc[...])
    gate_t = _dot_nt(km_hi, q_hi) + _dot_nt(km_hi, q_lo) + _dot_nt(km_lo, q_hi)
    bias_sc[...] = _moba_select_bias(gate_t.reshape(heads, nb, tm), i)

    lane_p = lax.broadcasted_iota(jnp.int32, (tm, 2 * HEAD_DIM), 1)
    for h in range(heads):
        qp = q_hi[:, (h // 2) * 2 * HEAD_DIM:(h // 2 + 1) * 2 * HEAD_DIM]
        keep = (lane_p < HEAD_DIM) if h % 2 == 0 else (lane_p >= HEAD_DIM)
        qh_sc[h] = jnp.where(keep, qp, jnp.zeros_like(qp))

    def scores(j, h):
        cols = slice((h // 2) * 2 * HEAD_DIM, (h // 2 + 1) * 2 * HEAD_DIM)
        return _dot_nt(k_sc[j, :, cols], qh_sc[h])

    rows = lambda h: slice(h * HEAD_ROWS, (h + 1) * HEAD_ROWS)

    kidx = lax.broadcasted_iota(jnp.int32, (tm, tm), 0)
    qidx = lax.broadcasted_iota(jnp.int32, (tm, tm), 1)
    causal = kidx <= qidx
    def heads_pipelined(j):
        pending = [scores(j, h) for h in range(MOBA_LOOKAHEAD)]
        for h in range(heads):
            if h + MOBA_LOOKAHEAD < heads:
                pending.append(scores(j, h + MOBA_LOOKAHEAD))
            yield h, pending.pop(0)

    def own_block(j, c):
        for h, s in heads_pipelined(j):
            if h < n_cc:
                project_next(h)
            s = jnp.where(causal, s, NEG_INF)
            m = jnp.max(s, axis=0, keepdims=True)
            e = jnp.exp2(s - m)
            m_sc[h] = m
            acc_sc[rows(h), :] = _dot(vt_sc[j, rows(h), :], e.astype(BF16))
        return c

    lax.fori_loop(i, i + 1, own_block, 0)
    for c in range(heads, n_cc):
        project_next(c)

    def body(j, c):
        for h, s in heads_pipelined(j):
            b = bias_sc[h, pl.ds(j, 1), :]
            m_old = m_sc[h]
            m_new = jnp.maximum(m_old, jnp.max(s, axis=0, keepdims=True) + b)
            e = jnp.exp2(s - (m_new - b))
            corr = jnp.exp2(m_old - m_new)
            m_sc[h] = m_new
            acc_sc[rows(h), :] = corr * acc_sc[rows(h), :] + _dot(vt_sc[j, rows(h), :], e.astype(BF16))
        return c

    lax.fori_loop(0, i, body, 0)
    for h in range(heads):
        r0 = h * HEAD_ROWS
        yt_sc[h * HEAD_DIM:(h + 1) * HEAD_DIM, :] = (acc_sc[r0:r0 + HEAD_DIM, :]
                                                     / acc_sc[r0 + HEAD_DIM:r0 + HEAD_DIM + 1, :])

    y_mix = yt_sc[...].T
    y_mem = _mem_attention(qm, kbd_ref[...], vbd_ref[...])
    o_ref[...] = _out_proj_ln(x, y_mix, y_mem, wo_ref, g1_ref, b1_ref, alpha, mix_w)


def _moba_mixer(x2d, batch, w_in, kbd, vbd, wo, g1, b1, alpha):
    t, d = x2d.shape
    tm = MOBA_BLOCK
    nb = t // batch // tm
    mix_w = d - MEM_WIDTH
    heads = mix_w // HEAD_DIM
    full = lambda shape: pl.BlockSpec(shape, lambda b, j: (0,) * len(shape))
    return pl.pallas_call(
        functools.partial(_moba_mixer_kernel, alpha=alpha, mix_w=mix_w, nb=nb),
        out_shape=jax.ShapeDtypeStruct((t, d), F32),
        grid=(batch, nb),
        in_specs=[
            pl.BlockSpec((tm, d), lambda b, j: (b * nb + j, 0)),
            pl.BlockSpec((tm, d), lambda b, j: (jnp.minimum(b * nb + j + 1, batch * nb - 1), 0)),
            full(w_in.shape),
            pl.BlockSpec((None,) + kbd.shape[1:], lambda b, j: (b, 0, 0)),
            pl.BlockSpec((None,) + vbd.shape[1:], lambda b, j: (b, 0, 0)),
            full(wo.shape), full(g1.shape), full(b1.shape),
        ],
        out_specs=pl.BlockSpec((tm, d), lambda b, j: (b * nb + j, 0)),
        scratch_shapes=[
            pltpu.VMEM((w_in.shape[0], tm, IN_PROJ_CHUNK), F32),
            pltpu.VMEM((tm, d), BF16),
            pltpu.VMEM((nb, tm, mix_w), BF16),
            pltpu.VMEM((nb, heads * HEAD_ROWS, tm), BF16),
            pltpu.VMEM((heads * nb, mix_w), F32),
            pltpu.VMEM((heads, nb, tm), F32),
            pltpu.VMEM((heads, tm, 2 * HEAD_DIM), BF16),
            pltpu.VMEM((heads, 1, tm), F32),
            pltpu.VMEM((heads * HEAD_ROWS, tm), F32),
            pltpu.VMEM((mix_w, tm), F32),
        ],
        compiler_params=pltpu.CompilerParams(
            dimension_semantics=("arbitrary", "arbitrary"), vmem_limit_bytes=VMEM_LIMIT),
        name="moba_mixer",
    )(x2d, x2d, w_in, kbd, vbd, wo, g1, b1)


def _router_kernel(x_ref, whi_ref, wlo_ref, br_ref, tri_ref, info_ref, sel_ref, cnt_ref, run_sc):
    step = pl.program_id(0)

    @pl.when(step == 0)
    def _():
        run_sc[...] = jnp.zeros(run_sc.shape, F32)

    x_hi, x_lo = _split_bf16(x_ref[...])
    logits = (_dot_nt(whi_ref[...], x_hi) + _dot_nt(wlo_ref[...], x_hi) + _dot_nt(whi_ref[...], x_lo)
              + br_ref[...])[0:ROUTER_ROWS, :]
    row = lax.broadcasted_iota(jnp.int32, logits.shape, 0).astype(F32)
    cmax = lambda a: jnp.max(a, axis=0, keepdims=True)
    cmin = lambda a: jnp.min(a, axis=0, keepdims=True)
    csum = lambda a: jnp.sum(a, axis=0, keepdims=True)
    big = float(2 * ROUTER_ROWS)

    is_g = row < N_GROUPS
    gl = jnp.where(is_g, logits, NEG_INF)
    gmax = cmax(gl)
    gidx = cmin(jnp.where(gl == gmax, row, big))
    g_w = 1.0 / csum(jnp.where(is_g, jnp.exp(gl - gmax), 0.0))

    lo = ROUTER_OFF + EXPERTS_PER_GROUP * gidx
    el = jnp.where((row >= lo) & (row < lo + EXPERTS_PER_GROUP), logits, NEG_INF)
    v0 = cmax(el)
    i0 = cmin(jnp.where(el == v0, row, big))
    el1 = jnp.where(row == i0, NEG_INF, el)
    v1 = cmax(el1)
    i1 = cmin(jnp.where(el1 == v1, row, big))
    t = jnp.exp(v1 - v0)
    w0 = g_w / (1.0 + t)
    w1 = g_w * t / (1.0 + t)

    pick0 = row == i0
    pick1 = row == i1
    onehot = jnp.where(pick0 | pick1, 1.0, 0.0)
    before = _dot(onehot.astype(BF16), tri_ref[...]) + run_sc[...]
    pos0 = csum(jnp.where(pick0, before, 0.0))
    pos1 = csum(jnp.where(pick1, before, 0.0))
    run_sc[...] = run_sc[...] + jnp.sum(onehot, axis=1, keepdims=True)
    cnt_ref[...] = run_sc[...]

    vals = (i0 - ROUTER_OFF, i1 - ROUTER_OFF, pos0, pos1, w0, w1)
    row8 = lax.broadcasted_iota(jnp.int32, sel_ref.shape, 0)
    sel = jnp.zeros(sel_ref.shape, F32)
    for c, val in enumerate(vals):
        sel = jnp.where(row8 == c, val, sel)
    sel_ref[...] = sel
    rowl = lax.broadcasted_iota(jnp.int32, (ROUTER_LANES, x_ref.shape[0]), 0)
    info_ref[...] = jnp.where(rowl == 4, w0, jnp.where(rowl == 5, w1, 0.0)).T


def _router(x2d, whi, wlo, br, tri):
    t, d = x2d.shape
    tm = tri.shape[0]
    full = lambda shape: pl.BlockSpec(shape, lambda s: (0,) * len(shape))
    return pl.pallas_call(
        _router_kernel,
        out_shape=(jax.ShapeDtypeStruct((t, ROUTER_LANES), F32),
                   jax.ShapeDtypeStruct((t // tm, SUBLANES, tm), F32),
                   jax.ShapeDtypeStruct((ROUTER_ROWS, tm), F32)),
        grid=(t // tm,),
        in_specs=[pl.BlockSpec((tm, d), lambda s: (s, 0)),
                  full(whi.shape), full(wlo.shape), full(br.shape), full(tri.shape)],
        out_specs=(pl.BlockSpec((tm, ROUTER_LANES), lambda s: (s, 0)),
                   pl.BlockSpec((None, SUBLANES, tm), lambda s: (s, 0, 0)),
                   full((ROUTER_ROWS, tm))),
        scratch_shapes=[pltpu.VMEM((ROUTER_ROWS, tm), F32)],
        compiler_params=pltpu.CompilerParams(dimension_semantics=("arbitrary",)),
        name="router",
    )(x2d, whi, wlo, br, tri)


def _row_copy(src, src_row, dst, dst_row, sem):
    return pltpu.make_async_copy(src.at[pl.ds(src_row, 1), :], dst.at[pl.ds(dst_row, 1), :], sem)


def _slots_kernel(sel_ref, base_ref, o_ref):
    sel = sel_ref[...]
    g, _, tm = sel.shape
    row = lax.broadcasted_iota(jnp.int32, (g, ROUTER_ROWS, tm), 1).astype(F32)
    base = base_ref[...][None]
    csum = lambda a: jnp.sum(a, axis=1, keepdims=True)
    s0 = csum(jnp.where(row == sel[:, 0:1, :] + ROUTER_OFF, base, 0.0)) + sel[:, 2:3, :]
    s1 = csum(jnp.where(row == sel[:, 1:2, :] + ROUTER_OFF, base, 0.0)) + sel[:, 3:4, :]
    row8 = lax.broadcasted_iota(jnp.int32, o_ref.shape, 1)
    o_ref[...] = jnp.where(row8 == 0, s0, jnp.where(row8 == 1, s1, 0.0)).astype(jnp.int32)


def _slots(sel, base_col):
    nt, _, tm = sel.shape
    g = SLOT_TILES_PER_STEP if nt % SLOT_TILES_PER_STEP == 0 else nt
    return pl.pallas_call(
        _slots_kernel,
        out_shape=jax.ShapeDtypeStruct((nt, SUBLANES, tm), jnp.int32),
        grid=(nt // g,),
        in_specs=[pl.BlockSpec((g, SUBLANES, tm), lambda s: (s, 0, 0)),
                  pl.BlockSpec(base_col.shape, lambda s: (0, 0))],
        out_specs=pl.BlockSpec((g, SUBLANES, tm), lambda s: (s, 0, 0)),
        compiler_params=pltpu.CompilerParams(dimension_semantics=("arbitrary",)),
        name="slots",
    )(sel, base_col)


def _dispatch_kernel(nv_ref, s0_ref, s1_ref, x_ref, xs_ref, zero_sc, sem, fill_sem, *, tm):

    @pl.when(pl.program_id(0) == 0)
    def _():
        zero_sc[...] = jnp.zeros(zero_sc.shape, F32)

        def fill(t):
            return pltpu.make_async_copy(zero_sc, xs_ref.at[pl.ds(pl.multiple_of(t * tm, tm), tm), :], fill_sem)

        def start_fill(t, c):
            @pl.when(nv_ref[t] < tm)
            def _():
                fill(t).start()
            return c

        def wait_fill(t, c):
            @pl.when(nv_ref[t] < tm)
            def _():
                fill(t).wait()
            return c

        lax.fori_loop(0, nv_ref.shape[0], start_fill, 0)
        lax.fori_loop(0, nv_ref.shape[0], wait_fill, 0)

    rows = x_ref.shape[0]
    for g in range(s0_ref.shape[0]):
        def start(r, c, g=g):
            row = g * s0_ref.shape[2] + r
            _row_copy(x_ref, row, xs_ref, s0_ref[g, 0, r], sem).start(priority=0)
            _row_copy(x_ref, row, xs_ref, s1_ref[g, 0, r], sem).start(priority=1)
            return c

        lax.fori_loop(0, s0_ref.shape[2], start, 0, unroll=DMA_UNROLL)
    for _ in range(2):
        pltpu.make_async_copy(x_ref, xs_ref.at[pl.ds(0, rows), :], sem).wait()


def _dispatch(x2d, slot0, slot1, tile_valid):
    t, d = x2d.shape
    tm = TOKEN_TILE
    st = slot0.shape[2]
    rows = DISPATCH_ROWS if t % DISPATCH_ROWS == 0 else st
    n_slots = tile_valid.shape[0] * tm
    smem_rows = pl.BlockSpec((rows // st, 1, st), lambda s, nv: (s, 0, 0), memory_space=pltpu.SMEM)
    return pl.pallas_call(
        functools.partial(_dispatch_kernel, tm=tm),
        out_shape=jax.ShapeDtypeStruct((n_slots, d), F32),
        grid_spec=pltpu.PrefetchScalarGridSpec(
            num_scalar_prefetch=1,
            grid=(t // rows,),
            in_specs=[smem_rows, smem_rows, pl.BlockSpec((rows, d), lambda s, nv: (s, 0))],
            out_specs=pl.BlockSpec(memory_space=pl.ANY),
            scratch_shapes=[pltpu.VMEM((tm, d), F32), pltpu.SemaphoreType.DMA, pltpu.SemaphoreType.DMA],
        ),
        compiler_params=pltpu.CompilerParams(dimension_semantics=("arbitrary",)),
        name="dispatch",
    )(tile_valid, slot0, slot1, x2d)


def _expert_kernel(te_ref, nv_ref, first_ref, buf_ref, next_ref, xs_hbm, wg_hbm, wu_hbm, wd_hbm, ys_ref,
                   xs_buf, wg_buf, wu_buf, wd_buf, wg_sc, wu_sc, wd_sc, xs_sems, sems, *, base):
    t = pl.program_id(0)
    n_steps = pl.num_programs(0)
    tm = ys_ref.shape[0]
    used = nv_ref[t] > 0

    def xs_copy(tile):
        row0 = tile * tm if isinstance(tile, int) else pl.multiple_of(tile * tm, tm)
        slot = tile % XS_RING
        return pltpu.make_async_copy(xs_hbm.at[pl.ds(row0, tm), :], xs_buf.at[slot], xs_sems.at[slot])

    @pl.when(t == 0)
    def _():
        for ahead in range(XS_RING - 1):
            @pl.when(ahead < n_steps)
            def _():
                xs_copy(ahead).start()

    @pl.when(t + XS_RING - 1 < n_steps)
    def _():
        xs_copy(t + XS_RING - 1).start()

    xs_copy(t).wait()

    def weight_copies(e, b):
        return (pltpu.make_async_copy(wg_hbm.at[base + e], wg_buf.at[b], sems.at[b, 0]),
                pltpu.make_async_copy(wu_hbm.at[base + e], wu_buf.at[b], sems.at[b, 1]),
                pltpu.make_async_copy(wd_hbm.at[base + e], wd_buf.at[b], sems.at[b, 2]))

    @pl.when(t == 0)
    def _():
        for c in weight_copies(te_ref[0], 0):
            c.start()

    @pl.when(first_ref[t] == 1)
    def _():
        b = buf_ref[t]
        for c in weight_copies(te_ref[t], b):
            c.wait()
        wg_sc[...] = wg_buf[b].astype(BF16)
        wu_sc[...] = wu_buf[b].astype(BF16)
        wd_sc[...] = wd_buf[b].astype(BF16)

        @pl.when(next_ref[t] >= 0)
        def _():
            for c in weight_copies(next_ref[t], 1 - b):
                c.start()

    @pl.when(used)
    def _():
        xb = xs_buf[t % XS_RING].astype(BF16)
        hg = _dot(xb, wg_sc[...])
        hu = _dot(xb, wu_sc[...])
        h = hg * jax.nn.sigmoid(hg) * hu
        ys_ref[...] = _dot(h.astype(BF16), wd_sc[...])

    @pl.when(jnp.logical_not(used))
    def _():
        ys_ref[...] = jnp.zeros(ys_ref.shape, F32)


def _expert_mlp(xs, tile_expert, tile_valid, tile_first, tile_buf, tile_next, w_gate, w_up, w_down, layer):
    ns, d = xs.shape
    f = w_gate.shape[-1]
    tm = TOKEN_TILE
    hbm = pl.BlockSpec(memory_space=pl.ANY)
    return pl.pallas_call(
        functools.partial(_expert_kernel, base=layer * N_EXPERTS),
        out_shape=jax.ShapeDtypeStruct((ns, d), F32),
        grid_spec=pltpu.PrefetchScalarGridSpec(
            num_scalar_prefetch=5,
            grid=(ns // tm,),
            in_specs=[hbm, hbm, hbm, hbm],
            out_specs=pl.BlockSpec((tm, d), lambda t, *_: (t, 0)),
            scratch_shapes=[pltpu.VMEM((XS_RING, tm, d), F32),
                            pltpu.VMEM((2, d, f), F32), pltpu.VMEM((2, d, f), F32), pltpu.VMEM((2, f, d), F32),
                            pltpu.VMEM((d, f), BF16), pltpu.VMEM((d, f), BF16), pltpu.VMEM((f, d), BF16),
                            pltpu.SemaphoreType.DMA((XS_RING,)), pltpu.SemaphoreType.DMA((2, 3))],
        ),
        compiler_params=pltpu.CompilerParams(
            dimension_semantics=("arbitrary",), vmem_limit_bytes=VMEM_LIMIT),
        name="expert_mlp",
    )(tile_expert, tile_valid, tile_first, tile_buf, tile_next, xs, w_gate, w_up, w_down)


def _combine_kernel(s0_ref, s1_ref, n0_ref, n1_ref, info_ref, x_ref, ys_ref, g_ref, b_ref, o_ref,
                    y0_sc, y1_sc, sems, *, alpha):
    s = pl.program_id(0)
    rows = x_ref.shape[0]

    def start_gathers(t0_ref, t1_ref, buf):
        for g in range(t0_ref.shape[0]):
            def start(r, c, g=g):
                row = g * t0_ref.shape[2] + r
                _row_copy(ys_ref, t0_ref[g, 0, r], y0_sc.at[buf], row, sems.at[buf]).start(priority=0)
                _row_copy(ys_ref, t1_ref[g, 0, r], y1_sc.at[buf], row, sems.at[buf]).start(priority=1)
                return c

            lax.fori_loop(0, t0_ref.shape[2], start, 0, unroll=DMA_UNROLL)

    cur = s % 2

    @pl.when(s == 0)
    def _():
        start_gathers(s0_ref, s1_ref, 0)

    @pl.when(s + 1 < pl.num_programs(0))
    def _():
        start_gathers(n0_ref, n1_ref, 1 - cur)

    for dst in (y0_sc, y1_sc):
        pltpu.make_async_copy(ys_ref.at[pl.ds(0, rows), :], dst.at[cur], sems.at[cur]).wait()
    info = info_ref[...]
    f = info[:, 4:5] * y0_sc[cur] + info[:, 5:6] * y1_sc[cur]
    o_ref[...] = _ln(alpha * x_ref[...] + f, g_ref[...], b_ref[...])


def _combine(x2d, info, ys, slot0, slot1, g2, b2, alpha):
    t, d = x2d.shape
    st = slot0.shape[2]
    rows = COMBINE_ROWS if t % COMBINE_ROWS == 0 else st
    n = t // rows
    smem_rows = pl.BlockSpec((rows // st, 1, st), lambda s: (s, 0, 0), memory_space=pltpu.SMEM)
    smem_next = pl.BlockSpec((rows // st, 1, st), lambda s: (jnp.minimum(s + 1, n - 1), 0, 0),
                             memory_space=pltpu.SMEM)
    full = lambda shape: pl.BlockSpec(shape, lambda s: (0,) * len(shape))
    return pl.pallas_call(
        functools.partial(_combine_kernel, alpha=alpha),
        out_shape=jax.ShapeDtypeStruct((t, d), F32),
        grid=(n,),
        in_specs=[smem_rows, smem_rows, smem_next, smem_next,
                  pl.BlockSpec((rows, ROUTER_LANES), lambda s: (s, 0)),
                  pl.BlockSpec((rows, d), lambda s: (s, 0)),
                  pl.BlockSpec(memory_space=pl.ANY),
                  full(g2.shape), full(b2.shape)],
        out_specs=pl.BlockSpec((rows, d), lambda s: (s, 0)),
        scratch_shapes=[pltpu.VMEM((2, rows, d), F32), pltpu.VMEM((2, rows, d), F32),
                        pltpu.SemaphoreType.DMA((2,))],
        compiler_params=pltpu.CompilerParams(
            dimension_semantics=("arbitrary",), vmem_limit_bytes=VMEM_LIMIT),
        name="combine",
    )(slot0, slot1, slot0, slot1, info, x2d, ys, g2, b2)


def _moe(x2d, layer, w_rg, b_rg, w_re, b_re, w_gate, w_up, w_down, g2, b2, alpha):
    t, d = x2d.shape
    tm = TOKEN_TILE
    wr = jnp.zeros((ROUTER_LANES, d), F32).at[0:N_GROUPS].set(w_rg.T)
    wr = wr.at[ROUTER_OFF:ROUTER_OFF + N_EXPERTS].set(w_re.T)
    br = jnp.zeros((ROUTER_LANES, 1), F32).at[0:N_GROUPS, 0].set(b_rg)
    br = br.at[ROUTER_OFF:ROUTER_OFF + N_EXPERTS, 0].set(b_re)
    whi, wlo = _split_bf16(wr)
    rt = ROUTER_TILE if t % ROUTER_TILE == 0 else tm
    ridx = lax.broadcasted_iota(jnp.int32, (rt, rt), 0)
    cidx = lax.broadcasted_iota(jnp.int32, (rt, rt), 1)
    tri = (ridx < cidx).astype(BF16)
    info, sel, cnt = _router(x2d, whi, wlo, br, tri)

    counts = cnt[ROUTER_OFF:ROUTER_OFF + N_EXPERTS, 0].astype(jnp.int32)
    padded = (counts + tm - 1) // tm * tm
    ends = jnp.cumsum(padded)
    base = ends - padded
    base_col = jnp.zeros((ROUTER_ROWS, 1), F32).at[ROUTER_OFF:ROUTER_OFF + N_EXPERTS, 0].set(base.astype(F32))
    slots = _slots(sel, base_col)
    slot0, slot1 = slots[:, 0:1, :], slots[:, 1:2, :]

    n_slots = 2 * t + N_EXPERTS * tm
    n_tiles = n_slots // tm
    n_used = ends[-1] // tm
    tile_ids = jnp.arange(n_tiles, dtype=jnp.int32)
    tile_expert = jnp.sum((jnp.minimum(tile_ids, n_used - 1)[:, None] * tm >= ends[None, :]).astype(jnp.int32),
                          axis=1)
    lo = jnp.maximum(base[None, :], tile_ids[:, None] * tm)
    hi = jnp.minimum((base + counts)[None, :], (tile_ids[:, None] + 1) * tm)
    tile_valid = jnp.sum(jnp.maximum(hi - lo, 0), axis=1).astype(jnp.int32)

    used = tile_valid > 0
    tile_first = (used & ((tile_ids == 0) | (tile_expert != jnp.roll(tile_expert, 1)))).astype(jnp.int32)
    tile_buf = ((jnp.cumsum(tile_first) - 1) % 2).astype(jnp.int32)
    experts = jnp.arange(N_EXPERTS, dtype=jnp.int32)
    later_nonempty = (experts[None, :] > experts[:, None]) & (counts[None, :] > 0)
    next_expert = jnp.min(jnp.where(later_nonempty, experts[None, :], N_EXPERTS), axis=1)
    next_expert = jnp.where(next_expert < N_EXPERTS, next_expert, -1)
    tile_next = jnp.sum(jnp.where(tile_expert[:, None] == experts[None, :], next_expert[None, :], 0),
                        axis=1).astype(jnp.int32)

    xs = _dispatch(x2d, slot0, slot1, tile_valid)
    ys = _expert_mlp(xs, tile_expert, tile_valid, tile_first, tile_buf, tile_next, w_gate, w_up, w_down, layer)
    return _combine(x2d, info, ys, slot0, slot1, g2, b2, alpha)


def kernel(x, mem, w_mem_kv, conv_w_in, conv_dw_w, conv_dw_b, conv_ln_g, conv_ln_b, moba_w_in, w_o,
           ln1_g, ln1_b, w_rg, b_rg, w_re, b_re, w_gate, w_up, w_down, ln2_g, ln2_b):
    batch, seq, d = x.shape
    depth = w_o.shape[0]
    alpha = (2 * depth) ** 0.25
    t = batch * seq
    row = lambda a: a.reshape(1, -1)

    kbd, vbd = _memkv(mem, w_mem_kv)

    e_shape = w_gate.shape
    w_gate = w_gate.reshape((-1,) + e_shape[-2:])
    w_up = w_up.reshape((-1,) + e_shape[-2:])
    w_down = w_down.reshape((-1,) + w_down.shape[-2:])

    x2d = x.reshape(t, d)
    for i in range(depth):
        j = i // 2
        wo = w_o[i].astype(BF16)
        if i % 2 == 0:
            dw_w = jnp.broadcast_to(conv_dw_w[j][:, None, :], (CONV_WIDTH, SUBLANES, conv_dw_w.shape[-1]))
            w_in = conv_w_in[j].astype(BF16).reshape(d, -1, IN_PROJ_CHUNK).transpose(1, 0, 2)
            x2d = _conv_mixer(x2d, batch, w_in, dw_w, row(conv_dw_b[j]),
                              row(conv_ln_g[j]), row(conv_ln_b[j]), kbd, vbd, wo,
                              row(ln1_g[i]), row(ln1_b[i]), alpha)
        else:
            w_in = moba_w_in[j].astype(BF16).reshape(d, -1, IN_PROJ_CHUNK).transpose(1, 0, 2)
            x2d = _moba_mixer(x2d, batch, w_in, kbd, vbd, wo,
                              row(ln1_g[i]), row(ln1_b[i]), alpha)
        x2d = _moe(x2d, i, w_rg[i], b_rg[i], w_re[i], b_re[i], w_gate, w_up, w_down,
                   row(ln2_g[i]), row(ln2_b[i]), alpha)
    return x2d.reshape(batch, seq, d)
```

```python
import functools

import jax
import jax.numpy as jnp
from jax import lax
from jax.experimental import pallas as pl
from jax.experimental.pallas import tpu as pltpu

F32 = jnp.float32
BF16 = jnp.bfloat16

HEAD_DIM = 64
MEM_LEN = 256
MEM_HEADS = 4
MEM_WIDTH = MEM_HEADS * HEAD_DIM
CONV_WIDTH = 31
MOBA_BLOCK = 256
MOBA_TOPK = 3
HEAD_ROWS = HEAD_DIM + 16
MOBA_BLOCKS_PER_TRIP = 4
MOBA_LOOKAHEAD = 6
N_GROUPS = 4
EXPERTS_PER_GROUP = 8
N_EXPERTS = N_GROUPS * EXPERTS_PER_GROUP
LN_EPS = 1e-5

SUBLANES = 8
TOKEN_TILE = 256
CONV_HALO = 32
CONV_CHUNK = 16
CONV_CHUNKS_PER_STEP = 2
IN_PROJ_CHUNK = 256
ROUTER_LANES = 128
ROUTER_ROWS = 40
ROUTER_OFF = N_GROUPS
DMA_UNROLL = 32
ROUTER_TILE = 512
DISPATCH_ROWS = 512
COMBINE_ROWS = 512
XS_RING = 3
SLOT_TILES_PER_STEP = 16
VMEM_LIMIT = 56 * 1024 * 1024

NEG_INF = float("-inf")
LOG2_E = 1.4426950408889634


def _ln(z, g, b):
    mu = jnp.mean(z, axis=-1, keepdims=True)
    zc = z - mu
    var = jnp.mean(zc * zc, axis=-1, keepdims=True)
    return zc * lax.rsqrt(var + LN_EPS) * g + b


def _dot(a, b):
    return jnp.dot(a, b, preferred_element_type=F32)


def _dot_nt(a, b):
    return lax.dot_general(a, b, (((1,), (1,)), ((), ())), preferred_element_type=F32)


def _split_bf16(x):
    hi = x.astype(BF16)
    lo = (x - hi.astype(F32)).astype(BF16)
    return hi, lo


def _memkv_kernel(mem_ref, w_ref, kbd_ref, vbd_ref):
    kv = _dot(mem_ref[...].astype(BF16), w_ref[...].astype(BF16))
    k_t = kv[:, 0:MEM_WIDTH].T
    v = kv[:, MEM_WIDTH:]
    r = lax.broadcasted_iota(jnp.int32, kbd_ref.shape, 0)
    c = lax.broadcasted_iota(jnp.int32, kbd_ref.shape, 1)
    kbd_ref[...] = jnp.where(r // HEAD_DIM == c // MEM_LEN,
                             jnp.concatenate([k_t] * MEM_HEADS, axis=1), 0.0).astype(BF16)
    r = lax.broadcasted_iota(jnp.int32, vbd_ref.shape, 0)
    c = lax.broadcasted_iota(jnp.int32, vbd_ref.shape, 1)
    vbd_ref[...] = jnp.where(r // MEM_LEN == c // HEAD_DIM,
                             jnp.concatenate([v] * MEM_HEADS, axis=0), 0.0).astype(BF16)


def _memkv(mem, w):
    batch, m, d = mem.shape
    return pl.pallas_call(
        _memkv_kernel,
        out_shape=(jax.ShapeDtypeStruct((batch, MEM_WIDTH, MEM_HEADS * m), BF16),
                   jax.ShapeDtypeStruct((batch, MEM_HEADS * m, MEM_WIDTH), BF16)),
        grid=(batch,),
        in_specs=[pl.BlockSpec((None, m, d), lambda b: (b, 0, 0)),
                  pl.BlockSpec(w.shape, lambda b: (0, 0))],
        out_specs=(pl.BlockSpec((None, MEM_WIDTH, MEM_HEADS * m), lambda b: (b, 0, 0)),
                   pl.BlockSpec((None, MEM_HEADS * m, MEM_WIDTH), lambda b: (b, 0, 0))),
        compiler_params=pltpu.CompilerParams(dimension_semantics=("arbitrary",)),
        name="memkv",
    )(mem, w)


def _mem_attention(qm, kbd, vbd):
    s = _dot(qm.astype(BF16), kbd) * (HEAD_DIM ** -0.5)
    parts = []
    for h in range(MEM_HEADS):
        seg = s[:, h * MEM_LEN:(h + 1) * MEM_LEN]
        m = jnp.max(seg, axis=-1, keepdims=True)
        e = jnp.exp(seg - m)
        parts.append(e / jnp.sum(e, axis=-1, keepdims=True))
    p = jnp.concatenate(parts, axis=-1)
    return _dot(p.astype(BF16), vbd)


def _out_proj_ln(x, y_mix, y_mem, wo_ref, g_ref, b_ref, alpha, mix_w):
    y = _dot(y_mix.astype(BF16), wo_ref[0:mix_w, :]) + _dot(y_mem.astype(BF16), wo_ref[mix_w:, :])
    return _ln(alpha * x + y, g_ref[...], b_ref[...])


def _conv_mixer_kernel(x_ref, xn_ref, win_ref, dww_ref, dwb_ref, cg_ref, cb_ref, kbd_ref, vbd_ref,
                       wo_ref, g1_ref, b1_ref, o_ref, ustage, xn_sc, hbuf, zbuf, cbuf, *, alpha, mix_w):
    tm = x_ref.shape[0]
    n_cc = win_ref.shape[0]
    per = mix_w // IN_PROJ_CHUNK
    j = pl.program_id(1)

    @pl.when((pl.program_id(0) == 0) & (j == 0))
    def _():
        xb = x_ref[...].astype(BF16)
        for c in range(n_cc):
            ustage[c] = _dot(xb, win_ref[c])

    x = x_ref[...]
    qm = ustage[2 * per]
    h = jnp.concatenate([ustage[c] * jax.nn.sigmoid(ustage[per + c]) for c in range(per)], axis=1)
    xn_sc[...] = xn_ref[...].astype(BF16)

    @pl.when(j == 0)
    def _():
        hbuf[0:CONV_HALO, :] = jnp.zeros((CONV_HALO, mix_w), F32)

    hbuf[CONV_HALO:CONV_HALO + tm, :] = h

    shifted_rows = CONV_HALO + tm - SUBLANES
    for a in range(1, SUBLANES):
        zbuf[a - 1, 0:shifted_rows, :] = hbuf[a:a + shifted_rows, :]

    first = CONV_HALO - (CONV_WIDTH - 1)
    aligned = lambda v, m: v if isinstance(v, int) else pl.multiple_of(v, m)

    def conv_rows(r0):
        acc = jnp.broadcast_to(dwb_ref[...], (CONV_CHUNK, mix_w))
        for a in range(SUBLANES):
            offs = [first + k - a for k in range(CONV_WIDTH) if (first + k) % SUBLANES == a]
            r = aligned(r0 + offs[0], SUBLANES)
            span = offs[-1] - offs[0] + CONV_CHUNK
            z = hbuf[pl.ds(r, span), :] if a == 0 else zbuf[a - 1, pl.ds(r, span), :]
            for off in offs:
                k = off + a - first
                rows = z[off - offs[0]:off - offs[0] + CONV_CHUNK, :]
                acc = acc + jnp.tile(dww_ref[k], (CONV_CHUNK // SUBLANES, 1)) * rows
        cbuf[pl.ds(r0, CONV_CHUNK), :] = acc

    def project_next(c):
        ustage[c] = _dot(xn_sc[...], win_ref[c])

    step_rows = CONV_CHUNK * CONV_CHUNKS_PER_STEP
    paired = min(n_cc, tm // step_rows)

    def paired_step(c, carry):
        project_next(c)
        for q in range(CONV_CHUNKS_PER_STEP):
            conv_rows(pl.multiple_of(c * step_rows + q * CONV_CHUNK, CONV_CHUNK))
        return carry

    lax.fori_loop(0, paired, paired_step, 0)
    for r0 in range(paired * step_rows, tm, CONV_CHUNK):
        conv_rows(r0)
    for c in range(paired, n_cc):
        project_next(c)

    hbuf[0:CONV_HALO, :] = hbuf[tm:tm + CONV_HALO, :]

    cn = _ln(cbuf[...], cg_ref[...], cb_ref[...])
    y_mix = cn * jax.nn.sigmoid(cn)
    y_mem = _mem_attention(qm, kbd_ref[...], vbd_ref[...])
    o_ref[...] = _out_proj_ln(x, y_mix, y_mem, wo_ref, g1_ref, b1_ref, alpha, mix_w)


def _conv_mixer(x2d, batch, w_in, dw_w, dw_b, cg, cb, kbd, vbd, wo, g1, b1, alpha):
    t, d = x2d.shape
    tm = TOKEN_TILE
    nj = t // batch // tm
    mix_w = d - MEM_WIDTH
    full = lambda shape: pl.BlockSpec(shape, lambda b, j: (0,) * len(shape))
    return pl.pallas_call(
        functools.partial(_conv_mixer_kernel, alpha=alpha, mix_w=mix_w),
        out_shape=jax.ShapeDtypeStruct((t, d), F32),
        grid=(batch, nj),
        in_specs=[
            pl.BlockSpec((tm, d), lambda b, j: (b * nj + j, 0)),
            pl.BlockSpec((tm, d), lambda b, j: (jnp.minimum(b * nj + j + 1, batch * nj - 1), 0)),
            full(w_in.shape), full(dw_w.shape), full(dw_b.shape), full(cg.shape), full(cb.shape),
            pl.BlockSpec((None,) + kbd.shape[1:], lambda b, j: (b, 0, 0)),
            pl.BlockSpec((None,) + vbd.shape[1:], lambda b, j: (b, 0, 0)),
            full(wo.shape), full(g1.shape), full(b1.shape),
        ],
        out_specs=pl.BlockSpec((tm, d), lambda b, j: (b * nj + j, 0)),
        scratch_shapes=[pltpu.VMEM((w_in.shape[0], tm, IN_PROJ_CHUNK), F32),
                        pltpu.VMEM((tm, d), BF16),
                        pltpu.VMEM((CONV_HALO + tm, mix_w), F32),
                        pltpu.VMEM((SUBLANES - 1, CONV_HALO + tm, mix_w), F32),
                        pltpu.VMEM((tm, mix_w), F32)],
        compiler_params=pltpu.CompilerParams(
            dimension_semantics=("arbitrary", "arbitrary"), vmem_limit_bytes=VMEM_LIMIT),
        name="conv_mixer",
    )(x2d, x2d, w_in, dw_w, dw_b, cg, cb, kbd, vbd, wo, g1, b1)


def _moba_select_bias(gate, i):
    nb = gate.shape[1]
    blk = lax.broadcasted_iota(jnp.int32, gate.shape, 1).astype(F32)
    past = blk < i.astype(F32)
    gm = jnp.where(past, gate, NEG_INF)
    bias = jnp.full(gate.shape, NEG_INF, F32)
    for _ in range(MOBA_TOPK):
        top = jnp.max(gm, axis=1, keepdims=True)
        first = jnp.min(jnp.where(gm == top, blk, float(nb)), axis=1, keepdims=True)
        taken = blk == first
        bias = jnp.where(taken & past, 0.0, bias)
        gm = jnp.where(taken, NEG_INF, gm)
    return bias


def _moba_mixer_kernel(x_ref, xn_ref, win_ref, kbd_ref, vbd_ref, wo_ref, g1_ref, b1_ref, o_ref,
                       ustage, xn_sc, k_sc, vt_sc, kmt_sc, bias_sc, qh_sc, m_sc, acc_sc, yt_sc,
                       *, alpha, mix_w, nb):
    tm = x_ref.shape[0]
    heads = mix_w // HEAD_DIM
    n_cc = win_ref.shape[0]
    per = mix_w // IN_PROJ_CHUNK
    i = pl.program_id(1)

    @pl.when((pl.program_id(0) == 0) & (i == 0))
    def _():
        xb = x_ref[...].astype(BF16)
        for c in range(n_cc):
            ustage[c] = _dot(xb, win_ref[c])

    x = x_ref[...]
    part = lambda p: jnp.concatenate([ustage[p * per + c] for c in range(per)], axis=1)
    q = part(0) * (HEAD_DIM ** -0.5 * LOG2_E)
    k = part(1)
    v = part(2)
    qm = ustage[3 * per]
    xn_sc[...] = xn_ref[...].astype(BF16)

    def project_next(c):
        ustage[c] = _dot(xn_sc[...], win_ref[c])

    @pl.when(i == 0)
    def _():
        kmt_sc[...] = jnp.zeros(kmt_sc.shape, F32)

    k_sc[i] = k.astype(BF16)
    v_t = v.T.astype(BF16)
    for h in range(heads):
        vt_sc[i, h * HEAD_ROWS:h * HEAD_ROWS + HEAD_DIM, :] = v_t[h * HEAD_DIM:(h + 1) * HEAD_DIM, :]
        vt_sc[i, h * HEAD_ROWS + HEAD_DIM:(h + 1) * HEAD_ROWS, :] = jnp.ones((HEAD_ROWS - HEAD_DIM, tm), BF16)
    kmean = jnp.mean(k, axis=0, keepdims=True)
    lane = lax.broadcasted_iota(jnp.int32, (1, mix_w), 1)
    for h in range(heads):
        in_head = (lane >= h * HEAD_DIM) & (lane < (h + 1) * HEAD_DIM)
        kmt_sc[pl.ds(h * nb + i, 1), :] = jnp.where(in_head, kmean, 0.0)

    q_hi, q_lo = _split_bf16(q)
    km_hi, km_lo = _split_bf16(kmt_sc[...])
    gate_t = _dot_nt(km_hi, q_hi) + _dot_nt(km_hi, q_lo) + _dot_nt(km_lo, q_hi)
    bias_sc[...] = _moba_select_bias(gate_t.reshape(heads, nb, tm), i)

    lane_p = lax.broadcasted_iota(jnp.int32, (tm, 2 * HEAD_DIM), 1)
    for h in range(heads):
        qp = q_hi[:, (h // 2) * 2 * HEAD_DIM:(h // 2 + 1) * 2 * HEAD_DIM]
        keep = (lane_p < HEAD_DIM) if h % 2 == 0 else (lane_p >= HEAD_DIM)
        qh_sc[h] = jnp.where(keep, qp, jnp.zeros_like(qp))

    def scores(j, h):
        cols = slice((h // 2) * 2 * HEAD_DIM, (h // 2 + 1) * 2 * HEAD_DIM)
        return _dot_nt(k_sc[j, :, cols], qh_sc[h])

    rows = lambda h: slice(h * HEAD_ROWS, (h + 1) * HEAD_ROWS)

    kidx = lax.broadcasted_iota(jnp.int32, (tm, tm), 0)
    qidx = lax.broadcasted_iota(jnp.int32, (tm, tm), 1)
    causal = kidx <= qidx
    def heads_pipelined(*blocks):
        items = [(j, h) for j in blocks for h in range(heads)]
        pending = [scores(j, h) for j, h in items[:MOBA_LOOKAHEAD]]
        for n, (j, h) in enumerate(items):
            if n + MOBA_LOOKAHEAD < len(items):
                pending.append(scores(*items[n + MOBA_LOOKAHEAD]))
            yield j, h, pending.pop(0)

    def own_block(j, c):
        for _, h, s in heads_pipelined(j):
            if h < n_cc:
                project_next(h)
            s = jnp.where(causal, s, NEG_INF)
            m = jnp.max(s, axis=0, keepdims=True)
            e = jnp.exp2(s - m)
            m_sc[h] = m
            acc_sc[rows(h), :] = _dot(vt_sc[j, rows(h), :], e.astype(BF16))
        return c

    lax.fori_loop(i, i + 1, own_block, 0)
    for c in range(heads, n_cc):
        project_next(c)

    def attend(*blocks):
        for j, h, s in heads_pipelined(*blocks):
            b = bias_sc[h, pl.ds(j, 1), :]
            m_old = m_sc[h]
            m_new = jnp.maximum(m_old, jnp.max(s, axis=0, keepdims=True) + b)
            e = jnp.exp2(s - (m_new - b))
            corr = jnp.exp2(m_old - m_new)
            m_sc[h] = m_new
            acc_sc[rows(h), :] = corr * acc_sc[rows(h), :] + _dot(vt_sc[j, rows(h), :], e.astype(BF16))

    done = 0
    group = MOBA_BLOCKS_PER_TRIP
    while group >= 1:
        trips = lax.div(i - done, group)

        def body(t, c, done=done, group=group):
            attend(*[done + t * group + g for g in range(group)])
            return c

        lax.fori_loop(0, trips, body, 0)
        done = done + trips * group
        group //= 2
    for h in range(heads):
        r0 = h * HEAD_ROWS
        yt_sc[h * HEAD_DIM:(h + 1) * HEAD_DIM, :] = (acc_sc[r0:r0 + HEAD_DIM, :]
                                                     / acc_sc[r0 + HEAD_DIM:r0 + HEAD_DIM + 1, :])

    y_mix = yt_sc[...].T
    y_mem = _mem_attention(qm, kbd_ref[...], vbd_ref[...])
    o_ref[...] = _out_proj_ln(x, y_mix, y_mem, wo_ref, g1_ref, b1_ref, alpha, mix_w)


def _moba_mixer(x2d, batch, w_in, kbd, vbd, wo, g1, b1, alpha):
    t, d = x2d.shape
    tm = MOBA_BLOCK
    nb = t // batch // tm
    mix_w = d - MEM_WIDTH
    heads = mix_w // HEAD_DIM
    full = lambda shape: pl.BlockSpec(shape, lambda b, j: (0,) * len(shape))
    return pl.pallas_call(
        functools.partial(_moba_mixer_kernel, alpha=alpha, mix_w=mix_w, nb=nb),
        out_shape=jax.ShapeDtypeStruct((t, d), F32),
        grid=(batch, nb),
        in_specs=[
            pl.BlockSpec((tm, d), lambda b, j: (b * nb + j, 0)),
            pl.BlockSpec((tm, d), lambda b, j: (jnp.minimum(b * nb + j + 1, batch * nb - 1), 0)),
            full(w_in.shape),
            pl.BlockSpec((None,) + kbd.shape[1:], lambda b, j: (b, 0, 0)),
            pl.BlockSpec((None,) + vbd.shape[1:], lambda b, j: (b, 0, 0)),
            full(wo.shape), full(g1.shape), full(b1.shape),
        ],
        out_specs=pl.BlockSpec((tm, d), lambda b, j: (b * nb + j, 0)),
        scratch_shapes=[
            pltpu.VMEM((w_in.shape[0], tm, IN_PROJ_CHUNK), F32),
            pltpu.VMEM((tm, d), BF16),
            pltpu.VMEM((nb, tm, mix_w), BF16),
            pltpu.VMEM((nb, heads * HEAD_ROWS, tm), BF16),
            pltpu.VMEM((heads * nb, mix_w), F32),
            pltpu.VMEM((heads, nb, tm), F32),
            pltpu.VMEM((heads, tm, 2 * HEAD_DIM), BF16),
            pltpu.VMEM((heads, 1, tm), F32),
            pltpu.VMEM((heads * HEAD_ROWS, tm), F32),
            pltpu.VMEM((mix_w, tm), F32),
        ],
        compiler_params=pltpu.CompilerParams(
            dimension_semantics=("arbitrary", "arbitrary"), vmem_limit_bytes=VMEM_LIMIT),
        name="moba_mixer",
    )(x2d, x2d, w_in, kbd, vbd, wo, g1, b1)


def _router_kernel(x_ref, whi_ref, wlo_ref, br_ref, tri_ref, info_ref, sel_ref, cnt_ref, run_sc):
    step = pl.program_id(0)

    @pl.when(step == 0)
    def _():
        run_sc[...] = jnp.zeros(run_sc.shape, F32)

    x_hi, x_lo = _split_bf16(x_ref[...])
    logits = (_dot_nt(whi_ref[...], x_hi) + _dot_nt(wlo_ref[...], x_hi) + _dot_nt(whi_ref[...], x_lo)
              + br_ref[...])[0:ROUTER_ROWS, :]
    row = lax.broadcasted_iota(jnp.int32, logits.shape, 0).astype(F32)
    cmax = lambda a: jnp.max(a, axis=0, keepdims=True)
    cmin = lambda a: jnp.min(a, axis=0, keepdims=True)
    csum = lambda a: jnp.sum(a, axis=0, keepdims=True)
    big = float(2 * ROUTER_ROWS)

    is_g = row < N_GROUPS
    gl = jnp.where(is_g, logits, NEG_INF)
    gmax = cmax(gl)
    gidx = cmin(jnp.where(gl == gmax, row, big))
    g_w = 1.0 / csum(jnp.where(is_g, jnp.exp(gl - gmax), 0.0))

    lo = ROUTER_OFF + EXPERTS_PER_GROUP * gidx
    el = jnp.where((row >= lo) & (row < lo + EXPERTS_PER_GROUP), logits, NEG_INF)
    v0 = cmax(el)
    i0 = cmin(jnp.where(el == v0, row, big))
    el1 = jnp.where(row == i0, NEG_INF, el)
    v1 = cmax(el1)
    i1 = cmin(jnp.where(el1 == v1, row, big))
    t = jnp.exp(v1 - v0)
    w0 = g_w / (1.0 + t)
    w1 = g_w * t / (1.0 + t)

    pick0 = row == i0
    pick1 = row == i1
    onehot = jnp.where(pick0 | pick1, 1.0, 0.0)
    before = _dot(onehot.astype(BF16), tri_ref[...]) + run_sc[...]
    pos0 = csum(jnp.where(pick0, before, 0.0))
    pos1 = csum(jnp.where(pick1, before, 0.0))
    run_sc[...] = run_sc[...] + jnp.sum(onehot, axis=1, keepdims=True)
    cnt_ref[...] = run_sc[...]

    vals = (i0 - ROUTER_OFF, i1 - ROUTER_OFF, pos0, pos1, w0, w1)
    row8 = lax.broadcasted_iota(jnp.int32, sel_ref.shape, 0)
    sel = jnp.zeros(sel_ref.shape, F32)
    for c, val in enumerate(vals):
        sel = jnp.where(row8 == c, val, sel)
    sel_ref[...] = sel
    rowl = lax.broadcasted_iota(jnp.int32, (ROUTER_LANES, x_ref.shape[0]), 0)
    info_ref[...] = jnp.where(rowl == 4, w0, jnp.where(rowl == 5, w1, 0.0)).T


def _router(x2d, whi, wlo, br, tri):
    t, d = x2d.shape
    tm = tri.shape[0]
    full = lambda shape: pl.BlockSpec(shape, lambda s: (0,) * len(shape))
    return pl.pallas_call(
        _router_kernel,
        out_shape=(jax.ShapeDtypeStruct((t, ROUTER_LANES), F32),
                   jax.ShapeDtypeStruct((t // tm, SUBLANES, tm), F32),
                   jax.ShapeDtypeStruct((ROUTER_ROWS, tm), F32)),
        grid=(t // tm,),
        in_specs=[pl.BlockSpec((tm, d), lambda s: (s, 0)),
                  full(whi.shape), full(wlo.shape), full(br.shape), full(tri.shape)],
        out_specs=(pl.BlockSpec((tm, ROUTER_LANES), lambda s: (s, 0)),
                   pl.BlockSpec((None, SUBLANES, tm), lambda s: (s, 0, 0)),
                   full((ROUTER_ROWS, tm))),
        scratch_shapes=[pltpu.VMEM((ROUTER_ROWS, tm), F32)],
        compiler_params=pltpu.CompilerParams(dimension_semantics=("arbitrary",)),
        name="router",
    )(x2d, whi, wlo, br, tri)


def _row_copy(src, src_row, dst, dst_row, sem):
    return pltpu.make_async_copy(src.at[pl.ds(src_row, 1), :], dst.at[pl.ds(dst_row, 1), :], sem)


def _slots_kernel(sel_ref, base_ref, o_ref):
    sel = sel_ref[...]
    g, _, tm = sel.shape
    row = lax.broadcasted_iota(jnp.int32, (g, ROUTER_ROWS, tm), 1).astype(F32)
    base = base_ref[...][None]
    csum = lambda a: jnp.sum(a, axis=1, keepdims=True)
    s0 = csum(jnp.where(row == sel[:, 0:1, :] + ROUTER_OFF, base, 0.0)) + sel[:, 2:3, :]
    s1 = csum(jnp.where(row == sel[:, 1:2, :] + ROUTER_OFF, base, 0.0)) + sel[:, 3:4, :]
    row8 = lax.broadcasted_iota(jnp.int32, o_ref.shape, 1)
    o_ref[...] = jnp.where(row8 == 0, s0, jnp.where(row8 == 1, s1, 0.0)).astype(jnp.int32)


def _slots(sel, base_col):
    nt, _, tm = sel.shape
    g = SLOT_TILES_PER_STEP if nt % SLOT_TILES_PER_STEP == 0 else nt
    return pl.pallas_call(
        _slots_kernel,
        out_shape=jax.ShapeDtypeStruct((nt, SUBLANES, tm), jnp.int32),
        grid=(nt // g,),
        in_specs=[pl.BlockSpec((g, SUBLANES, tm), lambda s: (s, 0, 0)),
                  pl.BlockSpec(base_col.shape, lambda s: (0, 0))],
        out_specs=pl.BlockSpec((g, SUBLANES, tm), lambda s: (s, 0, 0)),
        compiler_params=pltpu.CompilerParams(dimension_semantics=("arbitrary",)),
        name="slots",
    )(sel, base_col)


def _dispatch_kernel(nv_ref, s0_ref, s1_ref, x_ref, xs_ref, zero_sc, sem, fill_sem, *, tm):

    @pl.when(pl.program_id(0) == 0)
    def _():
        zero_sc[...] = jnp.zeros(zero_sc.shape, F32)

        def fill(t):
            return pltpu.make_async_copy(zero_sc, xs_ref.at[pl.ds(pl.multiple_of(t * tm, tm), tm), :], fill_sem)

        def start_fill(t, c):
            @pl.when(nv_ref[t] < tm)
            def _():
                fill(t).start()
            return c

        def wait_fill(t, c):
            @pl.when(nv_ref[t] < tm)
            def _():
                fill(t).wait()
            return c

        lax.fori_loop(0, nv_ref.shape[0], start_fill, 0)
        lax.fori_loop(0, nv_ref.shape[0], wait_fill, 0)

    rows = x_ref.shape[0]
    for g in range(s0_ref.shape[0]):
        def start(r, c, g=g):
            row = g * s0_ref.shape[2] + r
            _row_copy(x_ref, row, xs_ref, s0_ref[g, 0, r], sem).start(priority=0)
            _row_copy(x_ref, row, xs_ref, s1_ref[g, 0, r], sem).start(priority=1)
            return c

        lax.fori_loop(0, s0_ref.shape[2], start, 0, unroll=DMA_UNROLL)
    for _ in range(2):
        pltpu.make_async_copy(x_ref, xs_ref.at[pl.ds(0, rows), :], sem).wait()


def _dispatch(x2d, slot0, slot1, tile_valid):
    t, d = x2d.shape
    tm = TOKEN_TILE
    st = slot0.shape[2]
    rows = DISPATCH_ROWS if t % DISPATCH_ROWS == 0 else st
    n_slots = tile_valid.shape[0] * tm
    smem_rows = pl.BlockSpec((rows // st, 1, st), lambda s, nv: (s, 0, 0), memory_space=pltpu.SMEM)
    return pl.pallas_call(
        functools.partial(_dispatch_kernel, tm=tm),
        out_shape=jax.ShapeDtypeStruct((n_slots, d), F32),
        grid_spec=pltpu.PrefetchScalarGridSpec(
            num_scalar_prefetch=1,
            grid=(t // rows,),
            in_specs=[smem_rows, smem_rows, pl.BlockSpec((rows, d), lambda s, nv: (s, 0))],
            out_specs=pl.BlockSpec(memory_space=pl.ANY),
            scratch_shapes=[pltpu.VMEM((tm, d), F32), pltpu.SemaphoreType.DMA, pltpu.SemaphoreType.DMA],
        ),
        compiler_params=pltpu.CompilerParams(dimension_semantics=("arbitrary",)),
        name="dispatch",
    )(tile_valid, slot0, slot1, x2d)


def _expert_kernel(te_ref, nv_ref, first_ref, buf_ref, next_ref, xs_hbm, wg_hbm, wu_hbm, wd_hbm, ys_ref,
                   xs_buf, wg_buf, wu_buf, wd_buf, wg_sc, wu_sc, wd_sc, xs_sems, sems, *, base):
    t = pl.program_id(0)
    n_steps = pl.num_programs(0)
    tm = ys_ref.shape[0]
    used = nv_ref[t] > 0

    def xs_copy(tile):
        row0 = tile * tm if isinstance(tile, int) else pl.multiple_of(tile * tm, tm)
        slot = tile % XS_RING
        return pltpu.make_async_copy(xs_hbm.at[pl.ds(row0, tm), :], xs_buf.at[slot], xs_sems.at[slot])

    @pl.when(t == 0)
    def _():
        for ahead in range(XS_RING - 1):
            @pl.when(ahead < n_steps)
            def _():
                xs_copy(ahead).start()

    @pl.when(t + XS_RING - 1 < n_steps)
    def _():
        xs_copy(t + XS_RING - 1).start()

    xs_copy(t).wait()

    def weight_copies(e, b):
        return (pltpu.make_async_copy(wg_hbm.at[base + e], wg_buf.at[b], sems.at[b, 0]),
                pltpu.make_async_copy(wu_hbm.at[base + e], wu_buf.at[b], sems.at[b, 1]),
                pltpu.make_async_copy(wd_hbm.at[base + e], wd_buf.at[b], sems.at[b, 2]))

    @pl.when(t == 0)
    def _():
        for c in weight_copies(te_ref[0], 0):
            c.start()

    @pl.when(first_ref[t] == 1)
    def _():
        b = buf_ref[t]
        for c in weight_copies(te_ref[t], b):
            c.wait()
        wg_sc[...] = wg_buf[b].astype(BF16)
        wu_sc[...] = wu_buf[b].astype(BF16)
        wd_sc[...] = wd_buf[b].astype(BF16)

        @pl.when(next_ref[t] >= 0)
        def _():
            for c in weight_copies(next_ref[t], 1 - b):
                c.start()

    @pl.when(used)
    def _():
        xb = xs_buf[t % XS_RING].astype(BF16)
        hg = _dot(xb, wg_sc[...])
        hu = _dot(xb, wu_sc[...])
        h = hg * jax.nn.sigmoid(hg) * hu
        ys_ref[...] = _dot(h.astype(BF16), wd_sc[...])

    @pl.when(jnp.logical_not(used))
    def _():
        ys_ref[...] = jnp.zeros(ys_ref.shape, F32)


def _expert_mlp(xs, tile_expert, tile_valid, tile_first, tile_buf, tile_next, w_gate, w_up, w_down, layer):
    ns, d = xs.shape
    f = w_gate.shape[-1]
    tm = TOKEN_TILE
    hbm = pl.BlockSpec(memory_space=pl.ANY)
    return pl.pallas_call(
        functools.partial(_expert_kernel, base=layer * N_EXPERTS),
        out_shape=jax.ShapeDtypeStruct((ns, d), F32),
        grid_spec=pltpu.PrefetchScalarGridSpec(
            num_scalar_prefetch=5,
            grid=(ns // tm,),
            in_specs=[hbm, hbm, hbm, hbm],
            out_specs=pl.BlockSpec((tm, d), lambda t, *_: (t, 0)),
            scratch_shapes=[pltpu.VMEM((XS_RING, tm, d), F32),
                            pltpu.VMEM((2, d, f), F32), pltpu.VMEM((2, d, f), F32), pltpu.VMEM((2, f, d), F32),
                            pltpu.VMEM((d, f), BF16), pltpu.VMEM((d, f), BF16), pltpu.VMEM((f, d), BF16),
                            pltpu.SemaphoreType.DMA((XS_RING,)), pltpu.SemaphoreType.DMA((2, 3))],
        ),
        compiler_params=pltpu.CompilerParams(
            dimension_semantics=("arbitrary",), vmem_limit_bytes=VMEM_LIMIT),
        name="expert_mlp",
    )(tile_expert, tile_valid, tile_first, tile_buf, tile_next, xs, w_gate, w_up, w_down)


def _combine_kernel(s0_ref, s1_ref, n0_ref, n1_ref, info_ref, x_ref, ys_ref, g_ref, b_ref, o_ref,
                    y0_sc, y1_sc, sems, *, alpha):
    s = pl.program_id(0)
    rows = x_ref.shape[0]

    def start_gathers(t0_ref, t1_ref, buf):
        for g in range(t0_ref.shape[0]):
            def start(r, c, g=g):
                row = g * t0_ref.shape[2] + r
                _row_copy(ys_ref, t0_ref[g, 0, r], y0_sc.at[buf], row, sems.at[buf]).start(priority=0)
                _row_copy(ys_ref, t1_ref[g, 0, r], y1_sc.at[buf], row, sems.at[buf]).start(priority=1)
                return c

            lax.fori_loop(0, t0_ref.shape[2], start, 0, unroll=DMA_UNROLL)

    cur = s % 2

    @pl.when(s == 0)
    def _():
        start_gathers(s0_ref, s1_ref, 0)

    @pl.when(s + 1 < pl.num_programs(0))
    def _():
        start_gathers(n0_ref, n1_ref, 1 - cur)

    for dst in (y0_sc, y1_sc):
        pltpu.make_async_copy(ys_ref.at[pl.ds(0, rows), :], dst.at[cur], sems.at[cur]).wait()
    info = info_ref[...]
    f = info[:, 4:5] * y0_sc[cur] + info[:, 5:6] * y1_sc[cur]
    o_ref[...] = _ln(alpha * x_ref[...] + f, g_ref[...], b_ref[...])


def _combine(x2d, info, ys, slot0, slot1, g2, b2, alpha):
    t, d = x2d.shape
    st = slot0.shape[2]
    rows = COMBINE_ROWS if t % COMBINE_ROWS == 0 else st
    n = t // rows
    smem_rows = pl.BlockSpec((rows // st, 1, st), lambda s: (s, 0, 0), memory_space=pltpu.SMEM)
    smem_next = pl.BlockSpec((rows // st, 1, st), lambda s: (jnp.minimum(s + 1, n - 1), 0, 0),
                             memory_space=pltpu.SMEM)
    full = lambda shape: pl.BlockSpec(shape, lambda s: (0,) * len(shape))
    return pl.pallas_call(
        functools.partial(_combine_kernel, alpha=alpha),
        out_shape=jax.ShapeDtypeStruct((t, d), F32),
        grid=(n,),
        in_specs=[smem_rows, smem_rows, smem_next, smem_next,
                  pl.BlockSpec((rows, ROUTER_LANES), lambda s: (s, 0)),
                  pl.BlockSpec((rows, d), lambda s: (s, 0)),
                  pl.BlockSpec(memory_space=pl.ANY),
                  full(g2.shape), full(b2.shape)],
        out_specs=pl.BlockSpec((rows, d), lambda s: (s, 0)),
        scratch_shapes=[pltpu.VMEM((2, rows, d), F32), pltpu.VMEM((2, rows, d), F32),
                        pltpu.SemaphoreType.DMA((2,))],
        compiler_params=pltpu.CompilerParams(
            dimension_semantics=("arbitrary",), vmem_limit_bytes=VMEM_LIMIT),
        name="combine",
    )(slot0, slot1, slot0, slot1, info, x2d, ys, g2, b2)


def _moe(x2d, layer, w_rg, b_rg, w_re, b_re, w_gate, w_up, w_down, g2, b2, alpha):
    t, d = x2d.shape
    tm = TOKEN_TILE
    wr = jnp.zeros((ROUTER_LANES, d), F32).at[0:N_GROUPS].set(w_rg.T)
    wr = wr.at[ROUTER_OFF:ROUTER_OFF + N_EXPERTS].set(w_re.T)
    br = jnp.zeros((ROUTER_LANES, 1), F32).at[0:N_GROUPS, 0].set(b_rg)
    br = br.at[ROUTER_OFF:ROUTER_OFF + N_EXPERTS, 0].set(b_re)
    whi, wlo = _split_bf16(wr)
    rt = ROUTER_TILE if t % ROUTER_TILE == 0 else tm
    ridx = lax.broadcasted_iota(jnp.int32, (rt, rt), 0)
    cidx = lax.broadcasted_iota(jnp.int32, (rt, rt), 1)
    tri = (ridx < cidx).astype(BF16)
    info, sel, cnt = _router(x2d, whi, wlo, br, tri)

    counts = cnt[ROUTER_OFF:ROUTER_OFF + N_EXPERTS, 0].astype(jnp.int32)
    padded = (counts + tm - 1) // tm * tm
    ends = jnp.cumsum(padded)
    base = ends - padded
    base_col = jnp.zeros((ROUTER_ROWS, 1), F32).at[ROUTER_OFF:ROUTER_OFF + N_EXPERTS, 0].set(base.astype(F32))
    slots = _slots(sel, base_col)
    slot0, slot1 = slots[:, 0:1, :], slots[:, 1:2, :]

    n_slots = 2 * t + N_EXPERTS * tm
    n_tiles = n_slots // tm
    n_used = ends[-1] // tm
    tile_ids = jnp.arange(n_tiles, dtype=jnp.int32)
    tile_expert = jnp.sum((jnp.minimum(tile_ids, n_used - 1)[:, None] * tm >= ends[None, :]).astype(jnp.int32),
                          axis=1)
    lo = jnp.maximum(base[None, :], tile_ids[:, None] * tm)
    hi = jnp.minimum((base + counts)[None, :], (tile_ids[:, None] + 1) * tm)
    tile_valid = jnp.sum(jnp.maximum(hi - lo, 0), axis=1).astype(jnp.int32)

    used = tile_valid > 0
    tile_first = (used & ((tile_ids == 0) | (tile_expert != jnp.roll(tile_expert, 1)))).astype(jnp.int32)
    tile_buf = ((jnp.cumsum(tile_first) - 1) % 2).astype(jnp.int32)
    experts = jnp.arange(N_EXPERTS, dtype=jnp.int32)
    later_nonempty = (experts[None, :] > experts[:, None]) & (counts[None, :] > 0)
    next_expert = jnp.min(jnp.where(later_nonempty, experts[None, :], N_EXPERTS), axis=1)
    next_expert = jnp.where(next_expert < N_EXPERTS, next_expert, -1)
    tile_next = jnp.sum(jnp.where(tile_expert[:, None] == experts[None, :], next_expert[None, :], 0),
                        axis=1).astype(jnp.int32)

    xs = _dispatch(x2d, slot0, slot1, tile_valid)
    ys = _expert_mlp(xs, tile_expert, tile_valid, tile_first, tile_buf, tile_next, w_gate, w_up, w_down, layer)
    return _combine(x2d, info, ys, slot0, slot1, g2, b2, alpha)


def kernel(x, mem, w_mem_kv, conv_w_in, conv_dw_w, conv_dw_b, conv_ln_g, conv_ln_b, moba_w_in, w_o,
           ln1_g, ln1_b, w_rg, b_rg, w_re, b_re, w_gate, w_up, w_down, ln2_g, ln2_b):
    batch, seq, d = x.shape
    depth = w_o.shape[0]
    alpha = (2 * depth) ** 0.25
    t = batch * seq
    row = lambda a: a.reshape(1, -1)

    kbd, vbd = _memkv(mem, w_mem_kv)

    e_shape = w_gate.shape
    w_gate = w_gate.reshape((-1,) + e_shape[-2:])
    w_up = w_up.reshape((-1,) + e_shape[-2:])
    w_down = w_down.reshape((-1,) + w_down.shape[-2:])

    x2d = x.reshape(t, d)
    for i in range(depth):
        j = i // 2
        wo = w_o[i].astype(BF16)
        if i % 2 == 0:
            dw_w = jnp.broadcast_to(conv_dw_w[j][:, None, :], (CONV_WIDTH, SUBLANES, conv_dw_w.shape[-1]))
            w_in = conv_w_in[j].astype(BF16).reshape(d, -1, IN_PROJ_CHUNK).transpose(1, 0, 2)
            x2d = _conv_mixer(x2d, batch, w_in, dw_w, row(conv_dw_b[j]),
                              row(conv_ln_g[j]), row(conv_ln_b[j]), kbd, vbd, wo,
                              row(ln1_g[i]), row(ln1_b[i]), alpha)
        else:
            w_in = moba_w_in[j].astype(BF16).reshape(d, -1, IN_PROJ_CHUNK).transpose(1, 0, 2)
            x2d = _moba_mixer(x2d, batch, w_in, kbd, vbd, wo,
                              row(ln1_g[i]), row(ln1_b[i]), alpha)
        x2d = _moe(x2d, i, w_rg[i], b_rg[i], w_re[i], b_re[i], w_gate, w_up, w_down,
                   row(ln2_g[i]), row(ln2_b[i]), alpha)
    return x2d.reshape(batch, seq, d)
```

```python
import functools

import jax
import jax.numpy as jnp
from jax import lax
from jax.experimental import pallas as pl
from jax.experimental.pallas import tpu as pltpu

F32 = jnp.float32
BF16 = jnp.bfloat16

HEAD_DIM = 64
MEM_LEN = 256
MEM_HEADS = 4
MEM_WIDTH = MEM_HEADS * HEAD_DIM
CONV_WIDTH = 31
MOBA_BLOCK = 256
MOBA_TOPK = 3
HEAD_ROWS = HEAD_DIM + 16
MOBA_BLOCKS_PER_TRIP = 4
MOBA_LOOKAHEAD = 6
N_GROUPS = 4
EXPERTS_PER_GROUP = 8
N_EXPERTS = N_GROUPS * EXPERTS_PER_GROUP
LN_EPS = 1e-5

SUBLANES = 8
TOKEN_TILE = 256
CONV_HALO = 32
CONV_CHUNK = 16
CONV_CHUNKS_PER_STEP = 2
CONV_STEPS_PER_TRIP = 3
IN_PROJ_CHUNK = 256
ROUTER_LANES = 128
ROUTER_ROWS = 40
ROUTER_OFF = N_GROUPS
DMA_UNROLL = 32
ROUTER_TILE = 512
DISPATCH_ROWS = 512
COMBINE_ROWS = 512
XS_RING = 3
SLOT_TILES_PER_STEP = 16
VMEM_LIMIT = 56 * 1024 * 1024

NEG_INF = float("-inf")
LOG2_E = 1.4426950408889634


def _ln(z, g, b):
    mu = jnp.mean(z, axis=-1, keepdims=True)
    zc = z - mu
    var = jnp.mean(zc * zc, axis=-1, keepdims=True)
    return zc * lax.rsqrt(var + LN_EPS) * g + b


def _dot(a, b):
    return jnp.dot(a, b, preferred_element_type=F32)


def _dot_nt(a, b):
    return lax.dot_general(a, b, (((1,), (1,)), ((), ())), preferred_element_type=F32)


def _split_bf16(x):
    hi = x.astype(BF16)
    lo = (x - hi.astype(F32)).astype(BF16)
    return hi, lo


def _memkv_kernel(mem_ref, w_ref, kbd_ref, vbd_ref):
    kv = _dot(mem_ref[...].astype(BF16), w_ref[...].astype(BF16))
    k_t = kv[:, 0:MEM_WIDTH].T
    v = kv[:, MEM_WIDTH:]
    r = lax.broadcasted_iota(jnp.int32, kbd_ref.shape, 0)
    c = lax.broadcasted_iota(jnp.int32, kbd_ref.shape, 1)
    kbd_ref[...] = jnp.where(r // HEAD_DIM == c // MEM_LEN,
                             jnp.concatenate([k_t] * MEM_HEADS, axis=1), 0.0).astype(BF16)
    r = lax.broadcasted_iota(jnp.int32, vbd_ref.shape, 0)
    c = lax.broadcasted_iota(jnp.int32, vbd_ref.shape, 1)
    vbd_ref[...] = jnp.where(r // MEM_LEN == c // HEAD_DIM,
                             jnp.concatenate([v] * MEM_HEADS, axis=0), 0.0).astype(BF16)


def _memkv(mem, w):
    batch, m, d = mem.shape
    return pl.pallas_call(
        _memkv_kernel,
        out_shape=(jax.ShapeDtypeStruct((batch, MEM_WIDTH, MEM_HEADS * m), BF16),
                   jax.ShapeDtypeStruct((batch, MEM_HEADS * m, MEM_WIDTH), BF16)),
        grid=(batch,),
        in_specs=[pl.BlockSpec((None, m, d), lambda b: (b, 0, 0)),
                  pl.BlockSpec(w.shape, lambda b: (0, 0))],
        out_specs=(pl.BlockSpec((None, MEM_WIDTH, MEM_HEADS * m), lambda b: (b, 0, 0)),
                   pl.BlockSpec((None, MEM_HEADS * m, MEM_WIDTH), lambda b: (b, 0, 0))),
        compiler_params=pltpu.CompilerParams(dimension_semantics=("arbitrary",)),
        name="memkv",
    )(mem, w)


def _mem_attention(qm, kbd, vbd):
    s = _dot(qm.astype(BF16), kbd) * (HEAD_DIM ** -0.5)
    parts = []
    for h in range(MEM_HEADS):
        seg = s[:, h * MEM_LEN:(h + 1) * MEM_LEN]
        m = jnp.max(seg, axis=-1, keepdims=True)
        e = jnp.exp(seg - m)
        parts.append(e / jnp.sum(e, axis=-1, keepdims=True))
    p = jnp.concatenate(parts, axis=-1)
    return _dot(p.astype(BF16), vbd)


def _out_proj_ln(x, y_mix, y_mem, wo_ref, g_ref, b_ref, alpha, mix_w):
    y = _dot(y_mix.astype(BF16), wo_ref[0:mix_w, :]) + _dot(y_mem.astype(BF16), wo_ref[mix_w:, :])
    return _ln(alpha * x + y, g_ref[...], b_ref[...])


def _conv_mixer_kernel(x_ref, xn_ref, win_ref, dww_ref, dwb_ref, cg_ref, cb_ref, kbd_ref, vbd_ref,
                       wo_ref, g1_ref, b1_ref, o_ref, ustage, xn_sc, hbuf, zbuf, cbuf, *, alpha, mix_w):
    tm = x_ref.shape[0]
    n_cc = win_ref.shape[0]
    per = mix_w // IN_PROJ_CHUNK
    j = pl.program_id(1)

    @pl.when((pl.program_id(0) == 0) & (j == 0))
    def _():
        xb = x_ref[...].astype(BF16)
        for c in range(n_cc):
            ustage[c] = _dot(xb, win_ref[c])

    x = x_ref[...]
    qm = ustage[2 * per]
    h = jnp.concatenate([ustage[c] * jax.nn.sigmoid(ustage[per + c]) for c in range(per)], axis=1)
    xn_sc[...] = xn_ref[...].astype(BF16)

    @pl.when(j == 0)
    def _():
        hbuf[0:CONV_HALO, :] = jnp.zeros((CONV_HALO, mix_w), F32)

    hbuf[CONV_HALO:CONV_HALO + tm, :] = h

    shifted_rows = CONV_HALO + tm - SUBLANES
    for a in range(1, SUBLANES):
        zbuf[a - 1, 0:shifted_rows, :] = hbuf[a:a + shifted_rows, :]

    first = CONV_HALO - (CONV_WIDTH - 1)
    aligned = lambda v, m: v if isinstance(v, int) else pl.multiple_of(v, m)

    def conv_rows(r0):
        acc = jnp.broadcast_to(dwb_ref[...], (CONV_CHUNK, mix_w))
        for a in range(SUBLANES):
            offs = [first + k - a for k in range(CONV_WIDTH) if (first + k) % SUBLANES == a]
            r = aligned(r0 + offs[0], SUBLANES)
            span = offs[-1] - offs[0] + CONV_CHUNK
            z = hbuf[pl.ds(r, span), :] if a == 0 else zbuf[a - 1, pl.ds(r, span), :]
            for off in offs:
                k = off + a - first
                rows = z[off - offs[0]:off - offs[0] + CONV_CHUNK, :]
                acc = acc + jnp.tile(dww_ref[k], (CONV_CHUNK // SUBLANES, 1)) * rows
        cbuf[pl.ds(r0, CONV_CHUNK), :] = acc

    def project_next(c):
        ustage[c] = _dot(xn_sc[...], win_ref[c])

    step_rows = CONV_CHUNK * CONV_CHUNKS_PER_STEP
    paired = min(n_cc, tm // step_rows)

    def paired_step(c):
        project_next(c)
        for q in range(CONV_CHUNKS_PER_STEP):
            r0 = c * step_rows + q * CONV_CHUNK
            conv_rows(r0 if isinstance(c, int) else pl.multiple_of(r0, CONV_CHUNK))

    def trip(t, carry):
        for q in range(CONV_STEPS_PER_TRIP):
            paired_step(t * CONV_STEPS_PER_TRIP + q)
        return carry

    lax.fori_loop(0, paired // CONV_STEPS_PER_TRIP, trip, 0)

    def leftover(c, carry):
        paired_step(c)
        return carry

    first_left = paired - paired % CONV_STEPS_PER_TRIP
    lax.fori_loop(jnp.minimum(j, 0) + first_left, paired, leftover, 0)
    def tail_rows(n, carry):
        conv_rows(pl.multiple_of(n * CONV_CHUNK, CONV_CHUNK))
        return carry

    lax.fori_loop(jnp.minimum(j, 0) + paired * CONV_CHUNKS_PER_STEP, tm // CONV_CHUNK, tail_rows, 0)
    for c in range(paired, n_cc):
        project_next(c)

    hbuf[0:CONV_HALO, :] = hbuf[tm:tm + CONV_HALO, :]

    cn = _ln(cbuf[...], cg_ref[...], cb_ref[...])
    y_mix = cn * jax.nn.sigmoid(cn)
    y_mem = _mem_attention(qm, kbd_ref[...], vbd_ref[...])
    o_ref[...] = _out_proj_ln(x, y_mix, y_mem, wo_ref, g1_ref, b1_ref, alpha, mix_w)


def _conv_mixer(x2d, batch, w_in, dw_w, dw_b, cg, cb, kbd, vbd, wo, g1, b1, alpha):
    t, d = x2d.shape
    tm = TOKEN_TILE
    nj = t // batch // tm
    mix_w = d - MEM_WIDTH
    full = lambda shape: pl.BlockSpec(shape, lambda b, j: (0,) * len(shape))
    return pl.pallas_call(
        functools.partial(_conv_mixer_kernel, alpha=alpha, mix_w=mix_w),
        out_shape=jax.ShapeDtypeStruct((t, d), F32),
        grid=(batch, nj),
        in_specs=[
            pl.BlockSpec((tm, d), lambda b, j: (b * nj + j, 0)),
            pl.BlockSpec((tm, d), lambda b, j: (jnp.minimum(b * nj + j + 1, batch * nj - 1), 0)),
            full(w_in.shape), full(dw_w.shape), full(dw_b.shape), full(cg.shape), full(cb.shape),
            pl.BlockSpec((None,) + kbd.shape[1:], lambda b, j: (b, 0, 0)),
            pl.BlockSpec((None,) + vbd.shape[1:], lambda b, j: (b, 0, 0)),
            full(wo.shape), full(g1.shape), full(b1.shape),
        ],
        out_specs=pl.BlockSpec((tm, d), lambda b, j: (b * nj + j, 0)),
        scratch_shapes=[pltpu.VMEM((w_in.shape[0], tm, IN_PROJ_CHUNK), F32),
                        pltpu.VMEM((tm, d), BF16),
                        pltpu.VMEM((CONV_HALO + tm, mix_w), F32),
                        pltpu.VMEM((SUBLANES - 1, CONV_HALO + tm, mix_w), F32),
                        pltpu.VMEM((tm, mix_w), F32)],
        compiler_params=pltpu.CompilerParams(
            dimension_semantics=("arbitrary", "arbitrary"), vmem_limit_bytes=VMEM_LIMIT),
        name="conv_mixer",
    )(x2d, x2d, w_in, dw_w, dw_b, cg, cb, kbd, vbd, wo, g1, b1)


def _moba_select_bias(gate, i):
    nb = gate.shape[1]
    blk = lax.broadcasted_iota(jnp.int32, gate.shape, 1).astype(F32)
    past = blk < i.astype(F32)
    gm = jnp.where(past, gate, NEG_INF)
    bias = jnp.full(gate.shape, NEG_INF, F32)
    for _ in range(MOBA_TOPK):
        top = jnp.max(gm, axis=1, keepdims=True)
        first = jnp.min(jnp.where(gm == top, blk, float(nb)), axis=1, keepdims=True)
        taken = blk == first
        bias = jnp.where(taken & past, 0.0, bias)
        gm = jnp.where(taken, NEG_INF, gm)
    return bias


def _moba_mixer_kernel(x_ref, xn_ref, win_ref, kbd_ref, vbd_ref, wo_ref, g1_ref, b1_ref, o_ref,
                       ustage, xn_sc, k_sc, vt_sc, kmt_sc, bias_sc, qh_sc, m_sc, acc_sc, yt_sc,
                       *, alpha, mix_w, nb):
    tm = x_ref.shape[0]
    heads = mix_w // HEAD_DIM
    n_cc = win_ref.shape[0]
    per = mix_w // IN_PROJ_CHUNK
    i = pl.program_id(1)

    @pl.when((pl.program_id(0) == 0) & (i == 0))
    def _():
        xb = x_ref[...].astype(BF16)
        for c in range(n_cc):
            ustage[c] = _dot(xb, win_ref[c])

    x = x_ref[...]
    part = lambda p: jnp.concatenate([ustage[p * per + c] for c in range(per)], axis=1)
    q = part(0) * (HEAD_DIM ** -0.5 * LOG2_E)
    k = part(1)
    v = part(2)
    qm = ustage[3 * per]
    xn_sc[...] = xn_ref[...].astype(BF16)

    def project_next(c):
        ustage[c] = _dot(xn_sc[...], win_ref[c])

    @pl.when(i == 0)
    def _():
        kmt_sc[...] = jnp.zeros(kmt_sc.shape, F32)

    k_sc[i] = k.astype(BF16)
    v_t = v.T.astype(BF16)
    for h in range(heads):
        vt_sc[i, h * HEAD_ROWS:h * HEAD_ROWS + HEAD_DIM, :] = v_t[h * HEAD_DIM:(h + 1) * HEAD_DIM, :]
        vt_sc[i, h * HEAD_ROWS + HEAD_DIM:(h + 1) * HEAD_ROWS, :] = jnp.ones((HEAD_ROWS - HEAD_DIM, tm), BF16)
    kmean = jnp.mean(k, axis=0, keepdims=True)
    lane = lax.broadcasted_iota(jnp.int32, (1, mix_w), 1)
    for h in range(heads):
        in_head = (lane >= h * HEAD_DIM) & (lane < (h + 1) * HEAD_DIM)
        kmt_sc[pl.ds(h * nb + i, 1), :] = jnp.where(in_head, kmean, 0.0)

    q_hi, q_lo = _split_bf16(q)
    km_hi, km_lo = _split_bf16(kmt_sc[...])
    gate_t = _dot_nt(km_hi, q_hi) + _dot_nt(km_hi, q_lo) + _dot_nt(km_lo, q_hi)
    bias_sc[...] = _moba_select_bias(gate_t.reshape(heads, nb, tm), i)

    lane_p = lax.broadcasted_iota(jnp.int32, (tm, 2 * HEAD_DIM), 1)
    for h in range(heads):
        qp = q_hi[:, (h // 2) * 2 * HEAD_DIM:(h // 2 + 1) * 2 * HEAD_DIM]
        keep = (lane_p < HEAD_DIM) if h % 2 == 0 else (lane_p >= HEAD_DIM)
        qh_sc[h] = jnp.where(keep, qp, jnp.zeros_like(qp))

    def scores(j, h):
        cols = slice((h // 2) * 2 * HEAD_DIM, (h // 2 + 1) * 2 * HEAD_DIM)
        return _dot_nt(k_sc[j, :, cols], qh_sc[h])

    rows = lambda h: slice(h * HEAD_ROWS, (h + 1) * HEAD_ROWS)

    kidx = lax.broadcasted_iota(jnp.int32, (tm, tm), 0)
    qidx = lax.broadcasted_iota(jnp.int32, (tm, tm), 1)
    causal = kidx <= qidx
    def heads_pipelined(*blocks):
        items = [(j, h) for j in blocks for h in range(heads)]
        pending = [scores(j, h) for j, h in items[:MOBA_LOOKAHEAD]]
        for n, (j, h) in enumerate(items):
            if n + MOBA_LOOKAHEAD < len(items):
                pending.append(scores(*items[n + MOBA_LOOKAHEAD]))
            yield j, h, pending.pop(0)

    def own_block(j, c):
        for _, h, s in heads_pipelined(j):
            if h < n_cc:
                project_next(h)
            s = jnp.where(causal, s, NEG_INF)
            m = jnp.max(s, axis=0, keepdims=True)
            e = jnp.exp2(s - m)
            m_sc[h] = m
            acc_sc[rows(h), :] = _dot(vt_sc[j, rows(h), :], e.astype(BF16))
        return c

    lax.fori_loop(i, i + 1, own_block, 0)
    for c in range(heads, n_cc):
        project_next(c)

    def attend(*blocks):
        for j, h, s in heads_pipelined(*blocks):
            b = bias_sc[h, pl.ds(j, 1), :]
            m_old = m_sc[h]
            m_new = jnp.maximum(m_old, jnp.max(s, axis=0, keepdims=True) + b)
            e = jnp.exp2(s - (m_new - b))
            corr = jnp.exp2(m_old - m_new)
            m_sc[h] = m_new
            acc_sc[rows(h), :] = corr * acc_sc[rows(h), :] + _dot(vt_sc[j, rows(h), :], e.astype(BF16))

    done = 0
    group = MOBA_BLOCKS_PER_TRIP
    while group >= 1:
        trips = lax.div(i - done, group)

        def body(t, c, done=done, group=group):
            attend(*[done + t * group + g for g in range(group)])
            return c

        lax.fori_loop(0, trips, body, 0)
        done = done + trips * group
        group //= 2
    for h in range(heads):
        r0 = h * HEAD_ROWS
        yt_sc[h * HEAD_DIM:(h + 1) * HEAD_DIM, :] = (acc_sc[r0:r0 + HEAD_DIM, :]
                                                     / acc_sc[r0 + HEAD_DIM:r0 + HEAD_DIM + 1, :])

    y_mix = yt_sc[...].T
    y_mem = _mem_attention(qm, kbd_ref[...], vbd_ref[...])
    o_ref[...] = _out_proj_ln(x, y_mix, y_mem, wo_ref, g1_ref, b1_ref, alpha, mix_w)


def _moba_mixer(x2d, batch, w_in, kbd, vbd, wo, g1, b1, alpha):
    t, d = x2d.shape
    tm = MOBA_BLOCK
    nb = t // batch // tm
    mix_w = d - MEM_WIDTH
    heads = mix_w // HEAD_DIM
    full = lambda shape: pl.BlockSpec(shape, lambda b, j: (0,) * len(shape))
    return pl.pallas_call(
        functools.partial(_moba_mixer_kernel, alpha=alpha, mix_w=mix_w, nb=nb),
        out_shape=jax.ShapeDtypeStruct((t, d), F32),
        grid=(batch, nb),
        in_specs=[
            pl.BlockSpec((tm, d), lambda b, j: (b * nb + j, 0)),
            pl.BlockSpec((tm, d), lambda b, j: (jnp.minimum(b * nb + j + 1, batch * nb - 1), 0)),
            full(w_in.shape),
            pl.BlockSpec((None,) + kbd.shape[1:], lambda b, j: (b, 0, 0)),
            pl.BlockSpec((None,) + vbd.shape[1:], lambda b, j: (b, 0, 0)),
            full(wo.shape), full(g1.shape), full(b1.shape),
        ],
        out_specs=pl.BlockSpec((tm, d), lambda b, j: (b * nb + j, 0)),
        scratch_shapes=[
            pltpu.VMEM((w_in.shape[0], tm, IN_PROJ_CHUNK), F32),
            pltpu.VMEM((tm, d), BF16),
            pltpu.VMEM((nb, tm, mix_w), BF16),
            pltpu.VMEM((nb, heads * HEAD_ROWS, tm), BF16),
            pltpu.VMEM((heads * nb, mix_w), F32),
            pltpu.VMEM((heads, nb, tm), F32),
            pltpu.VMEM((heads, tm, 2 * HEAD_DIM), BF16),
            pltpu.VMEM((heads, 1, tm), F32),
            pltpu.VMEM((heads * HEAD_ROWS, tm), F32),
            pltpu.VMEM((mix_w, tm), F32),
        ],
        compiler_params=pltpu.CompilerParams(
            dimension_semantics=("arbitrary", "arbitrary"), vmem_limit_bytes=VMEM_LIMIT),
        name="moba_mixer",
    )(x2d, x2d, w_in, kbd, vbd, wo, g1, b1)


def _router_kernel(x_ref, whi_ref, wlo_ref, br_ref, tri_ref, info_ref, sel_ref, cnt_ref, run_sc):
    step = pl.program_id(0)

    @pl.when(step == 0)
    def _():
        run_sc[...] = jnp.zeros(run_sc.shape, F32)

    x_hi, x_lo = _split_bf16(x_ref[...])
    logits = (_dot_nt(whi_ref[...], x_hi) + _dot_nt(wlo_ref[...], x_hi) + _dot_nt(whi_ref[...], x_lo)
              + br_ref[...])[0:ROUTER_ROWS, :]
    row = lax.broadcasted_iota(jnp.int32, logits.shape, 0).astype(F32)
    cmax = lambda a: jnp.max(a, axis=0, keepdims=True)
    cmin = lambda a: jnp.min(a, axis=0, keepdims=True)
    csum = lambda a: jnp.sum(a, axis=0, keepdims=True)
    big = float(2 * ROUTER_ROWS)

    is_g = row < N_GROUPS
    gl = jnp.where(is_g, logits, NEG_INF)
    gmax = cmax(gl)
    gidx = cmin(jnp.where(gl == gmax, row, big))
    g_w = 1.0 / csum(jnp.where(is_g, jnp.exp(gl - gmax), 0.0))

    lo = ROUTER_OFF + EXPERTS_PER_GROUP * gidx
    el = jnp.where((row >= lo) & (row < lo + EXPERTS_PER_GROUP), logits, NEG_INF)
    v0 = cmax(el)
    i0 = cmin(jnp.where(el == v0, row, big))
    el1 = jnp.where(row == i0, NEG_INF, el)
    v1 = cmax(el1)
    i1 = cmin(jnp.where(el1 == v1, row, big))
    t = jnp.exp(v1 - v0)
    w0 = g_w / (1.0 + t)
    w1 = g_w * t / (1.0 + t)

    pick0 = row == i0
    pick1 = row == i1
    onehot = jnp.where(pick0 | pick1, 1.0, 0.0)
    before = _dot(onehot.astype(BF16), tri_ref[...]) + run_sc[...]
    pos0 = csum(jnp.where(pick0, before, 0.0))
    pos1 = csum(jnp.where(pick1, before, 0.0))
    run_sc[...] = run_sc[...] + jnp.sum(onehot, axis=1, keepdims=True)
    cnt_ref[...] = run_sc[...]

    vals = (i0 - ROUTER_OFF, i1 - ROUTER_OFF, pos0, pos1, w0, w1)
    row8 = lax.broadcasted_iota(jnp.int32, sel_ref.shape, 0)
    sel = jnp.zeros(sel_ref.shape, F32)
    for c, val in enumerate(vals):
        sel = jnp.where(row8 == c, val, sel)
    sel_ref[...] = sel
    rowl = lax.broadcasted_iota(jnp.int32, (ROUTER_LANES, x_ref.shape[0]), 0)
    info_ref[...] = jnp.where(rowl == 4, w0, jnp.where(rowl == 5, w1, 0.0)).T


def _router(x2d, whi, wlo, br, tri):
    t, d = x2d.shape
    tm = tri.shape[0]
    full = lambda shape: pl.BlockSpec(shape, lambda s: (0,) * len(shape))
    return pl.pallas_call(
        _router_kernel,
        out_shape=(jax.ShapeDtypeStruct((t, ROUTER_LANES), F32),
                   jax.ShapeDtypeStruct((t // tm, SUBLANES, tm), F32),
                   jax.ShapeDtypeStruct((ROUTER_ROWS, tm), F32)),
        grid=(t // tm,),
        in_specs=[pl.BlockSpec((tm, d), lambda s: (s, 0)),
                  full(whi.shape), full(wlo.shape), full(br.shape), full(tri.shape)],
        out_specs=(pl.BlockSpec((tm, ROUTER_LANES), lambda s: (s, 0)),
                   pl.BlockSpec((None, SUBLANES, tm), lambda s: (s, 0, 0)),
                   full((ROUTER_ROWS, tm))),
        scratch_shapes=[pltpu.VMEM((ROUTER_ROWS, tm), F32)],
        compiler_params=pltpu.CompilerParams(dimension_semantics=("arbitrary",)),
        name="router",
    )(x2d, whi, wlo, br, tri)


def _row_copy(src, src_row, dst, dst_row, sem):
    return pltpu.make_async_copy(src.at[pl.ds(src_row, 1), :], dst.at[pl.ds(dst_row, 1), :], sem)


def _slots_kernel(sel_ref, base_ref, o_ref):
    sel = sel_ref[...]
    g, _, tm = sel.shape
    row = lax.broadcasted_iota(jnp.int32, (g, ROUTER_ROWS, tm), 1).astype(F32)
    base = base_ref[...][None]
    csum = lambda a: jnp.sum(a, axis=1, keepdims=True)
    s0 = csum(jnp.where(row == sel[:, 0:1, :] + ROUTER_OFF, base, 0.0)) + sel[:, 2:3, :]
    s1 = csum(jnp.where(row == sel[:, 1:2, :] + ROUTER_OFF, base, 0.0)) + sel[:, 3:4, :]
    row8 = lax.broadcasted_iota(jnp.int32, o_ref.shape, 1)
    o_ref[...] = jnp.where(row8 == 0, s0, jnp.where(row8 == 1, s1, 0.0)).astype(jnp.int32)


def _slots(sel, base_col):
    nt, _, tm = sel.shape
    g = SLOT_TILES_PER_STEP if nt % SLOT_TILES_PER_STEP == 0 else nt
    return pl.pallas_call(
        _slots_kernel,
        out_shape=jax.ShapeDtypeStruct((nt, SUBLANES, tm), jnp.int32),
        grid=(nt // g,),
        in_specs=[pl.BlockSpec((g, SUBLANES, tm), lambda s: (s, 0, 0)),
                  pl.BlockSpec(base_col.shape, lambda s: (0, 0))],
        out_specs=pl.BlockSpec((g, SUBLANES, tm), lambda s: (s, 0, 0)),
        compiler_params=pltpu.CompilerParams(dimension_semantics=("arbitrary",)),
        name="slots",
    )(sel, base_col)


def _dispatch_kernel(nv_ref, s0_ref, s1_ref, x_ref, xs_ref, zero_sc, sem, fill_sem, *, tm):

    @pl.when(pl.program_id(0) == 0)
    def _():
        zero_sc[...] = jnp.zeros(zero_sc.shape, F32)

        def fill(t):
            return pltpu.make_async_copy(zero_sc, xs_ref.at[pl.ds(pl.multiple_of(t * tm, tm), tm), :], fill_sem)

        def start_fill(t, c):
            @pl.when(nv_ref[t] < tm)
            def _():
                fill(t).start()
            return c

        def wait_fill(t, c):
            @pl.when(nv_ref[t] < tm)
            def _():
                fill(t).wait()
            return c

        lax.fori_loop(0, nv_ref.shape[0], start_fill, 0)
        lax.fori_loop(0, nv_ref.shape[0], wait_fill, 0)

    rows = x_ref.shape[0]
    for g in range(s0_ref.shape[0]):
        def start(r, c, g=g):
            row = g * s0_ref.shape[2] + r
            _row_copy(x_ref, row, xs_ref, s0_ref[g, 0, r], sem).start(priority=0)
            _row_copy(x_ref, row, xs_ref, s1_ref[g, 0, r], sem).start(priority=1)
            return c

        lax.fori_loop(0, s0_ref.shape[2], start, 0, unroll=DMA_UNROLL)
    for _ in range(2):
        pltpu.make_async_copy(x_ref, xs_ref.at[pl.ds(0, rows), :], sem).wait()


def _dispatch(x2d, slot0, slot1, tile_valid):
    t, d = x2d.shape
    tm = TOKEN_TILE
    st = slot0.shape[2]
    rows = DISPATCH_ROWS if t % DISPATCH_ROWS == 0 else st
    n_slots = tile_valid.shape[0] * tm
    smem_rows = pl.BlockSpec((rows // st, 1, st), lambda s, nv: (s, 0, 0), memory_space=pltpu.SMEM)
    return pl.pallas_call(
        functools.partial(_dispatch_kernel, tm=tm),
        out_shape=jax.ShapeDtypeStruct((n_slots, d), F32),
        grid_spec=pltpu.PrefetchScalarGridSpec(
            num_scalar_prefetch=1,
            grid=(t // rows,),
            in_specs=[smem_rows, smem_rows, pl.BlockSpec((rows, d), lambda s, nv: (s, 0))],
            out_specs=pl.BlockSpec(memory_space=pl.ANY),
            scratch_shapes=[pltpu.VMEM((tm, d), F32), pltpu.SemaphoreType.DMA, pltpu.SemaphoreType.DMA],
        ),
        compiler_params=pltpu.CompilerParams(dimension_semantics=("arbitrary",)),
        name="dispatch",
    )(tile_valid, slot0, slot1, x2d)


def _expert_kernel(te_ref, nv_ref, first_ref, buf_ref, next_ref, xs_hbm, wg_hbm, wu_hbm, wd_hbm, ys_ref,
                   xs_buf, wg_buf, wu_buf, wd_buf, wg_sc, wu_sc, wd_sc, xs_sems, sems, *, base):
    t = pl.program_id(0)
    n_steps = pl.num_programs(0)
    tm = ys_ref.shape[0]
    used = nv_ref[t] > 0

    def xs_copy(tile):
        row0 = tile * tm if isinstance(tile, int) else pl.multiple_of(tile * tm, tm)
        slot = tile % XS_RING
        return pltpu.make_async_copy(xs_hbm.at[pl.ds(row0, tm), :], xs_buf.at[slot], xs_sems.at[slot])

    @pl.when(t == 0)
    def _():
        for ahead in range(XS_RING - 1):
            @pl.when(ahead < n_steps)
            def _():
                xs_copy(ahead).start()

    @pl.when(t + XS_RING - 1 < n_steps)
    def _():
        xs_copy(t + XS_RING - 1).start()

    xs_copy(t).wait()

    def weight_copies(e, b):
        return (pltpu.make_async_copy(wg_hbm.at[base + e], wg_buf.at[b], sems.at[b, 0]),
                pltpu.make_async_copy(wu_hbm.at[base + e], wu_buf.at[b], sems.at[b, 1]),
                pltpu.make_async_copy(wd_hbm.at[base + e], wd_buf.at[b], sems.at[b, 2]))

    @pl.when(t == 0)
    def _():
        for c in weight_copies(te_ref[0], 0):
            c.start()

    @pl.when(first_ref[t] == 1)
    def _():
        b = buf_ref[t]
        for c in weight_copies(te_ref[t], b):
            c.wait()
        wg_sc[...] = wg_buf[b].astype(BF16)
        wu_sc[...] = wu_buf[b].astype(BF16)
        wd_sc[...] = wd_buf[b].astype(BF16)

        @pl.when(next_ref[t] >= 0)
        def _():
            for c in weight_copies(next_ref[t], 1 - b):
                c.start()

    @pl.when(used)
    def _():
        xb = xs_buf[t % XS_RING].astype(BF16)
        hg = _dot(xb, wg_sc[...])
        hu = _dot(xb, wu_sc[...])
        h = hg * jax.nn.sigmoid(hg) * hu
        ys_ref[...] = _dot(h.astype(BF16), wd_sc[...])

    @pl.when(jnp.logical_not(used))
    def _():
        ys_ref[...] = jnp.zeros(ys_ref.shape, F32)


def _expert_mlp(xs, tile_expert, tile_valid, tile_first, tile_buf, tile_next, w_gate, w_up, w_down, layer):
    ns, d = xs.shape
    f = w_gate.shape[-1]
    tm = TOKEN_TILE
    hbm = pl.BlockSpec(memory_space=pl.ANY)
    return pl.pallas_call(
        functools.partial(_expert_kernel, base=layer * N_EXPERTS),
        out_shape=jax.ShapeDtypeStruct((ns, d), F32),
        grid_spec=pltpu.PrefetchScalarGridSpec(
            num_scalar_prefetch=5,
            grid=(ns // tm,),
            in_specs=[hbm, hbm, hbm, hbm],
            out_specs=pl.BlockSpec((tm, d), lambda t, *_: (t, 0)),
            scratch_shapes=[pltpu.VMEM((XS_RING, tm, d), F32),
                            pltpu.VMEM((2, d, f), F32), pltpu.VMEM((2, d, f), F32), pltpu.VMEM((2, f, d), F32),
                            pltpu.VMEM((d, f), BF16), pltpu.VMEM((d, f), BF16), pltpu.VMEM((f, d), BF16),
                            pltpu.SemaphoreType.DMA((XS_RING,)), pltpu.SemaphoreType.DMA((2, 3))],
        ),
        compiler_params=pltpu.CompilerParams(
            dimension_semantics=("arbitrary",), vmem_limit_bytes=VMEM_LIMIT),
        name="expert_mlp",
    )(tile_expert, tile_valid, tile_first, tile_buf, tile_next, xs, w_gate, w_up, w_down)


def _combine_kernel(s0_ref, s1_ref, n0_ref, n1_ref, info_ref, x_ref, ys_ref, g_ref, b_ref, o_ref,
                    y0_sc, y1_sc, sems, *, alpha):
    s = pl.program_id(0)
    rows = x_ref.shape[0]

    def start_gathers(t0_ref, t1_ref, buf):
        for g in range(t0_ref.shape[0]):
            def start(r, c, g=g):
                row = g * t0_ref.shape[2] + r
                _row_copy(ys_ref, t0_ref[g, 0, r], y0_sc.at[buf], row, sems.at[buf]).start(priority=0)
                _row_copy(ys_ref, t1_ref[g, 0, r], y1_sc.at[buf], row, sems.at[buf]).start(priority=1)
                return c

            lax.fori_loop(0, t0_ref.shape[2], start, 0, unroll=DMA_UNROLL)

    cur = s % 2

    @pl.when(s == 0)
    def _():
        start_gathers(s0_ref, s1_ref, 0)

    @pl.when(s + 1 < pl.num_programs(0))
    def _():
        start_gathers(n0_ref, n1_ref, 1 - cur)

    for dst in (y0_sc, y1_sc):
        pltpu.make_async_copy(ys_ref.at[pl.ds(0, rows), :], dst.at[cur], sems.at[cur]).wait()
    info = info_ref[...]
    f = info[:, 4:5] * y0_sc[cur] + info[:, 5:6] * y1_sc[cur]
    o_ref[...] = _ln(alpha * x_ref[...] + f, g_ref[...], b_ref[...])


def _combine(x2d, info, ys, slot0, slot1, g2, b2, alpha):
    t, d = x2d.shape
    st = slot0.shape[2]
    rows = COMBINE_ROWS if t % COMBINE_ROWS == 0 else st
    n = t // rows
    smem_rows = pl.BlockSpec((rows // st, 1, st), lambda s: (s, 0, 0), memory_space=pltpu.SMEM)
    smem_next = pl.BlockSpec((rows // st, 1, st), lambda s: (jnp.minimum(s + 1, n - 1), 0, 0),
                             memory_space=pltpu.SMEM)
    full = lambda shape: pl.BlockSpec(shape, lambda s: (0,) * len(shape))
    return pl.pallas_call(
        functools.partial(_combine_kernel, alpha=alpha),
        out_shape=jax.ShapeDtypeStruct((t, d), F32),
        grid=(n,),
        in_specs=[smem_rows, smem_rows, smem_next, smem_next,
                  pl.BlockSpec((rows, ROUTER_LANES), lambda s: (s, 0)),
                  pl.BlockSpec((rows, d), lambda s: (s, 0)),
                  pl.BlockSpec(memory_space=pl.ANY),
                  full(g2.shape), full(b2.shape)],
        out_specs=pl.BlockSpec((rows, d), lambda s: (s, 0)),
        scratch_shapes=[pltpu.VMEM((2, rows, d), F32), pltpu.VMEM((2, rows, d), F32),
                        pltpu.SemaphoreType.DMA((2,))],
        compiler_params=pltpu.CompilerParams(
            dimension_semantics=("arbitrary",), vmem_limit_bytes=VMEM_LIMIT),
        name="combine",
    )(slot0, slot1, slot0, slot1, info, x2d, ys, g2, b2)


def _moe(x2d, layer, w_rg, b_rg, w_re, b_re, w_gate, w_up, w_down, g2, b2, alpha):
    t, d = x2d.shape
    tm = TOKEN_TILE
    wr = jnp.zeros((ROUTER_LANES, d), F32).at[0:N_GROUPS].set(w_rg.T)
    wr = wr.at[ROUTER_OFF:ROUTER_OFF + N_EXPERTS].set(w_re.T)
    br = jnp.zeros((ROUTER_LANES, 1), F32).at[0:N_GROUPS, 0].set(b_rg)
    br = br.at[ROUTER_OFF:ROUTER_OFF + N_EXPERTS, 0].set(b_re)
    whi, wlo = _split_bf16(wr)
    rt = ROUTER_TILE if t % ROUTER_TILE == 0 else tm
    ridx = lax.broadcasted_iota(jnp.int32, (rt, rt), 0)
    cidx = lax.broadcasted_iota(jnp.int32, (rt, rt), 1)
    tri = (ridx < cidx).astype(BF16)
    info, sel, cnt = _router(x2d, whi, wlo, br, tri)

    counts = cnt[ROUTER_OFF:ROUTER_OFF + N_EXPERTS, 0].astype(jnp.int32)
    padded = (counts + tm - 1) // tm * tm
    ends = jnp.cumsum(padded)
    base = ends - padded
    base_col = jnp.zeros((ROUTER_ROWS, 1), F32).at[ROUTER_OFF:ROUTER_OFF + N_EXPERTS, 0].set(base.astype(F32))
    slots = _slots(sel, base_col)
    slot0, slot1 = slots[:, 0:1, :], slots[:, 1:2, :]

    n_slots = 2 * t + N_EXPERTS * tm
    n_tiles = n_slots // tm
    n_used = ends[-1] // tm
    tile_ids = jnp.arange(n_tiles, dtype=jnp.int32)
    tile_expert = jnp.sum((jnp.minimum(tile_ids, n_used - 1)[:, None] * tm >= ends[None, :]).astype(jnp.int32),
                          axis=1)
    lo = jnp.maximum(base[None, :], tile_ids[:, None] * tm)
    hi = jnp.minimum((base + counts)[None, :], (tile_ids[:, None] + 1) * tm)
    tile_valid = jnp.sum(jnp.maximum(hi - lo, 0), axis=1).astype(jnp.int32)

    used = tile_valid > 0
    tile_first = (used & ((tile_ids == 0) | (tile_expert != jnp.roll(tile_expert, 1)))).astype(jnp.int32)
    tile_buf = ((jnp.cumsum(tile_first) - 1) % 2).astype(jnp.int32)
    experts = jnp.arange(N_EXPERTS, dtype=jnp.int32)
    later_nonempty = (experts[None, :] > experts[:, None]) & (counts[None, :] > 0)
    next_expert = jnp.min(jnp.where(later_nonempty, experts[None, :], N_EXPERTS), axis=1)
    next_expert = jnp.where(next_expert < N_EXPERTS, next_expert, -1)
    tile_next = jnp.sum(jnp.where(tile_expert[:, None] == experts[None, :], next_expert[None, :], 0),
                        axis=1).astype(jnp.int32)

    xs = _dispatch(x2d, slot0, slot1, tile_valid)
    ys = _expert_mlp(xs, tile_expert, tile_valid, tile_first, tile_buf, tile_next, w_gate, w_up, w_down, layer)
    return _combine(x2d, info, ys, slot0, slot1, g2, b2, alpha)


def kernel(x, mem, w_mem_kv, conv_w_in, conv_dw_w, conv_dw_b, conv_ln_g, conv_ln_b, moba_w_in, w_o,
           ln1_g, ln1_b, w_rg, b_rg, w_re, b_re, w_gate, w_up, w_down, ln2_g, ln2_b):
    batch, seq, d = x.shape
    depth = w_o.shape[0]
    alpha = (2 * depth) ** 0.25
    t = batch * seq
    row = lambda a: a.reshape(1, -1)

    kbd, vbd = _memkv(mem, w_mem_kv)

    e_shape = w_gate.shape
    w_gate = w_gate.reshape((-1,) + e_shape[-2:])
    w_up = w_up.reshape((-1,) + e_shape[-2:])
    w_down = w_down.reshape((-1,) + w_down.shape[-2:])

    x2d = x.reshape(t, d)
    for i in range(depth):
        j = i // 2
        wo = w_o[i].astype(BF16)
        if i % 2 == 0:
            dw_w = jnp.broadcast_to(conv_dw_w[j][:, None, :], (CONV_WIDTH, SUBLANES, conv_dw_w.shape[-1]))
            w_in = conv_w_in[j].astype(BF16).reshape(d, -1, IN_PROJ_CHUNK).transpose(1, 0, 2)
            x2d = _conv_mixer(x2d, batch, w_in, dw_w, row(conv_dw_b[j]),
                              row(conv_ln_g[j]), row(conv_ln_b[j]), kbd, vbd, wo,
                              row(ln1_g[i]), row(ln1_b[i]), alpha)
        else:
            w_in = moba_w_in[j].astype(BF16).reshape(d, -1, IN_PROJ_CHUNK).transpose(1, 0, 2)
            x2d = _moba_mixer(x2d, batch, w_in, kbd, vbd, wo,
                              row(ln1_g[i]), row(ln1_b[i]), alpha)
        x2d = _moe(x2d, i, w_rg[i], b_rg[i], w_re[i], b_re[i], w_gate, w_up, w_down,
                   row(ln2_g[i]), row(ln2_b[i]), alpha)
    return x2d.reshape(batch, seq, d)
```

```python
import functools

import jax
import jax.numpy as jnp
from jax import lax
from jax.experimental import pallas as pl
from jax.experimental.pallas import tpu as pltpu

F32 = jnp.float32
BF16 = jnp.bfloat16

HEAD_DIM = 64
MEM_LEN = 256
MEM_HEADS = 4
MEM_WIDTH = MEM_HEADS * HEAD_DIM
CONV_WIDTH = 31
MOBA_BLOCK = 256
MOBA_TOPK = 3
HEAD_ROWS = HEAD_DIM + 16
MOBA_BLOCKS_PER_TRIP = 8
MOBA_LOOKAHEAD = 6
N_GROUPS = 4
EXPERTS_PER_GROUP = 8
N_EXPERTS = N_GROUPS * EXPERTS_PER_GROUP
LN_EPS = 1e-5

SUBLANES = 8
TOKEN_TILE = 256
CONV_HALO = 32
CONV_CHUNK = 16
CONV_CHUNKS_PER_STEP = 2
CONV_STEPS_PER_TRIP = 2
IN_PROJ_CHUNK = 256
ROUTER_LANES = 128
ROUTER_ROWS = 40
ROUTER_OFF = N_GROUPS
DMA_UNROLL = 32
ROUTER_TILE = 512
DISPATCH_ROWS = 512
COMBINE_ROWS = 512
XS_RING = 3
SLOT_TILES_PER_STEP = 16
VMEM_LIMIT = 56 * 1024 * 1024

NEG_INF = float("-inf")
LOG2_E = 1.4426950408889634


def _ln(z, g, b):
    mu = jnp.mean(z, axis=-1, keepdims=True)
    zc = z - mu
    var = jnp.mean(zc * zc, axis=-1, keepdims=True)
    return zc * lax.rsqrt(var + LN_EPS) * g + b


def _dot(a, b):
    return jnp.dot(a, b, preferred_element_type=F32)


def _dot_nt(a, b):
    return lax.dot_general(a, b, (((1,), (1,)), ((), ())), preferred_element_type=F32)


def _split_bf16(x):
    hi = x.astype(BF16)
    lo = (x - hi.astype(F32)).astype(BF16)
    return hi, lo


def _memkv_kernel(mem_ref, w_ref, kbd_ref, vbd_ref):
    kv = _dot(mem_ref[...].astype(BF16), w_ref[...].astype(BF16))
    k_t = kv[:, 0:MEM_WIDTH].T
    v = kv[:, MEM_WIDTH:]
    r = lax.broadcasted_iota(jnp.int32, kbd_ref.shape, 0)
    c = lax.broadcasted_iota(jnp.int32, kbd_ref.shape, 1)
    kbd_ref[...] = jnp.where(r // HEAD_DIM == c // MEM_LEN,
                             jnp.concatenate([k_t] * MEM_HEADS, axis=1), 0.0).astype(BF16)
    r = lax.broadcasted_iota(jnp.int32, vbd_ref.shape, 0)
    c = lax.broadcasted_iota(jnp.int32, vbd_ref.shape, 1)
    vbd_ref[...] = jnp.where(r // MEM_LEN == c // HEAD_DIM,
                             jnp.concatenate([v] * MEM_HEADS, axis=0), 0.0).astype(BF16)


def _memkv(mem, w):
    batch, m, d = mem.shape
    return pl.pallas_call(
        _memkv_kernel,
        out_shape=(jax.ShapeDtypeStruct((batch, MEM_WIDTH, MEM_HEADS * m), BF16),
                   jax.ShapeDtypeStruct((batch, MEM_HEADS * m, MEM_WIDTH), BF16)),
        grid=(batch,),
        in_specs=[pl.BlockSpec((None, m, d), lambda b: (b, 0, 0)),
                  pl.BlockSpec(w.shape, lambda b: (0, 0))],
        out_specs=(pl.BlockSpec((None, MEM_WIDTH, MEM_HEADS * m), lambda b: (b, 0, 0)),
                   pl.BlockSpec((None, MEM_HEADS * m, MEM_WIDTH), lambda b: (b, 0, 0))),
        compiler_params=pltpu.CompilerParams(dimension_semantics=("arbitrary",)),
        name="memkv",
    )(mem, w)


def _mem_attention(qm, kbd, vbd):
    s = _dot(qm.astype(BF16), kbd) * (HEAD_DIM ** -0.5)
    parts = []
    for h in range(MEM_HEADS):
        seg = s[:, h * MEM_LEN:(h + 1) * MEM_LEN]
        m = jnp.max(seg, axis=-1, keepdims=True)
        e = jnp.exp(seg - m)
        parts.append(e / jnp.sum(e, axis=-1, keepdims=True))
    p = jnp.concatenate(parts, axis=-1)
    return _dot(p.astype(BF16), vbd)


def _out_proj_ln(x, y_mix, y_mem, wo_ref, g_ref, b_ref, alpha, mix_w):
    y = _dot(y_mix.astype(BF16), wo_ref[0:mix_w, :]) + _dot(y_mem.astype(BF16), wo_ref[mix_w:, :])
    return _ln(alpha * x + y, g_ref[...], b_ref[...])


def _conv_mixer_kernel(x_ref, xn_ref, win_ref, dww_ref, dwb_ref, cg_ref, cb_ref, kbd_ref, vbd_ref,
                       wo_ref, g1_ref, b1_ref, o_ref, ustage, xn_sc, hbuf, zbuf, cbuf, *, alpha, mix_w):
    tm = x_ref.shape[0]
    n_cc = win_ref.shape[0]
    per = mix_w // IN_PROJ_CHUNK
    j = pl.program_id(1)

    @pl.when((pl.program_id(0) == 0) & (j == 0))
    def _():
        xb = x_ref[...].astype(BF16)
        for c in range(n_cc):
            ustage[c] = _dot(xb, win_ref[c])

    x = x_ref[...]
    qm = ustage[2 * per]
    h = jnp.concatenate([ustage[c] * jax.nn.sigmoid(ustage[per + c]) for c in range(per)], axis=1)
    xn_sc[...] = xn_ref[...].astype(BF16)

    @pl.when(j == 0)
    def _():
        hbuf[0:CONV_HALO, :] = jnp.zeros((CONV_HALO, mix_w), F32)

    hbuf[CONV_HALO:CONV_HALO + tm, :] = h

    shifted_rows = CONV_HALO + tm - SUBLANES
    for a in range(1, SUBLANES):
        zbuf[a - 1, 0:shifted_rows, :] = hbuf[a:a + shifted_rows, :]

    first = CONV_HALO - (CONV_WIDTH - 1)
    aligned = lambda v, m: v if isinstance(v, int) else pl.multiple_of(v, m)

    def conv_rows(r0):
        acc = jnp.broadcast_to(dwb_ref[...], (CONV_CHUNK, mix_w))
        for a in range(SUBLANES):
            offs = [first + k - a for k in range(CONV_WIDTH) if (first + k) % SUBLANES == a]
            r = aligned(r0 + offs[0], SUBLANES)
            span = offs[-1] - offs[0] + CONV_CHUNK
            z = hbuf[pl.ds(r, span), :] if a == 0 else zbuf[a - 1, pl.ds(r, span), :]
            for off in offs:
                k = off + a - first
                rows = z[off - offs[0]:off - offs[0] + CONV_CHUNK, :]
                acc = acc + jnp.tile(dww_ref[k], (CONV_CHUNK // SUBLANES, 1)) * rows
        cbuf[pl.ds(r0, CONV_CHUNK), :] = acc

    def project_next(c):
        ustage[c] = _dot(xn_sc[...], win_ref[c])

    step_rows = CONV_CHUNK * CONV_CHUNKS_PER_STEP
    paired = min(n_cc, tm // step_rows)

    def paired_step(c):
        project_next(c)
        for q in range(CONV_CHUNKS_PER_STEP):
            r0 = c * step_rows + q * CONV_CHUNK
            conv_rows(r0 if isinstance(c, int) else pl.multiple_of(r0, CONV_CHUNK))

    def trip(t, carry):
        for q in range(CONV_STEPS_PER_TRIP):
            paired_step(t * CONV_STEPS_PER_TRIP + q)
        return carry

    lax.fori_loop(0, paired // CONV_STEPS_PER_TRIP, trip, 0)

    def leftover(c, carry):
        paired_step(c)
        return carry

    first_left = paired - paired % CONV_STEPS_PER_TRIP
    lax.fori_loop(jnp.minimum(j, 0) + first_left, paired, leftover, 0)
    for r0 in range(paired * step_rows, tm, CONV_CHUNK):
        conv_rows(r0)
    for c in range(paired, n_cc):
        project_next(c)

    hbuf[0:CONV_HALO, :] = hbuf[tm:tm + CONV_HALO, :]

    cn = _ln(cbuf[...], cg_ref[...], cb_ref[...])
    y_mix = cn * jax.nn.sigmoid(cn)
    y_mem = _mem_attention(qm, kbd_ref[...], vbd_ref[...])
    o_ref[...] = _out_proj_ln(x, y_mix, y_mem, wo_ref, g1_ref, b1_ref, alpha, mix_w)


def _conv_mixer(x2d, batch, w_in, dw_w, dw_b, cg, cb, kbd, vbd, wo, g1, b1, alpha):
    t, d = x2d.shape
    tm = TOKEN_TILE
    nj = t // batch // tm
    mix_w = d - MEM_WIDTH
    full = lambda shape: pl.BlockSpec(shape, lambda b, j: (0,) * len(shape))
    return pl.pallas_call(
        functools.partial(_conv_mixer_kernel, alpha=alpha, mix_w=mix_w),
        out_shape=jax.ShapeDtypeStruct((t, d), F32),
        grid=(batch, nj),
        in_specs=[
            pl.BlockSpec((tm, d), lambda b, j: (b * nj + j, 0)),
            pl.BlockSpec((tm, d), lambda b, j: (jnp.minimum(b * nj + j + 1, batch * nj - 1), 0)),
            full(w_in.shape), full(dw_w.shape), full(dw_b.shape), full(cg.shape), full(cb.shape),
            pl.BlockSpec((None,) + kbd.shape[1:], lambda b, j: (b, 0, 0)),
            pl.BlockSpec((None,) + vbd.shape[1:], lambda b, j: (b, 0, 0)),
            full(wo.shape), full(g1.shape), full(b1.shape),
        ],
        out_specs=pl.BlockSpec((tm, d), lambda b, j: (b * nj + j, 0)),
        scratch_shapes=[pltpu.VMEM((w_in.shape[0], tm, IN_PROJ_CHUNK), F32),
                        pltpu.VMEM((tm, d), BF16),
                        pltpu.VMEM((CONV_HALO + tm, mix_w), F32),
                        pltpu.VMEM((SUBLANES - 1, CONV_HALO + tm, mix_w), F32),
                        pltpu.VMEM((tm, mix_w), F32)],
        compiler_params=pltpu.CompilerParams(
            dimension_semantics=("arbitrary", "arbitrary"), vmem_limit_bytes=VMEM_LIMIT),
        name="conv_mixer",
    )(x2d, x2d, w_in, dw_w, dw_b, cg, cb, kbd, vbd, wo, g1, b1)


def _moba_select_bias(gate, i):
    nb = gate.shape[1]
    blk = lax.broadcasted_iota(jnp.int32, gate.shape, 1).astype(F32)
    past = blk < i.astype(F32)
    gm = jnp.where(past, gate, NEG_INF)
    bias = jnp.full(gate.shape, NEG_INF, F32)
    for _ in range(MOBA_TOPK):
        top = jnp.max(gm, axis=1, keepdims=True)
        first = jnp.min(jnp.where(gm == top, blk, float(nb)), axis=1, keepdims=True)
        taken = blk == first
        bias = jnp.where(taken & past, 0.0, bias)
        gm = jnp.where(taken, NEG_INF, gm)
    return bias


def _moba_mixer_kernel(x_ref, xn_ref, win_ref, kbd_ref, vbd_ref, wo_ref, g1_ref, b1_ref, o_ref,
                       ustage, xn_sc, k_sc, vt_sc, kmt_sc, bias_sc, qh_sc, m_sc, acc_sc, yt_sc,
                       *, alpha, mix_w, nb):
    tm = x_ref.shape[0]
    heads = mix_w // HEAD_DIM
    n_cc = win_ref.shape[0]
    per = mix_w // IN_PROJ_CHUNK
    i = pl.program_id(1)

    @pl.when((pl.program_id(0) == 0) & (i == 0))
    def _():
        xb = x_ref[...].astype(BF16)
        for c in range(n_cc):
            ustage[c] = _dot(xb, win_ref[c])

    x = x_ref[...]
    part = lambda p: jnp.concatenate([ustage[p * per + c] for c in range(per)], axis=1)
    q = part(0) * (HEAD_DIM ** -0.5 * LOG2_E)
    k = part(1)
    v = part(2)
    qm = ustage[3 * per]
    xn_sc[...] = xn_ref[...].astype(BF16)

    def project_next(c):
        ustage[c] = _dot(xn_sc[...], win_ref[c])

    @pl.when(i == 0)
    def _():
        kmt_sc[...] = jnp.zeros(kmt_sc.shape, F32)

    k_sc[i] = k.astype(BF16)
    v_t = v.T.astype(BF16)
    for h in range(heads):
        vt_sc[i, h * HEAD_ROWS:h * HEAD_ROWS + HEAD_DIM, :] = v_t[h * HEAD_DIM:(h + 1) * HEAD_DIM, :]
        vt_sc[i, h * HEAD_ROWS + HEAD_DIM:(h + 1) * HEAD_ROWS, :] = jnp.ones((HEAD_ROWS - HEAD_DIM, tm), BF16)
    kmean = jnp.mean(k, axis=0, keepdims=True)
    lane = lax.broadcasted_iota(jnp.int32, (1, mix_w), 1)
    for h in range(heads):
        in_head = (lane >= h * HEAD_DIM) & (lane < (h + 1) * HEAD_DIM)
        kmt_sc[pl.ds(h * nb + i, 1), :] = jnp.where(in_head, kmean, 0.0)

    q_hi, q_lo = _split_bf16(q)
    km_hi, km_lo = _split_bf16(kmt_sc[...])
    gate_t = _dot_nt(km_hi, q_hi) + _dot_nt(km_hi, q_lo) + _dot_nt(km_lo, q_hi)
    bias_sc[...] = _moba_select_bias(gate_t.reshape(heads, nb, tm), i)

    lane_p = lax.broadcasted_iota(jnp.int32, (tm, 2 * HEAD_DIM), 1)
    for h in range(heads):
        qp = q_hi[:, (h // 2) * 2 * HEAD_DIM:(h // 2 + 1) * 2 * HEAD_DIM]
        keep = (lane_p < HEAD_DIM) if h % 2 == 0 else (lane_p >= HEAD_DIM)
        qh_sc[h] = jnp.where(keep, qp, jnp.zeros_like(qp))

    def scores(j, h):
        cols = slice((h // 2) * 2 * HEAD_DIM, (h // 2 + 1) * 2 * HEAD_DIM)
        return _dot_nt(k_sc[j, :, cols], qh_sc[h])

    rows = lambda h: slice(h * HEAD_ROWS, (h + 1) * HEAD_ROWS)

    kidx = lax.broadcasted_iota(jnp.int32, (tm, tm), 0)
    qidx = lax.broadcasted_iota(jnp.int32, (tm, tm), 1)
    causal = kidx <= qidx
    def heads_pipelined(*blocks):
        items = [(j, h) for j in blocks for h in range(heads)]
        pending = [scores(j, h) for j, h in items[:MOBA_LOOKAHEAD]]
        for n, (j, h) in enumerate(items):
            if n + MOBA_LOOKAHEAD < len(items):
                pending.append(scores(*items[n + MOBA_LOOKAHEAD]))
            yield j, h, pending.pop(0)

    def own_block(j, c):
        for _, h, s in heads_pipelined(j):
            if h < n_cc:
                project_next(h)
            s = jnp.where(causal, s, NEG_INF)
            m = jnp.max(s, axis=0, keepdims=True)
            e = jnp.exp2(s - m)
            m_sc[h] = m
            acc_sc[rows(h), :] = _dot(vt_sc[j, rows(h), :], e.astype(BF16))
        return c

    lax.fori_loop(i, i + 1, own_block, 0)
    for c in range(heads, n_cc):
        project_next(c)

    def attend(*blocks):
        for j, h, s in heads_pipelined(*blocks):
            b = bias_sc[h, pl.ds(j, 1), :]
            m_old = m_sc[h]
            m_new = jnp.maximum(m_old, jnp.max(s, axis=0, keepdims=True) + b)
            e = jnp.exp2(s - (m_new - b))
            corr = jnp.exp2(m_old - m_new)
            m_sc[h] = m_new
            acc_sc[rows(h), :] = corr * acc_sc[rows(h), :] + _dot(vt_sc[j, rows(h), :], e.astype(BF16))

    done = 0
    group = MOBA_BLOCKS_PER_TRIP
    while group >= 1:
        trips = lax.div(i - done, group)

        def body(t, c, done=done, group=group):
            attend(*[done + t * group + g for g in range(group)])
            return c

        lax.fori_loop(0, trips, body, 0)
        done = done + trips * group
        group //= 2
    for h in range(heads):
        r0 = h * HEAD_ROWS
        yt_sc[h * HEAD_DIM:(h + 1) * HEAD_DIM, :] = (acc_sc[r0:r0 + HEAD_DIM, :]
                                                     / acc_sc[r0 + HEAD_DIM:r0 + HEAD_DIM + 1, :])

    y_mix = yt_sc[...].T
    y_mem = _mem_attention(qm, kbd_ref[...], vbd_ref[...])
    o_ref[...] = _out_proj_ln(x, y_mix, y_mem, wo_ref, g1_ref, b1_ref, alpha, mix_w)


def _moba_mixer(x2d, batch, w_in, kbd, vbd, wo, g1, b1, alpha):
    t, d = x2d.shape
    tm = MOBA_BLOCK
    nb = t // batch // tm
    mix_w = d - MEM_WIDTH
    heads = mix_w // HEAD_DIM
    full = lambda shape: pl.BlockSpec(shape, lambda b, j: (0,) * len(shape))
    return pl.pallas_call(
        functools.partial(_moba_mixer_kernel, alpha=alpha, mix_w=mix_w, nb=nb),
        out_shape=jax.ShapeDtypeStruct((t, d), F32),
        grid=(batch, nb),
        in_specs=[
            pl.BlockSpec((tm, d), lambda b, j: (b * nb + j, 0)),
            pl.BlockSpec((tm, d), lambda b, j: (jnp.minimum(b * nb + j + 1, batch * nb - 1), 0)),
            full(w_in.shape),
            pl.BlockSpec((None,) + kbd.shape[1:], lambda b, j: (b, 0, 0)),
            pl.BlockSpec((None,) + vbd.shape[1:], lambda b, j: (b, 0, 0)),
            full(wo.shape), full(g1.shape), full(b1.shape),
        ],
        out_specs=pl.BlockSpec((tm, d), lambda b, j: (b * nb + j, 0)),
        scratch_shapes=[
            pltpu.VMEM((w_in.shape[0], tm, IN_PROJ_CHUNK), F32),
            pltpu.VMEM((tm, d), BF16),
            pltpu.VMEM((nb, tm, mix_w), BF16),
            pltpu.VMEM((nb, heads * HEAD_ROWS, tm), BF16),
            pltpu.VMEM((heads * nb, mix_w), F32),
            pltpu.VMEM((heads, nb, tm), F32),
            pltpu.VMEM((heads, tm, 2 * HEAD_DIM), BF16),
            pltpu.VMEM((heads, 1, tm), F32),
            pltpu.VMEM((heads * HEAD_ROWS, tm), F32),
            pltpu.VMEM((mix_w, tm), F32),
        ],
        compiler_params=pltpu.CompilerParams(
            dimension_semantics=("arbitrary", "arbitrary"), vmem_limit_bytes=VMEM_LIMIT),
        name="moba_mixer",
    )(x2d, x2d, w_in, kbd, vbd, wo, g1, b1)


def _router_kernel(x_ref, whi_ref, wlo_ref, br_ref, tri_ref, info_ref, sel_ref, cnt_ref, run_sc):
    step = pl.program_id(0)

    @pl.when(step == 0)
    def _():
        run_sc[...] = jnp.zeros(run_sc.shape, F32)

    x_hi, x_lo = _split_bf16(x_ref[...])
    logits = (_dot_nt(whi_ref[...], x_hi) + _dot_nt(wlo_ref[...], x_hi) + _dot_nt(whi_ref[...], x_lo)
              + br_ref[...])[0:ROUTER_ROWS, :]
    row = lax.broadcasted_iota(jnp.int32, logits.shape, 0).astype(F32)
    cmax = lambda a: jnp.max(a, axis=0, keepdims=True)
    cmin = lambda a: jnp.min(a, axis=0, keepdims=True)
    csum = lambda a: jnp.sum(a, axis=0, keepdims=True)
    big = float(2 * ROUTER_ROWS)

    is_g = row < N_GROUPS
    gl = jnp.where(is_g, logits, NEG_INF)
    gmax = cmax(gl)
    gidx = cmin(jnp.where(gl == gmax, row, big))
    g_w = 1.0 / csum(jnp.where(is_g, jnp.exp(gl - gmax), 0.0))

    lo = ROUTER_OFF + EXPERTS_PER_GROUP * gidx
    el = jnp.where((row >= lo) & (row < lo + EXPERTS_PER_GROUP), logits, NEG_INF)
    v0 = cmax(el)
    i0 = cmin(jnp.where(el == v0, row, big))
    el1 = jnp.where(row == i0, NEG_INF, el)
    v1 = cmax(el1)
    i1 = cmin(jnp.where(el1 == v1, row, big))
    t = jnp.exp(v1 - v0)
    w0 = g_w / (1.0 + t)
    w1 = g_w * t / (1.0 + t)

    pick0 = row == i0
    pick1 = row == i1
    onehot = jnp.where(pick0 | pick1, 1.0, 0.0)
    before = _dot(onehot.astype(BF16), tri_ref[...]) + run_sc[...]
    pos0 = csum(jnp.where(pick0, before, 0.0))
    pos1 = csum(jnp.where(pick1, before, 0.0))
    run_sc[...] = run_sc[...] + jnp.sum(onehot, axis=1, keepdims=True)
    cnt_ref[...] = run_sc[...]

    vals = (i0 - ROUTER_OFF, i1 - ROUTER_OFF, pos0, pos1, w0, w1)
    row8 = lax.broadcasted_iota(jnp.int32, sel_ref.shape, 0)
    sel = jnp.zeros(sel_ref.shape, F32)
    for c, val in enumerate(vals):
        sel = jnp.where(row8 == c, val, sel)
    sel_ref[...] = sel
    rowl = lax.broadcasted_iota(jnp.int32, (ROUTER_LANES, x_ref.shape[0]), 0)
    info_ref[...] = jnp.where(rowl == 4, w0, jnp.where(rowl == 5, w1, 0.0)).T


def _router(x2d, whi, wlo, br, tri):
    t, d = x2d.shape
    tm = tri.shape[0]
    full = lambda shape: pl.BlockSpec(shape, lambda s: (0,) * len(shape))
    return pl.pallas_call(
        _router_kernel,
        out_shape=(jax.ShapeDtypeStruct((t, ROUTER_LANES), F32),
                   jax.ShapeDtypeStruct((t // tm, SUBLANES, tm), F32),
                   jax.ShapeDtypeStruct((ROUTER_ROWS, tm), F32)),
        grid=(t // tm,),
        in_specs=[pl.BlockSpec((tm, d), lambda s: (s, 0)),
                  full(whi.shape), full(wlo.shape), full(br.shape), full(tri.shape)],
        out_specs=(pl.BlockSpec((tm, ROUTER_LANES), lambda s: (s, 0)),
                   pl.BlockSpec((None, SUBLANES, tm), lambda s: (s, 0, 0)),
                   full((ROUTER_ROWS, tm))),
        scratch_shapes=[pltpu.VMEM((ROUTER_ROWS, tm), F32)],
        compiler_params=pltpu.CompilerParams(dimension_semantics=("arbitrary",)),
        name="router",
    )(x2d, whi, wlo, br, tri)


def _row_copy(src, src_row, dst, dst_row, sem):
    return pltpu.make_async_copy(src.at[pl.ds(src_row, 1), :], dst.at[pl.ds(dst_row, 1), :], sem)


def _slots_kernel(sel_ref, base_ref, o_ref):
    sel = sel_ref[...]
    g, _, tm = sel.shape
    row = lax.broadcasted_iota(jnp.int32, (g, ROUTER_ROWS, tm), 1).astype(F32)
    base = base_ref[...][None]
    csum = lambda a: jnp.sum(a, axis=1, keepdims=True)
    s0 = csum(jnp.where(row == sel[:, 0:1, :] + ROUTER_OFF, base, 0.0)) + sel[:, 2:3, :]
    s1 = csum(jnp.where(row == sel[:, 1:2, :] + ROUTER_OFF, base, 0.0)) + sel[:, 3:4, :]
    row8 = lax.broadcasted_iota(jnp.int32, o_ref.shape, 1)
    o_ref[...] = jnp.where(row8 == 0, s0, jnp.where(row8 == 1, s1, 0.0)).astype(jnp.int32)


def _slots(sel, base_col):
    nt, _, tm = sel.shape
    g = SLOT_TILES_PER_STEP if nt % SLOT_TILES_PER_STEP == 0 else nt
    return pl.pallas_call(
        _slots_kernel,
        out_shape=jax.ShapeDtypeStruct((nt, SUBLANES, tm), jnp.int32),
        grid=(nt // g,),
        in_specs=[pl.BlockSpec((g, SUBLANES, tm), lambda s: (s, 0, 0)),
                  pl.BlockSpec(base_col.shape, lambda s: (0, 0))],
        out_specs=pl.BlockSpec((g, SUBLANES, tm), lambda s: (s, 0, 0)),
        compiler_params=pltpu.CompilerParams(dimension_semantics=("arbitrary",)),
        name="slots",
    )(sel, base_col)


def _dispatch_kernel(nv_ref, s0_ref, s1_ref, x_ref, xs_ref, zero_sc, sem, fill_sem, *, tm):

    @pl.when(pl.program_id(0) == 0)
    def _():
        zero_sc[...] = jnp.zeros(zero_sc.shape, F32)

        def fill(t):
            return pltpu.make_async_copy(zero_sc, xs_ref.at[pl.ds(pl.multiple_of(t * tm, tm), tm), :], fill_sem)

        def start_fill(t, c):
            @pl.when(nv_ref[t] < tm)
            def _():
                fill(t).start()
            return c

        def wait_fill(t, c):
            @pl.when(nv_ref[t] < tm)
            def _():
                fill(t).wait()
            return c

        lax.fori_loop(0, nv_ref.shape[0], start_fill, 0)
        lax.fori_loop(0, nv_ref.shape[0], wait_fill, 0)

    rows = x_ref.shape[0]
    for g in range(s0_ref.shape[0]):
        def start(r, c, g=g):
            row = g * s0_ref.shape[2] + r
            _row_copy(x_ref, row, xs_ref, s0_ref[g, 0, r], sem).start(priority=0)
            _row_copy(x_ref, row, xs_ref, s1_ref[g, 0, r], sem).start(priority=1)
            return c

        lax.fori_loop(0, s0_ref.shape[2], start, 0, unroll=DMA_UNROLL)
    for _ in range(2):
        pltpu.make_async_copy(x_ref, xs_ref.at[pl.ds(0, rows), :], sem).wait()


def _dispatch(x2d, slot0, slot1, tile_valid):
    t, d = x2d.shape
    tm = TOKEN_TILE
    st = slot0.shape[2]
    rows = DISPATCH_ROWS if t % DISPATCH_ROWS == 0 else st
    n_slots = tile_valid.shape[0] * tm
    smem_rows = pl.BlockSpec((rows // st, 1, st), lambda s, nv: (s, 0, 0), memory_space=pltpu.SMEM)
    return pl.pallas_call(
        functools.partial(_dispatch_kernel, tm=tm),
        out_shape=jax.ShapeDtypeStruct((n_slots, d), F32),
        grid_spec=pltpu.PrefetchScalarGridSpec(
            num_scalar_prefetch=1,
            grid=(t // rows,),
            in_specs=[smem_rows, smem_rows, pl.BlockSpec((rows, d), lambda s, nv: (s, 0))],
            out_specs=pl.BlockSpec(memory_space=pl.ANY),
            scratch_shapes=[pltpu.VMEM((tm, d), F32), pltpu.SemaphoreType.DMA, pltpu.SemaphoreType.DMA],
        ),
        compiler_params=pltpu.CompilerParams(dimension_semantics=("arbitrary",)),
        name="dispatch",
    )(tile_valid, slot0, slot1, x2d)


def _expert_kernel(te_ref, nv_ref, first_ref, buf_ref, next_ref, xs_hbm, wg_hbm, wu_hbm, wd_hbm, ys_ref,
                   xs_buf, wg_buf, wu_buf, wd_buf, wg_sc, wu_sc, wd_sc, xs_sems, sems, *, base):
    t = pl.program_id(0)
    n_steps = pl.num_programs(0)
    tm = ys_ref.shape[0]
    used = nv_ref[t] > 0

    def xs_copy(tile):
        row0 = tile * tm if isinstance(tile, int) else pl.multiple_of(tile * tm, tm)
        slot = tile % XS_RING
        return pltpu.make_async_copy(xs_hbm.at[pl.ds(row0, tm), :], xs_buf.at[slot], xs_sems.at[slot])

    @pl.when(t == 0)
    def _():
        for ahead in range(XS_RING - 1):
            @pl.when(ahead < n_steps)
            def _():
                xs_copy(ahead).start()

    @pl.when(t + XS_RING - 1 < n_steps)
    def _():
        xs_copy(t + XS_RING - 1).start()

    xs_copy(t).wait()

    def weight_copies(e, b):
        return (pltpu.make_async_copy(wg_hbm.at[base + e], wg_buf.at[b], sems.at[b, 0]),
                pltpu.make_async_copy(wu_hbm.at[base + e], wu_buf.at[b], sems.at[b, 1]),
                pltpu.make_async_copy(wd_hbm.at[base + e], wd_buf.at[b], sems.at[b, 2]))

    @pl.when(t == 0)
    def _():
        for c in weight_copies(te_ref[0], 0):
            c.start()

    @pl.when(first_ref[t] == 1)
    def _():
        b = buf_ref[t]
        for c in weight_copies(te_ref[t], b):
            c.wait()
        wg_sc[...] = wg_buf[b].astype(BF16)
        wu_sc[...] = wu_buf[b].astype(BF16)
        wd_sc[...] = wd_buf[b].astype(BF16)

        @pl.when(next_ref[t] >= 0)
        def _():
            for c in weight_copies(next_ref[t], 1 - b):
                c.start()

    @pl.when(used)
    def _():
        xb = xs_buf[t % XS_RING].astype(BF16)
        hg = _dot(xb, wg_sc[...])
        hu = _dot(xb, wu_sc[...])
        h = hg * jax.nn.sigmoid(hg) * hu
        ys_ref[...] = _dot(h.astype(BF16), wd_sc[...])

    @pl.when(jnp.logical_not(used))
    def _():
        ys_ref[...] = jnp.zeros(ys_ref.shape, F32)


def _expert_mlp(xs, tile_expert, tile_valid, tile_first, tile_buf, tile_next, w_gate, w_up, w_down, layer):
    ns, d = xs.shape
    f = w_gate.shape[-1]
    tm = TOKEN_TILE
    hbm = pl.BlockSpec(memory_space=pl.ANY)
    return pl.pallas_call(
        functools.partial(_expert_kernel, base=layer * N_EXPERTS),
        out_shape=jax.ShapeDtypeStruct((ns, d), F32),
        grid_spec=pltpu.PrefetchScalarGridSpec(
            num_scalar_prefetch=5,
            grid=(ns // tm,),
            in_specs=[hbm, hbm, hbm, hbm],
            out_specs=pl.BlockSpec((tm, d), lambda t, *_: (t, 0)),
            scratch_shapes=[pltpu.VMEM((XS_RING, tm, d), F32),
                            pltpu.VMEM((2, d, f), F32), pltpu.VMEM((2, d, f), F32), pltpu.VMEM((2, f, d), F32),
                            pltpu.VMEM((d, f), BF16), pltpu.VMEM((d, f), BF16), pltpu.VMEM((f, d), BF16),
                            pltpu.SemaphoreType.DMA((XS_RING,)), pltpu.SemaphoreType.DMA((2, 3))],
        ),
        compiler_params=pltpu.CompilerParams(
            dimension_semantics=("arbitrary",), vmem_limit_bytes=VMEM_LIMIT),
        name="expert_mlp",
    )(tile_expert, tile_valid, tile_first, tile_buf, tile_next, xs, w_gate, w_up, w_down)


def _combine_kernel(s0_ref, s1_ref, n0_ref, n1_ref, info_ref, x_ref, ys_ref, g_ref, b_ref, o_ref,
                    y0_sc, y1_sc, sems, *, alpha):
    s = pl.program_id(0)
    rows = x_ref.shape[0]

    def start_gathers(t0_ref, t1_ref, buf):
        for g in range(t0_ref.shape[0]):
            def start(r, c, g=g):
                row = g * t0_ref.shape[2] + r
                _row_copy(ys_ref, t0_ref[g, 0, r], y0_sc.at[buf], row, sems.at[buf]).start(priority=0)
                _row_copy(ys_ref, t1_ref[g, 0, r], y1_sc.at[buf], row, sems.at[buf]).start(priority=1)
                return c

            lax.fori_loop(0, t0_ref.shape[2], start, 0, unroll=DMA_UNROLL)

    cur = s % 2

    @pl.when(s == 0)
    def _():
        start_gathers(s0_ref, s1_ref, 0)

    @pl.when(s + 1 < pl.num_programs(0))
    def _():
        start_gathers(n0_ref, n1_ref, 1 - cur)

    for dst in (y0_sc, y1_sc):
        pltpu.make_async_copy(ys_ref.at[pl.ds(0, rows), :], dst.at[cur], sems.at[cur]).wait()
    info = info_ref[...]
    f = info[:, 4:5] * y0_sc[cur] + info[:, 5:6] * y1_sc[cur]
    o_ref[...] = _ln(alpha * x_ref[...] + f, g_ref[...], b_ref[...])


def _combine(x2d, info, ys, slot0, slot1, g2, b2, alpha):
    t, d = x2d.shape
    st = slot0.shape[2]
    rows = COMBINE_ROWS if t % COMBINE_ROWS == 0 else st
    n = t // rows
    smem_rows = pl.BlockSpec((rows // st, 1, st), lambda s: (s, 0, 0), memory_space=pltpu.SMEM)
    smem_next = pl.BlockSpec((rows // st, 1, st), lambda s: (jnp.minimum(s + 1, n - 1), 0, 0),
                             memory_space=pltpu.SMEM)
    full = lambda shape: pl.BlockSpec(shape, lambda s: (0,) * len(shape))
    return pl.pallas_call(
        functools.partial(_combine_kernel, alpha=alpha),
        out_shape=jax.ShapeDtypeStruct((t, d), F32),
        grid=(n,),
        in_specs=[smem_rows, smem_rows, smem_next, smem_next,
                  pl.BlockSpec((rows, ROUTER_LANES), lambda s: (s, 0)),
                  pl.BlockSpec((rows, d), lambda s: (s, 0)),
                  pl.BlockSpec(memory_space=pl.ANY),
                  full(g2.shape), full(b2.shape)],
        out_specs=pl.BlockSpec((rows, d), lambda s: (s, 0)),
        scratch_shapes=[pltpu.VMEM((2, rows, d), F32), pltpu.VMEM((2, rows, d), F32),
                        pltpu.SemaphoreType.DMA((2,))],
        compiler_params=pltpu.CompilerParams(
            dimension_semantics=("arbitrary",), vmem_limit_bytes=VMEM_LIMIT),
        name="combine",
    )(slot0, slot1, slot0, slot1, info, x2d, ys, g2, b2)


def _moe(x2d, layer, w_rg, b_rg, w_re, b_re, w_gate, w_up, w_down, g2, b2, alpha):
    t, d = x2d.shape
    tm = TOKEN_TILE
    wr = jnp.zeros((ROUTER_LANES, d), F32).at[0:N_GROUPS].set(w_rg.T)
    wr = wr.at[ROUTER_OFF:ROUTER_OFF + N_EXPERTS].set(w_re.T)
    br = jnp.zeros((ROUTER_LANES, 1), F32).at[0:N_GROUPS, 0].set(b_rg)
    br = br.at[ROUTER_OFF:ROUTER_OFF + N_EXPERTS, 0].set(b_re)
    whi, wlo = _split_bf16(wr)
    rt = ROUTER_TILE if t % ROUTER_TILE == 0 else tm
    ridx = lax.broadcasted_iota(jnp.int32, (rt, rt), 0)
    cidx = lax.broadcasted_iota(jnp.int32, (rt, rt), 1)
    tri = (ridx < cidx).astype(BF16)
    info, sel, cnt = _router(x2d, whi, wlo, br, tri)

    counts = cnt[ROUTER_OFF:ROUTER_OFF + N_EXPERTS, 0].astype(jnp.int32)
    padded = (counts + tm - 1) // tm * tm
    ends = jnp.cumsum(padded)
    base = ends - padded
    base_col = jnp.zeros((ROUTER_ROWS, 1), F32).at[ROUTER_OFF:ROUTER_OFF + N_EXPERTS, 0].set(base.astype(F32))
    slots = _slots(sel, base_col)
    slot0, slot1 = slots[:, 0:1, :], slots[:, 1:2, :]

    n_slots = 2 * t + N_EXPERTS * tm
    n_tiles = n_slots // tm
    n_used = ends[-1] // tm
    tile_ids = jnp.arange(n_tiles, dtype=jnp.int32)
    tile_expert = jnp.sum((jnp.minimum(tile_ids, n_used - 1)[:, None] * tm >= ends[None, :]).astype(jnp.int32),
                          axis=1)
    lo = jnp.maximum(base[None, :], tile_ids[:, None] * tm)
    hi = jnp.minimum((base + counts)[None, :], (tile_ids[:, None] + 1) * tm)
    tile_valid = jnp.sum(jnp.maximum(hi - lo, 0), axis=1).astype(jnp.int32)

    used = tile_valid > 0
    tile_first = (used & ((tile_ids == 0) | (tile_expert != jnp.roll(tile_expert, 1)))).astype(jnp.int32)
    tile_buf = ((jnp.cumsum(tile_first) - 1) % 2).astype(jnp.int32)
    experts = jnp.arange(N_EXPERTS, dtype=jnp.int32)
    later_nonempty = (experts[None, :] > experts[:, None]) & (counts[None, :] > 0)
    next_expert = jnp.min(jnp.where(later_nonempty, experts[None, :], N_EXPERTS), axis=1)
    next_expert = jnp.where(next_expert < N_EXPERTS, next_expert, -1)
    tile_next = jnp.sum(jnp.where(tile_expert[:, None] == experts[None, :], next_expert[None, :], 0),
                        axis=1).astype(jnp.int32)

    xs = _dispatch(x2d, slot0, slot1, tile_valid)
    ys = _expert_mlp(xs, tile_expert, tile_valid, tile_first, tile_buf, tile_next, w_gate, w_up, w_down, layer)
    return _combine(x2d, info, ys, slot0, slot1, g2, b2, alpha)


def kernel(x, mem, w_mem_kv, conv_w_in, conv_dw_w, conv_dw_b, conv_ln_g, conv_ln_b, moba_w_in, w_o,
           ln1_g, ln1_b, w_rg, b_rg, w_re, b_re, w_gate, w_up, w_down, ln2_g, ln2_b):
    batch, seq, d = x.shape
    depth = w_o.shape[0]
    alpha = (2 * depth) ** 0.25
    t = batch * seq
    row = lambda a: a.reshape(1, -1)

    kbd, vbd = _memkv(mem, w_mem_kv)

    e_shape = w_gate.shape
    w_gate = w_gate.reshape((-1,) + e_shape[-2:])
    w_up = w_up.reshape((-1,) + e_shape[-2:])
    w_down = w_down.reshape((-1,) + w_down.shape[-2:])

    x2d = x.reshape(t, d)
    for i in range(depth):
        j = i // 2
        wo = w_o[i].astype(BF16)
        if i % 2 == 0:
            dw_w = jnp.broadcast_to(conv_dw_w[j][:, None, :], (CONV_WIDTH, SUBLANES, conv_dw_w.shape[-1]))
            w_in = conv_w_in[j].astype(BF16).reshape(d, -1, IN_PROJ_CHUNK).transpose(1, 0, 2)
            x2d = _conv_mixer(x2d, batch, w_in, dw_w, row(conv_dw_b[j]),
                              row(conv_ln_g[j]), row(conv_ln_b[j]), kbd, vbd, wo,
                              row(ln1_g[i]), row(ln1_b[i]), alpha)
        else:
            w_in = moba_w_in[j].astype(BF16).reshape(d, -1, IN_PROJ_CHUNK).transpose(1, 0, 2)
            x2d = _moba_mixer(x2d, batch, w_in, kbd, vbd, wo,
                              row(ln1_g[i]), row(ln1_b[i]), alpha)
        x2d = _moe(x2d, i, w_rg[i], b_rg[i], w_re[i], b_re[i], w_gate, w_up, w_down,
                   row(ln2_g[i]), row(ln2_b[i]), alpha)
    return x2d.reshape(batch, seq, d)
```
